```python
import math
import jax
import jax.numpy as jnp
from jax import lax
import numpy as np

D_MODEL = 1024
BATCH = 8
SEQ = 2048
DEPTH = 4

GRID_W = 64
CTX_LEN = 256
F32 = jnp.float32

N_EVEN = (DEPTH + 1) // 2
N_ODD = DEPTH // 2

A_HEADS = 4
A_DK = 128
A_DV = 128
A_WIDTH = A_HEADS * A_DK
A_CHUNK = 64
A_IN = 5 * A_WIDTH

B_HEADS = 8
B_KV_HEADS = 2
B_HEAD_DIM = 64
B_QW = B_HEADS * B_HEAD_DIM
B_KVW = B_KV_HEADS * B_HEAD_DIM
B_IN = B_QW + 2 * B_KVW
B_WINDOW = 128
B_BLOCK = 128
ROPE_BASE = 10000.0

EVEN_IN = A_IN + B_IN
EVEN_OUT = A_HEADS * A_DV + B_QW

C_GROUP = 16
C_GROUPS = 32
C_WIDTH = C_GROUP * C_GROUPS
C_STATE = 64

R_HEADS = 8
R_HEAD = 64
R_WIDTH = R_HEADS * R_HEAD
R_LORA_W = 64
R_LORA_A = 64
R_LORA_G = 128
R_SPLITS = (R_WIDTH, R_WIDTH, R_WIDTH, R_LORA_W, R_LORA_W, R_LORA_A, R_LORA_A, R_LORA_G)
R_IN = sum(R_SPLITS)

ODD_IN = C_WIDTH + R_IN
ODD_OUT = C_WIDTH + R_WIDTH

D_FF = 2816
N_EXPERTS = 8
TOP_K = 2
E_FF = 2816

DEEPNORM_ALPHA = (2 * DEPTH) ** 0.25
DEEPNORM_BETA = (8 * DEPTH) ** -0.25
LN_EPS = 1e-5
RWKV_GN_EPS = 64e-5

kernel_name = "hybrid_hgrn2_swa_s5_rwkv7_moe_dit"


def _split(x, sizes):
    idx = np.cumsum(sizes)[:-1].tolist()
    return jnp.split(x, idx, axis=-1)


def layer_norm(x, g, b):
    xf = x.astype(F32)
    mu = jnp.mean(xf, -1, keepdims=True)
    var = jnp.mean(jnp.square(xf - mu), -1, keepdims=True)
    return ((xf - mu) * lax.rsqrt(var + LN_EPS)).astype(x.dtype) * g + b


def axial_rope_tables(rows):
    row = jnp.repeat(jnp.arange(rows, dtype=F32), GRID_W)
    col = jnp.tile(jnp.arange(GRID_W, dtype=F32), rows)
    n_freq = B_HEAD_DIM // 4
    inv = ROPE_BASE ** (-jnp.arange(n_freq, dtype=F32) / n_freq)
    ang = jnp.concatenate([row[:, None] * inv, col[:, None] * inv], axis=-1)
    return jnp.cos(ang), jnp.sin(ang)


def apply_rope(x, cos, sin):
    half = x.shape[-1] // 2
    x1, x2 = x[..., :half], x[..., half:]
    cos = cos.astype(x.dtype)
    sin = sin.astype(x.dtype)
    return jnp.concatenate([x1 * cos - x2 * sin, x1 * sin + x2 * cos], axis=-1)


def _gla_chunk_scan(q, k, v, logf, s0):
    bsz, h, L, _ = q.shape
    n = L // A_CHUNK

    def to_chunks(t):
        return t.reshape(bsz, h, n, A_CHUNK, t.shape[-1]).transpose(2, 0, 1, 3, 4)

    tri = jnp.tril(jnp.ones((A_CHUNK, A_CHUNK), bool))[:, :, None]

    def step(s, inp):
        qi, ki, vi, gi = inp
        b = jnp.cumsum(gi, axis=-2)
        diff = b[..., :, None, :] - b[..., None, :, :]
        decay = jnp.where(tri, jnp.exp(jnp.where(tri, diff, 0.0)), 0.0)
        att = jnp.einsum('bhtd,bhsd,bhtsd->bhts', qi, ki, decay)
        o = att @ vi + jnp.einsum('bhtd,bhdv->bhtv', qi * jnp.exp(b), s)
        b_last = b[..., -1:, :]
        s_new = jnp.exp(b_last[..., 0, :])[..., None] * s + jnp.einsum(
            'bhsd,bhsv->bhdv', ki * jnp.exp(b_last - b), vi)
        return s_new, o

    s_fin, o = lax.scan(step, s0, (to_chunks(q), to_chunks(k), to_chunks(v), to_chunks(logf)))
    o = o.transpose(1, 2, 0, 3, 4).reshape(bsz, h, L, -1)
    return o, s_fin


def hgrn2_mixer(pc, pl, lb, norm_g, need_ctx):
    def heads(t):
        b, l, _ = t.shape
        return t.astype(F32).reshape(b, l, A_HEADS, -1).transpose(0, 2, 1, 3)

    def prep(p):
        q, f_fw, f_bw, i, g = _split(p, (A_WIDTH,) * 5)
        q = heads(jax.nn.silu(q)) * A_DK ** -0.5
        dirs = []
        for fl in (f_fw, f_bw):
            f = lb + (1.0 - lb) * jax.nn.sigmoid(fl.astype(F32))
            dirs.append((heads(1.0 - f), heads(jnp.log(f))))
        return q, heads(i), dirs, g

    qc, ic, dc, gc = prep(pc)
    ql, il, dl, gl = prep(pl)
    zero = jnp.zeros((pl.shape[0], A_HEADS, A_DK, A_DV), F32)
    rev = lambda t: jnp.flip(t, axis=2)
    oc_f, sc_f = _gla_chunk_scan(qc, dc[0][0], ic, dc[0][1], zero)
    ol_f, _ = _gla_chunk_scan(ql, dl[0][0], il, dl[0][1], sc_f)
    oc_b, sc_b = _gla_chunk_scan(rev(qc), rev(dc[1][0]), rev(ic), rev(dc[1][1]), zero)
    ol_b, _ = _gla_chunk_scan(rev(ql), rev(dl[1][0]), rev(il), rev(dl[1][1]), sc_b)

    def finish(o, g):
        o = o * lax.rsqrt(jnp.mean(jnp.square(o), -1, keepdims=True) + 1e-6)
        b, h, l, dv = o.shape
        o = o.transpose(0, 2, 1, 3).reshape(b, l, h * dv).astype(g.dtype)
        return o * norm_g * jax.nn.silu(g)

    out_l = finish(ol_f + rev(ol_b), gl)
    out_c = finish(oc_f + rev(oc_b), gc) if need_ctx else None
    return out_c, out_l


def window_gqa_mixer(pc, pl, sink, cos, sin, need_ctx):
    bsz, L, _ = pl.shape
    Lc = pc.shape[1]
    G, R, dh = B_KV_HEADS, B_HEADS // B_KV_HEADS, B_HEAD_DIM
    scale = dh ** -0.5

    def qkv(p):
        b, l, _ = p.shape
        q, k, v = _split(p, (B_QW, B_KVW, B_KVW))
        return q.reshape(b, l, G, R, dh), k.reshape(b, l, G, dh), v.reshape(b, l, G, dh)

    qc, kc, vc = qkv(pc)
    ql, kl, vl = qkv(pl)
    ql = apply_rope(ql, cos[:, None, None, :], sin[:, None, None, :])
    kl = apply_rope(kl, cos[:, None, :], sin[:, None, :])
    sink_gr = sink.astype(F32).reshape(G, R)

    nb = L // B_BLOCK
    qb = ql.reshape(bsz, nb, B_BLOCK, G, R, dh)

    def band(t):
        tp = jnp.pad(t, ((0, 0), (B_BLOCK, B_BLOCK), (0, 0), (0, 0))).reshape(bsz, nb + 2, B_BLOCK, G, dh)
        return jnp.concatenate([tp[:, :-2], tp[:, 1:-1], tp[:, 2:]], axis=2)

    kb, vb = band(kl), band(vl)
    blk = jnp.arange(nb)[:, None]
    qpos = blk * B_BLOCK + jnp.arange(B_BLOCK)[None, :]
    kpos = (blk - 1) * B_BLOCK + jnp.arange(3 * B_BLOCK)[None, :]
    mask = (jnp.abs(qpos[:, :, None] - kpos[:, None, :]) <= B_WINDOW) & ((kpos >= 0) & (kpos < L))[:, None, :]
    s_win = jnp.einsum('bnqgrd,bnkgd->bngrqk', qb, kb).astype(F32) * scale
    s_win = jnp.where(mask[None, :, None, None], s_win, -jnp.inf)
    s_ctx = jnp.einsum('bnqgrd,bcgd->bngrqc', qb, kc).astype(F32) * scale
    s_sink = jnp.broadcast_to(sink_gr[None, None, :, :, None, None], s_win.shape[:-1] + (1,))
    p = jax.nn.softmax(jnp.concatenate([s_win, s_ctx, s_sink], axis=-1), axis=-1).astype(pl.dtype)
    nk = 3 * B_BLOCK
    ol = (jnp.einsum('bngrqk,bnkgd->bnqgrd', p[..., :nk], vb)
          + jnp.einsum('bngrqc,bcgd->bnqgrd', p[..., nk:nk + Lc], vc)).reshape(bsz, L, B_QW)

    oc = None
    if need_ctx:
        s_cc = jnp.einsum('bqgrd,bkgd->bgrqk', qc, kc).astype(F32) * scale
        sk = jnp.broadcast_to(sink_gr[None, :, :, None, None], s_cc.shape[:-1] + (1,))
        pcc = jax.nn.softmax(jnp.concatenate([s_cc, sk], axis=-1), axis=-1).astype(pc.dtype)
        oc = jnp.einsum('bgrqk,bkgd->bqgrd', pcc[..., :Lc], vc).reshape(bsz, Lc, B_QW)
    return oc, ol


def _cplx_combine(e1, e2):
    a1r, a1i, b1r, b1i = e1
    a2r, a2i, b2r, b2i = e2
    return (a1r * a2r - a1i * a2i, a1r * a2i + a1i * a2r,
            a2r * b1r - a2i * b1i + b2r, a2r * b1i + a2i * b1r + b2i)


def s5_discretize(lam_re, lam_im, log_dt, b_re, b_im):
    lam_re = jnp.minimum(lam_re.astype(F32), -1e-4)
    lam_im = lam_im.astype(F32)
    dt = jnp.exp(log_dt.astype(F32))[:, None]
    mag = jnp.exp(lam_re * dt)
    ab_re, ab_im = mag * jnp.cos(lam_im * dt), mag * jnp.sin(lam_im * dt)
    den = lam_re ** 2 + lam_im ** 2
    nr = ab_re - 1.0
    co_re = (nr * lam_re + ab_im * lam_im) / den
    co_im = (ab_im * lam_re - nr * lam_im) / den
    b_re, b_im = b_re.astype(F32), b_im.astype(F32)
    bb_re = co_re[..., None] * b_re - co_im[..., None] * b_im
    bb_im = co_re[..., None] * b_im + co_im[..., None] * b_re
    return ab_re, ab_im, bb_re, bb_im


def s5_scan(u, ab_re, ab_im, bb_re, bb_im, x0_re, x0_im):
    bu_re = jnp.einsum('lbgh,gph->lbgp', u, bb_re)
    bu_im = jnp.einsum('lbgh,gph->lbgp', u, bb_im)
    shape = (u.shape[0], 1) + ab_re.shape
    a_re, a_im = jnp.broadcast_to(ab_re, shape), jnp.broadcast_to(ab_im, shape)
    ac_re, ac_im, x_re, x_im = lax.associative_scan(_cplx_combine, (a_re, a_im, bu_re, bu_im), axis=0)
    x_re = x_re + ac_re * x0_re - ac_im * x0_im
    x_im = x_im + ac_re * x0_im + ac_im * x0_re
    return x_re, x_im


def s5_mixer(uc, ul, lam_re, lam_im, log_dt, b_re, b_im, c_re, c_im, d_skip, glu_w, need_ctx):
    def groups(u):
        b, l, _ = u.shape
        return u.astype(F32).reshape(b, l, C_GROUPS, C_GROUP).transpose(1, 0, 2, 3)

    g_c, g_l = groups(uc), groups(ul)
    zero = jnp.zeros((ul.shape[0], C_GROUPS, C_STATE), F32)
    fw = s5_discretize(lam_re[0], lam_im[0], log_dt[0], b_re, b_im)
    bw = s5_discretize(lam_re[1], lam_im[1], log_dt[1], b_re, b_im)
    xcf = s5_scan(g_c, *fw, zero, zero)
    xlf = s5_scan(g_l, *fw, xcf[0][-1], xcf[1][-1])
    xcb = s5_scan(g_c[::-1], *bw, zero, zero)
    xlb = s5_scan(g_l[::-1], *bw, xcb[0][-1], xcb[1][-1])
    cr, ci = c_re.astype(F32), c_im.astype(F32)
    dsk = d_skip.astype(F32).reshape(C_GROUPS, C_GROUP)

    def readout(u, xf, xb, dtype):
        x_re = xf[0] + xb[0][::-1]
        x_im = xf[1] + xb[1][::-1]
        y = jnp.einsum('lbgp,ghp->lbgh', x_re, cr) - jnp.einsum('lbgp,ghp->lbgh', x_im, ci) + dsk * u
        l, b = y.shape[:2]
        y = jax.nn.gelu(y.transpose(1, 0, 2, 3).reshape(b, l, C_WIDTH).astype(dtype))
        return y * jax.nn.sigmoid(y @ glu_w)

    out_l = readout(g_l, xlf, xlb, ul.dtype)
    out_c = readout(g_c, xcf, xcb, uc.dtype) if need_ctx else None
    return out_c, out_l


def centred_shift(x):
    prev = jnp.pad(x[:, :-1], ((0, 0), (1, 0), (0, 0)))
    nxt = jnp.pad(x[:, 1:], ((0, 0), (0, 1), (0, 0)))
    return 0.5 * (prev + nxt)


def _rwkv7_scan(r, w, k, v, a, b, s0):
    def step(s, inp):
        rt, wt, kt, vt, at, bt = inp
        sa = jnp.einsum('bhij,bhj->bhi', s, at)
        s = s * wt[:, :, None, :] + sa[..., None] * bt[:, :, None, :] + vt[..., None] * kt[:, :, None, :]
        return s, jnp.einsum('bhij,bhj->bhi', s, rt)

    xs = tuple(jnp.swapaxes(t, 0, 1) for t in (r, w, k, v, a, b))
    s_fin, o = lax.scan(step, s0, xs)
    return jnp.swapaxes(o, 0, 1), s_fin


def rwkv7_mixer(pc, pl, mu, w0, w2, a0, a2, g2, k_k, k_a, r_k, lnx_g, lnx_b, need_ctx):
    def heads(t):
        b, l, _ = t.shape
        return t.astype(F32).reshape(b, l, R_HEADS, R_HEAD)

    def prep(p):
        p = p + mu * (centred_shift(p) - p)
        r, k, v, wd_f, wd_b, ad_f, ad_b, gd = _split(p, R_SPLITS)
        g = jax.nn.sigmoid(gd) @ g2
        kk = heads(k * k_k)
        kk = kk / jnp.maximum(jnp.sqrt(jnp.sum(jnp.square(kk), -1, keepdims=True)), 1e-12)
        dirs = []
        for d, (wd, ad) in enumerate(((wd_f, ad_f), (wd_b, ad_b))):
            w = -jax.nn.softplus(-(w0[d] + jnp.tanh(wd) @ w2[d])) - 0.5
            decay = jnp.exp(-jnp.exp(heads(w)))
            a = jax.nn.sigmoid(a0[d] + ad @ a2[d])
            kd = heads(k * (1.0 + (a - 1.0) * k_a))
            dirs.append((decay, kd, -kk, kk * heads(a)))
        return heads(r), heads(k), heads(v), dirs, g

    rc, kc, vc, dc, gc = prep(pc)
    rl, kl, vl, dl, gl = prep(pl)
    zero = jnp.zeros((pl.shape[0], R_HEADS, R_HEAD, R_HEAD), F32)
    rev = lambda t: jnp.flip(t, axis=1)
    oc_f, sc_f = _rwkv7_scan(rc, dc[0][0], dc[0][1], vc, dc[0][2], dc[0][3], zero)
    ol_f, _ = _rwkv7_scan(rl, dl[0][0], dl[0][1], vl, dl[0][2], dl[0][3], sc_f)
    oc_b, sc_b = _rwkv7_scan(*[rev(t) for t in (rc, dc[1][0], dc[1][1], vc, dc[1][2], dc[1][3])], zero)
    ol_b, _ = _rwkv7_scan(*[rev(t) for t in (rl, dl[1][0], dl[1][1], vl, dl[1][2], dl[1][3])], sc_b)

    def finish(o, r, k, v, g):
        m = jnp.mean(o, -1, keepdims=True)
        var = jnp.mean(jnp.square(o - m), -1, keepdims=True)
        o = (o - m) * lax.rsqrt(var + RWKV_GN_EPS)
        b, l = o.shape[:2]
        o = o.reshape(b, l, R_WIDTH).astype(g.dtype) * lnx_g + lnx_b
        bonus = (jnp.sum(r * k * r_k, -1, keepdims=True) * v).reshape(b, l, R_WIDTH).astype(g.dtype)
        return (o + bonus) * g

    out_l = finish(ol_f + rev(ol_b), rl, kl, vl, gl)
    out_c = finish(oc_f + rev(oc_b), rc, kc, vc, gc) if need_ctx else None
    return out_c, out_l


def swiglu(h, wg, wu, wd):
    return (jax.nn.silu(h @ wg) * (h @ wu)) @ wd


def moe_swiglu(h, router_w, router_b, wg, wu, wd):
    logits = (h @ router_w + router_b).astype(F32)
    top_v, top_i = lax.top_k(logits, TOP_K)
    top_p = jax.nn.softmax(top_v, axis=-1)
    gates = jnp.sum(jax.nn.one_hot(top_i, N_EXPERTS, dtype=F32) * top_p[..., None], axis=-2).astype(h.dtype)
    out = jnp.zeros_like(h)
    for e in range(N_EXPERTS):
        out = out + gates[..., e:e + 1] * swiglu(h, wg[e], wu[e], wd[e])
    return out


def setup_inputs(seed: int = 0) -> dict:
    key = jax.random.key(seed)
    keys = iter(jax.random.split(key, 48))

    def nrm(shape, std):
        return std * jax.random.normal(next(keys), shape, F32)

    def uni(shape, lo, hi):
        return jax.random.uniform(next(keys), shape, F32, lo, hi)

    D = D_MODEL
    beta = DEEPNORM_BETA
    lam_im = jnp.pi * jnp.arange(C_STATE, dtype=F32) + nrm((N_ODD, 2, C_GROUPS, C_STATE), 0.01)
    return {
        "x": nrm((BATCH, SEQ, D), 1.0),
        "c": nrm((BATCH, D), 1.0),
        "ctx": nrm((BATCH, CTX_LEN, D), 1.0),
        "c_ctx": nrm((D,), 1.0),
        "ada_w": nrm((DEPTH, D, 6 * D), 0.5 * D ** -0.5),
        "ada_b": nrm((DEPTH, 6 * D), 0.02),
        "ln_g": 1.0 + nrm((DEPTH, 2, D), 0.02),
        "ln_b": nrm((DEPTH, 2, D), 0.02),
        "ev_w_in": nrm((N_EVEN, D, EVEN_IN), D ** -0.5),
        "ev_w_out": nrm((N_EVEN, EVEN_OUT, D), beta * EVEN_OUT ** -0.5),
        "hg_lb": 1.0 + nrm((N_EVEN, A_WIDTH), 0.1),
        "hg_norm_g": 1.0 + nrm((N_EVEN, A_WIDTH), 0.02),
        "attn_sink": nrm((N_EVEN, B_HEADS), 0.5),
        "ffn_w_gate": nrm((N_EVEN, D, D_FF), D ** -0.5),
        "ffn_w_up": nrm((N_EVEN, D, D_FF), D ** -0.5),
        "ffn_w_down": nrm((N_EVEN, D_FF, D), beta * D_FF ** -0.5),
        "od_w_in": nrm((N_ODD, D, ODD_IN), D ** -0.5),
        "od_w_out": nrm((N_ODD, ODD_OUT, D), beta * ODD_OUT ** -0.5),
        "s5_lam_re": -0.5 + nrm((N_ODD, 2, C_GROUPS, C_STATE), 0.01),
        "s5_lam_im": lam_im,
        "s5_log_dt": uni((N_ODD, 2, C_GROUPS), math.log(0.001), math.log(0.1)),
        "s5_b_re": nrm((N_ODD, C_GROUPS, C_STATE, C_GROUP), (2 * C_GROUP) ** -0.5),
        "s5_b_im": nrm((N_ODD, C_GROUPS, C_STATE, C_GROUP), (2 * C_GROUP) ** -0.5),
        "s5_c_re": nrm((N_ODD, C_GROUPS, C_GROUP, C_STATE), C_STATE ** -0.5),
        "s5_c_im": nrm((N_ODD, C_GROUPS, C_GROUP, C_STATE), C_STATE ** -0.5),
        "s5_d": nrm((N_ODD, C_WIDTH), 1.0),
        "s5_glu_w": nrm((N_ODD, C_WIDTH, C_WIDTH), C_WIDTH ** -0.5),
        "rwkv_mu": uni((N_ODD, R_IN), 0.0, 1.0),
        "rwkv_w0": uni((N_ODD, 2, R_WIDTH), -5.5, -0.5),
        "rwkv_w2": nrm((N_ODD, 2, R_LORA_W, R_WIDTH), 0.5 * R_LORA_W ** -0.5),
        "rwkv_a0": nrm((N_ODD, 2, R_WIDTH), 0.5),
        "rwkv_a2": nrm((N_ODD, 2, R_LORA_A, R_WIDTH), 0.5 * R_LORA_A ** -0.5),
        "rwkv_g2": nrm((N_ODD, R_LORA_G, R_WIDTH), R_LORA_G ** -0.5),
        "rwkv_k_k": 0.85 + nrm((N_ODD, R_WIDTH), 0.05),
        "rwkv_k_a": 1.0 + nrm((N_ODD, R_WIDTH), 0.05),
        "rwkv_r_k": nrm((N_ODD, R_HEADS, R_HEAD), 0.1),
        "rwkv_ln_g": 1.0 + nrm((N_ODD, R_WIDTH), 0.02),
        "rwkv_ln_b": nrm((N_ODD, R_WIDTH), 0.02),
        "moe_router_w": nrm((N_ODD, D, N_EXPERTS), D ** -0.5),
        "moe_router_b": nrm((N_ODD, N_EXPERTS), 0.01),
        "moe_w_gate": nrm((N_ODD, N_EXPERTS, D, E_FF), D ** -0.5),
        "moe_w_up": nrm((N_ODD, N_EXPERTS, D, E_FF), D ** -0.5),
        "moe_w_down": nrm((N_ODD, N_EXPERTS, E_FF, D), beta * E_FF ** -0.5),
    }


def reference(x, c, ctx, c_ctx, ada_w, ada_b, ln_g, ln_b,
              ev_w_in, ev_w_out, hg_lb, hg_norm_g, attn_sink, ffn_w_gate, ffn_w_up, ffn_w_down,
              od_w_in, od_w_out, s5_lam_re, s5_lam_im, s5_log_dt, s5_b_re, s5_b_im, s5_c_re, s5_c_im,
              s5_d, s5_glu_w, rwkv_mu, rwkv_w0, rwkv_w2, rwkv_a0, rwkv_a2, rwkv_g2, rwkv_k_k, rwkv_k_a,
              rwkv_r_k, rwkv_ln_g, rwkv_ln_b, moe_router_w, moe_router_b, moe_w_gate, moe_w_up, moe_w_down):
    L = x.shape[1]
    rows = L // GRID_W
    cos, sin = axial_rope_tables(rows)
    lb_soft = jax.nn.softmax(hg_lb.astype(F32), axis=0)
    lb_all = jnp.cumsum(lb_soft, axis=0) - lb_soft[0:1]
    cond_l = jax.nn.silu(c)[:, None, :]
    cond_c = jax.nn.silu(c_ctx)[None, None, :]
    xl, xc = x, ctx
    Lc = ctx.shape[1]
    for layer in range(DEPTH):
        j = layer // 2
        need_ctx = layer < DEPTH - 1
        sh1, sc1, gt1, sh2, sc2, gt2 = _split(cond_l @ ada_w[layer] + ada_b[layer], (D_MODEL,) * 6)
        csh1, csc1, cgt1, csh2, csc2, cgt2 = _split(cond_c @ ada_w[layer] + ada_b[layer], (D_MODEL,) * 6)
        hl = xl * (1.0 + sc1) + sh1
        hc = xc * (1.0 + csc1) + csh1
        if layer % 2 == 0:
            w_in, w_out = ev_w_in[j], ev_w_out[j]
            pc, pl = hc @ w_in, hl @ w_in
            oa_c, oa_l = hgrn2_mixer(pc[..., :A_IN], pl[..., :A_IN], lb_all[j], hg_norm_g[j], need_ctx)
            ob_c, ob_l = window_gqa_mixer(pc[..., A_IN:], pl[..., A_IN:], attn_sink[j], cos, sin, need_ctx)
            yl = jnp.concatenate([oa_l, ob_l], axis=-1)
            yc = jnp.concatenate([oa_c, ob_c], axis=-1) if need_ctx else None
        else:
            w_in, w_out = od_w_in[j], od_w_out[j]
            pc, pl = hc @ w_in, hl @ w_in
            oc_c, oc_l = s5_mixer(pc[..., :C_WIDTH], pl[..., :C_WIDTH], s5_lam_re[j], s5_lam_im[j],
                                  s5_log_dt[j], s5_b_re[j], s5_b_im[j], s5_c_re[j], s5_c_im[j],
                                  s5_d[j], s5_glu_w[j], need_ctx)
            od_c, od_l = rwkv7_mixer(pc[..., C_WIDTH:], pl[..., C_WIDTH:], rwkv_mu[j], rwkv_w0[j], rwkv_w2[j],
                                     rwkv_a0[j], rwkv_a2[j], rwkv_g2[j], rwkv_k_k[j], rwkv_k_a[j],
                                     rwkv_r_k[j], rwkv_ln_g[j], rwkv_ln_b[j], need_ctx)
            yl = jnp.concatenate([oc_l, od_l], axis=-1)
            yc = jnp.concatenate([oc_c, od_c], axis=-1) if need_ctx else None
        xl = layer_norm(DEEPNORM_ALPHA * xl + gt1 * (yl @ w_out), ln_g[layer, 0], ln_b[layer, 0])
        hl = xl * (1.0 + sc2) + sh2
        if need_ctx:
            xc = layer_norm(DEEPNORM_ALPHA * xc + cgt1 * (yc @ w_out), ln_g[layer, 0], ln_b[layer, 0])
            h = jnp.concatenate([xc * (1.0 + csc2) + csh2, hl], axis=1)
        else:
            h = hl
        if layer % 2 == 0:
            f = swiglu(h, ffn_w_gate[j], ffn_w_up[j], ffn_w_down[j])
        else:
            f = moe_swiglu(h, moe_router_w[j], moe_router_b[j], moe_w_gate[j], moe_w_up[j], moe_w_down[j])
        if need_ctx:
            xc = layer_norm(DEEPNORM_ALPHA * xc + cgt2 * f[:, :Lc], ln_g[layer, 1], ln_b[layer, 1])
            f = f[:, Lc:]
        xl = layer_norm(DEEPNORM_ALPHA * xl + gt2 * f, ln_g[layer, 1], ln_b[layer, 1])
    return xl
```

```python
import functools
import math

import numpy as np
import jax
import jax.numpy as jnp
from jax import lax
from jax.experimental import pallas as pl
from jax.experimental.pallas import tpu as pltpu

F32 = jnp.float32
BF16 = jnp.bfloat16
HIGHEST = lax.Precision.HIGHEST

D_MODEL = 1024
BATCH = 8
SEQ = 2048
CTX_LEN = 256
DEPTH = 4
GRID_W = 64
ROW_TILE = 256
SEQ_TILES = (CTX_LEN + SEQ) // ROW_TILE
CTX_ROWS = BATCH * CTX_LEN
M_ROWS = BATCH * (CTX_LEN + SEQ)

A_HEADS, A_DK, A_WIDTH = 4, 128, 512
B_HEADS, B_KV_HEADS, B_HEAD_DIM = 8, 2, 64
B_QW, B_KVW = 512, 128
B_BLOCK = 128
ROPE_BASE = 10000.0
C_GROUP, C_GROUPS, C_WIDTH, C_STATE = 16, 32, 512, 64
R_HEADS, R_HEAD, R_WIDTH = 8, 64, 512
R_LORA_W, R_LORA_A, R_LORA_G = 64, 64, 128
R_IN = 3 * R_WIDTH + 2 * R_LORA_W + 2 * R_LORA_A + R_LORA_G
D_FF = 2816
N_EXPERTS = 8
DEEPNORM_ALPHA = (2 * DEPTH) ** 0.25
LN_EPS = 1e-5
RWKV_GN_EPS = 64e-5

VMEM_LIMIT_BYTES = 56 * 1024 * 1024


def _params(*sem):
    return pltpu.CompilerParams(dimension_semantics=sem, vmem_limit_bytes=VMEM_LIMIT_BYTES)


def _dot(a, b, *, trans_a=False, trans_b=False, precision=None):
    dn = (((0 if trans_a else 1,), (1 if trans_b else 0,)), ((), ()))
    return lax.dot_general(a, b, dn, preferred_element_type=F32, precision=precision)


_MXU_DTYPE = BF16


def _bdot(a, b, **kw):
    return _dot(a.astype(_MXU_DTYPE), b.astype(_MXU_DTYPE), **kw)


def _sigmoid(x):
    return 1.0 / (1.0 + jnp.exp(-x))


def _silu(x):
    return x * _sigmoid(x)


def _seq_block(b, i):
    return jnp.where(i == 0, b, BATCH + b * (SEQ // ROW_TILE) + i - 1)


def _tile_order(i, reverse):
    if not reverse:
        return i
    return jnp.where(i == 0, 0, SEQ_TILES - i)


def _ada_kernel(cond_ref, w_ref, b_ref, o_ref):
    o_ref[0] = _dot(_silu(cond_ref[...]), w_ref[0], precision=HIGHEST) + b_ref[0]


def _ada_all(cond, ada_w, ada_b):
    rows = cond.shape[0]
    tn = 1536
    return pl.pallas_call(
        _ada_kernel,
        grid=(DEPTH, 6 * D_MODEL // tn),
        in_specs=[pl.BlockSpec((rows, D_MODEL), lambda l, j: (0, 0)),
                  pl.BlockSpec((1, D_MODEL, tn), lambda l, j: (l, 0, j)),
                  pl.BlockSpec((1, 1, tn), lambda l, j: (l, 0, j))],
        out_specs=pl.BlockSpec((1, rows, tn), lambda l, j: (l, 0, j)),
        out_shape=jax.ShapeDtypeStruct((DEPTH, rows, 6 * D_MODEL), F32),
        compiler_params=_params("parallel", "parallel"),
        name="ada",
    )(cond, ada_w, ada_b.reshape(DEPTH, 1, 6 * D_MODEL))


def _mod_matmul_kernel(splits, x_ref, sc_ref, sh_ref, w_ref, *o_refs):
    h = (x_ref[...] * sc_ref[0] + sh_ref[0]).astype(BF16)
    off = 0
    for o_ref, width in zip(o_refs, splits):
        o_ref[...] = _dot(h, w_ref[:, off:off + width])
        off += width


def _mod_matmul(x, scale_t, shift_t, w_bf16, splits, tm):
    m, d = x.shape
    n = w_bf16.shape[1]
    assert sum(splits) == n and m % tm == 0
    return pl.pallas_call(
        functools.partial(_mod_matmul_kernel, splits),
        grid=(m // tm,),
        in_specs=[pl.BlockSpec((tm, d), lambda i: (i, 0)),
                  pl.BlockSpec((1, 1, d), lambda i: (i, 0, 0)),
                  pl.BlockSpec((1, 1, d), lambda i: (i, 0, 0)),
                  pl.BlockSpec((d, n), lambda i: (0, 0))],
        out_specs=[pl.BlockSpec((tm, w), lambda i: (i, 0)) for w in splits],
        out_shape=[jax.ShapeDtypeStruct((m, w), F32) for w in splits],
        compiler_params=_params("parallel"),
        name="mod_matmul",
    )(x, scale_t, shift_t, w_bf16)


def _layer_norm_rows(z, g, b):
    mu = jnp.mean(z, axis=-1, keepdims=True)
    zc = z - mu
    var = jnp.mean(zc * zc, axis=-1, keepdims=True)
    return zc * lax.rsqrt(var + LN_EPS) * g + b


def _out_proj_kernel(y1_ref, y2_ref, w_ref, x_ref, gt_ref, g_ref, b_ref, o_ref):
    k1 = y1_ref.shape[1]
    proj = _bdot(y1_ref[...], w_ref[:k1, :]) + _bdot(y2_ref[...], w_ref[k1:, :])
    z = DEEPNORM_ALPHA * x_ref[...] + gt_ref[0] * proj
    o_ref[...] = _layer_norm_rows(z, g_ref[...], b_ref[...])


def _out_proj_ln(y1, y2, w_bf16, x, gate_t, ln_g, ln_b, tm):
    m, d = x.shape
    k1, k2 = y1.shape[1], y2.shape[1]
    return pl.pallas_call(
        _out_proj_kernel,
        grid=(m // tm,),
        in_specs=[pl.BlockSpec((tm, k1), lambda i: (i, 0)),
                  pl.BlockSpec((tm, k2), lambda i: (i, 0)),
                  pl.BlockSpec((k1 + k2, d), lambda i: (0, 0)),
                  pl.BlockSpec((tm, d), lambda i: (i, 0)),
                  pl.BlockSpec((1, 1, d), lambda i: (i, 0, 0)),
                  pl.BlockSpec((1, d), lambda i: (0, 0)),
                  pl.BlockSpec((1, d), lambda i: (0, 0))],
        out_specs=pl.BlockSpec((tm, d), lambda i: (i, 0)),
        out_shape=jax.ShapeDtypeStruct((m, d), F32),
        compiler_params=_params("parallel"),
        name="out_proj_ln",
    )(y1, y2, w_bf16, x, gate_t, ln_g.reshape(1, d), ln_b.reshape(1, d))


def _ffn_kernel(n_exp, x_ref, sc_ref, sh_ref, gt_ref, route_ref, wg_ref, wu_ref, wd_ref,
                g_ref, b_ref, o_ref, h_ref, acc_ref):
    e, f = pl.program_id(1), pl.program_id(2)

    @pl.when((e == 0) & (f == 0))
    def _():
        h_ref[...] = (x_ref[...] * sc_ref[0] + sh_ref[0]).astype(BF16)
        acc_ref[...] = jnp.zeros_like(acc_ref)

    h = h_ref[...]
    a = _silu(_dot(h, wg_ref[0])) * _dot(h, wu_ref[0])
    if n_exp > 1:
        lane = lax.broadcasted_iota(jnp.int32, route_ref.shape, 1)
        a = a * jnp.sum(jnp.where(lane == e, route_ref[...], 0.0), axis=1, keepdims=True)
    acc_ref[...] += _bdot(a, wd_ref[0])

    @pl.when((e == n_exp - 1) & (f == pl.num_programs(2) - 1))
    def _():
        z = DEEPNORM_ALPHA * x_ref[...] + gt_ref[0] * acc_ref[...]
        o_ref[...] = _layer_norm_rows(z, g_ref[...], b_ref[...])


def _ffn_ln(x, scale_t, shift_t, gate_t, route, wg, wu, wd, ln_g, ln_b, tm, tf):
    m, d = x.shape
    n_exp, _, ff = wg.shape
    assert ff % tf == 0 and m % tm == 0
    return pl.pallas_call(
        functools.partial(_ffn_kernel, n_exp),
        grid=(m // tm, n_exp, ff // tf),
        in_specs=[pl.BlockSpec((tm, d), lambda i, e, f: (i, 0)),
                  pl.BlockSpec((1, 1, d), lambda i, e, f: (i, 0, 0)),
                  pl.BlockSpec((1, 1, d), lambda i, e, f: (i, 0, 0)),
                  pl.BlockSpec((1, 1, d), lambda i, e, f: (i, 0, 0)),
                  pl.BlockSpec((tm, 128), lambda i, e, f: (i, 0)),
                  pl.BlockSpec((1, d, tf), lambda i, e, f: (e, 0, f)),
                  pl.BlockSpec((1, d, tf), lambda i, e, f: (e, 0, f)),
                  pl.BlockSpec((1, tf, d), lambda i, e, f: (e, f, 0)),
                  pl.BlockSpec((1, d), lambda i, e, f: (0, 0)),
                  pl.BlockSpec((1, d), lambda i, e, f: (0, 0))],
        out_specs=pl.BlockSpec((tm, d), lambda i, e, f: (i, 0)),
        out_shape=jax.ShapeDtypeStruct((m, d), F32),
        scratch_shapes=[pltpu.VMEM((tm, d), BF16), pltpu.VMEM((tm, d), F32)],
        compiler_params=_params("parallel", "arbitrary", "arbitrary"),
        name="ffn_ln",
    )(x, scale_t, shift_t, gate_t, route, wg, wu, wd, ln_g.reshape(1, d), ln_b.reshape(1, d))


def _router_kernel(x_ref, sc_ref, sh_ref, w_ref, b_ref, o_ref):
    h = x_ref[...] * sc_ref[0] + sh_ref[0]
    logits = _dot(h, w_ref[...], precision=HIGHEST) + b_ref[...]
    lane = lax.broadcasted_iota(jnp.int32, logits.shape, 1)
    neg = jnp.float32(-jnp.inf)
    logits = jnp.where(lane < N_EXPERTS, logits, neg)
    v1 = jnp.max(logits, axis=1, keepdims=True)
    i1 = jnp.min(jnp.where(logits == v1, lane, 128), axis=1, keepdims=True)
    rest = jnp.where(lane == i1, neg, logits)
    v2 = jnp.max(rest, axis=1, keepdims=True)
    i2 = jnp.min(jnp.where(rest == v2, lane, 128), axis=1, keepdims=True)
    e2 = jnp.exp(v2 - v1)
    p1 = 1.0 / (1.0 + e2)
    p2 = e2 / (1.0 + e2)
    o_ref[...] = jnp.where(lane == i1, p1, 0.0) + jnp.where(lane == i2, p2, 0.0)


def _router(x, scale_t, shift_t, w_pad, b_pad, tm):
    m, d = x.shape
    return pl.pallas_call(
        _router_kernel,
        grid=(m // tm,),
        in_specs=[pl.BlockSpec((tm, d), lambda i: (i, 0)),
                  pl.BlockSpec((1, 1, d), lambda i: (i, 0, 0)),
                  pl.BlockSpec((1, 1, d), lambda i: (i, 0, 0)),
                  pl.BlockSpec((d, 128), lambda i: (0, 0)),
                  pl.BlockSpec((1, 128), lambda i: (0, 0))],
        out_specs=pl.BlockSpec((tm, 128), lambda i: (i, 0)),
        out_shape=jax.ShapeDtypeStruct((m, 128), F32),
        compiler_params=_params("parallel"),
        name="router",
    )(x, scale_t, shift_t, w_pad, b_pad)


HGRN_CHUNK = 16


def _hgrn_kernel(reverse, finish, *refs):
    if finish:
        q_ref, f_ref, i_ref, lb_ref, of_ref, g_ref, ng_ref, o_ref, st_ref = refs
    else:
        q_ref, f_ref, i_ref, lb_ref, o_ref, st_ref = refs
    C = HGRN_CHUNK
    n_chunks = ROW_TILE // C

    @pl.when(pl.program_id(1) == 0)
    def _():
        st_ref[...] = jnp.zeros_like(st_ref)

    row = lax.broadcasted_iota(jnp.int32, (C, C), 0)
    col = lax.broadcasted_iota(jnp.int32, (C, C), 1)
    keep = (row <= col) if reverse else (row >= col)
    tri = keep.astype(F32)

    def chunk(ci, carry):
        c = (n_chunks - 1 - ci) if reverse else ci
        rows = pl.ds(pl.multiple_of(c * C, C), C)
        for h in range(A_HEADS):
            sl = slice(h * A_DK, (h + 1) * A_DK)
            lb = lb_ref[:, sl]
            q = _silu(q_ref[rows, sl]) * A_DK ** -0.5
            f = lb + (1.0 - lb) * _sigmoid(f_ref[rows, sl])
            k = 1.0 - f
            v = i_ref[rows, sl]
            b = _dot(tri, jnp.log(f), precision=HIGHEST)
            b_tot = b[0:1] if reverse else b[C - 1:C]
            att = jnp.zeros((C, C), F32)
            for s in range(C):
                e = jnp.exp(jnp.minimum(b - b[s:s + 1], 0.0))
                att = jnp.where(col == s, jnp.sum(q * e * k[s:s + 1], axis=1, keepdims=True), att)
            att = jnp.where(keep, att, 0.0)
            st = st_ref[h]
            o = _bdot(q * jnp.exp(b), st, trans_b=True) + _bdot(att, v)
            st_ref[h] = st * jnp.exp(b_tot) + _bdot(v, k * jnp.exp(b_tot - b), trans_a=True)
            if finish:
                o = o + of_ref[rows, sl]
                o = o * lax.rsqrt(jnp.mean(o * o, axis=1, keepdims=True) + 1e-6)
                o = o * ng_ref[:, sl] * _silu(g_ref[rows, sl])
            o_ref[rows, sl] = o
        return carry

    lax.fori_loop(0, n_chunks, chunk, 0)


def _hgrn_pass(reverse, q, f, i, lb, finish_args=None):
    m, w = q.shape
    seq_spec = pl.BlockSpec((ROW_TILE, w), lambda b, t: (_seq_block(b, _tile_order(t, reverse)), 0))
    vec_spec = pl.BlockSpec((1, w), lambda b, t: (0, 0))
    ins = [q, f, i, lb]
    specs = [seq_spec, seq_spec, seq_spec, vec_spec]
    if finish_args is not None:
        of, g, ng = finish_args
        ins += [of, g, ng]
        specs += [seq_spec, seq_spec, vec_spec]
    return pl.pallas_call(
        functools.partial(_hgrn_kernel, reverse, finish_args is not None),
        grid=(BATCH, SEQ_TILES),
        in_specs=specs,
        out_specs=seq_spec,
        out_shape=jax.ShapeDtypeStruct((m, w), F32),
        scratch_shapes=[pltpu.VMEM((A_HEADS, A_DK, A_DK), F32)],
        compiler_params=_params("parallel", "arbitrary"),
        name="hgrn_bwd" if reverse else "hgrn_fwd",
    )(*ins)


def _rope_kernel(q_ref, k_ref, cos_ref, sa_ref, sb_ref, qo_ref, ko_ref):
    cos, sa, sb = cos_ref[...], sa_ref[...], sb_ref[...]

    def rot(x):
        return x * cos + pltpu.roll(x, 96, 1) * sa + pltpu.roll(x, 32, 1) * sb

    for j in range(B_QW // 128):
        qo_ref[:, j * 128:(j + 1) * 128] = rot(q_ref[:, j * 128:(j + 1) * 128])
    ko_ref[...] = rot(k_ref[...])


def _rope(q, k, cos_t, sa_t, sb_t):
    m = q.shape[0]
    seq = lambda w: pl.BlockSpec((ROW_TILE, w), lambda b, t: (_seq_block(b, t), 0))
    tab = pl.BlockSpec((ROW_TILE, 128), lambda b, t: (t, 0))
    return pl.pallas_call(
        _rope_kernel,
        grid=(BATCH, SEQ_TILES),
        in_specs=[seq(B_QW), seq(B_KVW), tab, tab, tab],
        out_specs=[seq(B_QW), seq(B_KVW)],
        out_shape=[jax.ShapeDtypeStruct((m, B_QW), F32), jax.ShapeDtypeStruct((m, B_KVW), F32)],
        compiler_params=_params("parallel", "parallel"),
        name="rope",
    )(q, k, cos_t, sa_t, sb_t)


def _rope_tables():
    rows = SEQ // GRID_W
    row = jnp.repeat(jnp.arange(rows, dtype=F32), GRID_W)
    colp = jnp.tile(jnp.arange(GRID_W, dtype=F32), rows)
    n_freq = B_HEAD_DIM // 4
    inv = ROPE_BASE ** (-jnp.arange(n_freq, dtype=F32) / n_freq)
    ang = jnp.concatenate([row[:, None] * inv, colp[:, None] * inv], axis=-1)
    cos, sin = jnp.cos(ang), jnp.sin(ang)
    zero = jnp.zeros_like(sin)
    cos_l = jnp.tile(cos, (1, 4))
    sa_l = jnp.tile(jnp.concatenate([-sin, zero], axis=-1), (1, 2))
    sb_l = jnp.tile(jnp.concatenate([zero, sin], axis=-1), (1, 2))
    pad = lambda t, v: jnp.concatenate([jnp.full((CTX_LEN, 128), v, F32), t], axis=0)
    return pad(cos_l, 1.0), pad(sa_l, 0.0), pad(sb_l, 0.0)


def _attend(q_ref, o_ref, sink_ref, segments):
    tq = q_ref.shape[0]
    lane = lax.broadcasted_iota(jnp.int32, (tq, 128), 1)
    left = lane < B_HEAD_DIM
    neg = jnp.float32(-jnp.inf)
    scale = B_HEAD_DIM ** -0.5
    dup = []
    for g in range(B_KV_HEADS):
        segs = []
        for k, v, mask in segments:
            lk = lax.broadcasted_iota(jnp.int32, k.shape, 1) < B_HEAD_DIM
            kr, vr = pltpu.roll(k, 64, 1), pltpu.roll(v, 64, 1)
            kd = jnp.where(lk, k, kr) if g == 0 else jnp.where(lk, kr, k)
            vd = jnp.where(lk, v, vr) if g == 0 else jnp.where(lk, vr, v)
            segs.append((kd.astype(_MXU_DTYPE), vd, lk, mask))
        dup.append(segs)
    pairs = B_HEADS // 2
    for p in range(pairs):
        segs = dup[p // (pairs // B_KV_HEADS)]
        q_pair = q_ref[:, p * 128:(p + 1) * 128] * scale
        out = jnp.zeros((tq, 128), F32)
        for half in range(2):
            sel = left if half == 0 else jnp.logical_not(left)
            sink = sink_ref[2 * p + half:2 * p + half + 1, 0:1]
            qm = jnp.where(sel, q_pair, 0.0).astype(_MXU_DTYPE)
            scores = []
            for kd, vd, lk, mask in segs:
                s = _dot(qm, kd, trans_b=True)
                scores.append(s if mask is None else jnp.where(mask, s, neg))
            mx = sink
            for s in scores:
                mx = jnp.maximum(mx, jnp.max(s, axis=1, keepdims=True))
            denom = jnp.exp(sink - mx)
            acc = jnp.zeros((tq, 128), F32)
            for s, (kd, vd, lk, mask) in zip(scores, segs):
                e = jnp.exp(s - mx)
                denom = denom + jnp.sum(e, axis=1, keepdims=True)
                vsel = jnp.where(lk if half == 0 else jnp.logical_not(lk), vd, 0.0)
                acc = acc + _bdot(e, vsel)
            out = out + acc / denom
        o_ref[:, p * 128:(p + 1) * 128] = out


def _attn_latent_kernel(q_ref, kp_ref, kc_ref, kn_ref, vp_ref, vc_ref, vn_ref, kx_ref, vx_ref, sink_ref, o_ref):
    n = pl.program_id(1)
    nb = pl.num_programs(1)
    row = lax.broadcasted_iota(jnp.int32, (B_BLOCK, B_BLOCK), 0)
    col = lax.broadcasted_iota(jnp.int32, (B_BLOCK, B_BLOCK), 1)
    segments = [
        (kp_ref[...], vp_ref[...], (col >= row) & (n > 0)),
        (kc_ref[...], vc_ref[...], None),
        (kn_ref[...], vn_ref[...], (col <= row) & (n < nb - 1)),
        (kx_ref[...], vx_ref[...], None),
    ]
    _attend(q_ref, o_ref, sink_ref, segments)


def _attn_ctx_kernel(q_ref, kx_ref, vx_ref, sink_ref, latent_out_ref, o_ref):
    del latent_out_ref
    _attend(q_ref, o_ref, sink_ref, [(kx_ref[...], vx_ref[...], None)])


def _attention(q, k, v, sink_rows):
    m = q.shape[0]
    nb = SEQ // B_BLOCK
    base = CTX_ROWS // B_BLOCK
    qspec = pl.BlockSpec((B_BLOCK, B_QW), lambda b, n: (base + b * nb + n, 0))

    def kv(shift):
        return pl.BlockSpec((B_BLOCK, B_KVW), lambda b, n: (base + b * nb + jnp.clip(n + shift, 0, nb - 1), 0))

    ctx_kv = pl.BlockSpec((CTX_LEN, B_KVW), lambda b, n: (b, 0))
    sink_spec = pl.BlockSpec((B_HEADS, 128), lambda b, n: (0, 0))
    out_l = pl.pallas_call(
        _attn_latent_kernel,
        grid=(BATCH, nb),
        in_specs=[qspec, kv(-1), kv(0), kv(1), kv(-1), kv(0), kv(1), ctx_kv, ctx_kv, sink_spec],
        out_specs=qspec,
        out_shape=jax.ShapeDtypeStruct((m, B_QW), F32),
        compiler_params=_params("parallel", "parallel"),
        name="attn_latent",
    )(q, k, k, k, v, v, v, k, v, sink_rows)
    out = pl.pallas_call(
        _attn_ctx_kernel,
        grid=(BATCH,),
        in_specs=[pl.BlockSpec((CTX_LEN, B_QW), lambda b: (b, 0)),
                  pl.BlockSpec((CTX_LEN, B_KVW), lambda b: (b, 0)),
                  pl.BlockSpec((CTX_LEN, B_KVW), lambda b: (b, 0)),
                  pl.BlockSpec((B_HEADS, 128), lambda b: (0, 0)),
                  pl.BlockSpec(memory_space=pl.ANY)],
        out_specs=pl.BlockSpec((CTX_LEN, B_QW), lambda b: (b, 0)),
        out_shape=jax.ShapeDtypeStruct((m, B_QW), F32),
        input_output_aliases={4: 0},
        compiler_params=_params("parallel"),
        name="attn_ctx",
    )(q, k, v, sink_rows, out_l)
    return out


def _window_attention(q, k, v, sink):
    cos_t, sa_t, sb_t = _rope_tables()
    qr, kr = _rope(q, k, cos_t, sa_t, sb_t)
    sink_rows = jnp.broadcast_to(sink.astype(F32)[:, None], (B_HEADS, 128))
    return _attention(qr, kr, v, sink_rows)


def _hgrn2(q, f_fw, f_bw, i, g, lb, norm_g):
    lb = lb.reshape(1, A_WIDTH)
    o_fw = _hgrn_pass(False, q, f_fw, i, lb)
    return _hgrn_pass(True, q, f_bw, i, lb, (o_fw, g, norm_g.reshape(1, A_WIDTH)))


S5_STEPS = 64
S5_ROWS = S5_STEPS * BATCH
S5_TILES = (CTX_LEN + SEQ) // S5_STEPS
S5_CTX_TILES = CTX_LEN // S5_STEPS
S5_BLOCKS = 4
S5_BLOCK_IN = C_WIDTH // S5_BLOCKS
S5_BLOCK_STATE = C_GROUPS * C_STATE // S5_BLOCKS


def _s5_kernel(reverse, finish, *refs):
    if finish:
        u_ref, a_ref, wb_ref, wc_ref, yf_ref, glu_ref, y_ref, x_ref, st_ref = refs
    else:
        u_ref, a_ref, wb_ref, wc_ref, d_ref, y_ref, x_ref, st_ref = refs
    ns = S5_BLOCK_STATE

    @pl.when(pl.program_id(0) == 0)
    def _():
        st_ref[...] = jnp.zeros_like(st_ref)

    for k in range(S5_BLOCKS):
        x_ref[:, 2 * ns * k:2 * ns * (k + 1)] = _bdot(u_ref[:, S5_BLOCK_IN * k:S5_BLOCK_IN * (k + 1)], wb_ref[k])

    def step(tt, carry):
        t = (S5_STEPS - 1 - tt) if reverse else tt
        rows = pl.ds(pl.multiple_of(t * BATCH, BATCH), BATCH)
        for k in range(S5_BLOCKS):
            re = slice(2 * ns * k, 2 * ns * k + ns)
            im = slice(2 * ns * k + ns, 2 * ns * (k + 1))
            ar, ai = a_ref[:, re], a_ref[:, im]
            sr, si = st_ref[:, re], st_ref[:, im]
            nr = ar * sr - ai * si + x_ref[rows, re]
            ni = ar * si + ai * sr + x_ref[rows, im]
            st_ref[:, re] = nr
            st_ref[:, im] = ni
            x_ref[rows, re] = nr
            x_ref[rows, im] = ni
        return carry

    lax.fori_loop(0, S5_STEPS, step, 0)

    for k in range(S5_BLOCKS):
        cols = slice(S5_BLOCK_IN * k, S5_BLOCK_IN * (k + 1))
        y = _bdot(x_ref[:, 2 * ns * k:2 * ns * (k + 1)], wc_ref[k])
        if finish:
            y_ref[:, cols] = y + yf_ref[:, cols]
        else:
            y_ref[:, cols] = y + d_ref[:, cols] * u_ref[:, cols]
    if finish:
        y = jax.nn.gelu(y_ref[...])
        y_ref[...] = y * _sigmoid(_bdot(y, glu_ref[...]))


def _s5_tile_order(i, reverse):
    if not reverse:
        return i
    return jnp.where(i < S5_CTX_TILES, S5_CTX_TILES - 1 - i, S5_TILES + S5_CTX_TILES - 1 - i)


def _s5_pass(reverse, u_tm, acoef, wb, wc, extra):
    m, w = u_tm.shape
    nstate = 2 * C_GROUPS * C_STATE
    row_spec = pl.BlockSpec((S5_ROWS, w), lambda i: (_s5_tile_order(i, reverse), 0))
    full = lambda a: pl.BlockSpec(a.shape, lambda i: (0,) * a.ndim)
    finish = reverse
    if finish:
        yf, glu_w = extra
        ins, specs = [u_tm, acoef, wb, wc, yf, glu_w], [row_spec, full(acoef), full(wb), full(wc), row_spec, full(glu_w)]
    else:
        (dskip,) = extra
        ins, specs = [u_tm, acoef, wb, wc, dskip], [row_spec, full(acoef), full(wb), full(wc), full(dskip)]
    return pl.pallas_call(
        functools.partial(_s5_kernel, reverse, finish),
        grid=(S5_TILES,),
        in_specs=specs,
        out_specs=row_spec,
        out_shape=jax.ShapeDtypeStruct((m, w), F32),
        scratch_shapes=[pltpu.VMEM((S5_ROWS, nstate), F32), pltpu.VMEM((BATCH, nstate), F32)],
        compiler_params=_params("arbitrary"),
        name="s5_bwd" if reverse else "s5_fwd",
    )(*ins)


def _s5_discretize(lam_re, lam_im, log_dt, b_re, b_im):
    lam_re = jnp.minimum(lam_re.astype(F32), -1e-4)
    lam_im = lam_im.astype(F32)
    dt = jnp.exp(log_dt.astype(F32))[:, None]
    mag = jnp.exp(lam_re * dt)
    ab_re, ab_im = mag * jnp.cos(lam_im * dt), mag * jnp.sin(lam_im * dt)
    den = lam_re ** 2 + lam_im ** 2
    nr = ab_re - 1.0
    co_re = (nr * lam_re + ab_im * lam_im) / den
    co_im = (ab_im * lam_re - nr * lam_im) / den
    bb_re = co_re[..., None] * b_re - co_im[..., None] * b_im
    bb_im = co_re[..., None] * b_im + co_im[..., None] * b_re
    return ab_re, ab_im, bb_re, bb_im


def _s5_tables(lam_re, lam_im, log_dt, b_re, b_im, c_re, c_im):
    eye = jnp.eye(C_GROUPS // S5_BLOCKS, dtype=F32)
    gb = C_GROUPS // S5_BLOCKS

    def in_map(bb):
        return jnp.einsum('kgph,gG->kghGp', bb.reshape(S5_BLOCKS, gb, C_STATE, C_GROUP), eye).reshape(
            S5_BLOCKS, S5_BLOCK_IN, S5_BLOCK_STATE)

    def out_map(cc):
        return jnp.einsum('kghp,gG->kgpGh', cc.reshape(S5_BLOCKS, gb, C_GROUP, C_STATE), eye).reshape(
            S5_BLOCKS, S5_BLOCK_STATE, S5_BLOCK_IN)

    wc = jnp.concatenate([out_map(c_re.astype(F32)), -out_map(c_im.astype(F32))], axis=1).astype(_MXU_DTYPE)
    tables = []
    for d in range(2):
        ab_re, ab_im, bb_re, bb_im = _s5_discretize(lam_re[d], lam_im[d], log_dt[d], b_re.astype(F32), b_im.astype(F32))
        a = jnp.concatenate([ab_re.reshape(S5_BLOCKS, S5_BLOCK_STATE), ab_im.reshape(S5_BLOCKS, S5_BLOCK_STATE)], axis=1)
        acoef = jnp.broadcast_to(a.reshape(1, -1), (BATCH, 2 * C_GROUPS * C_STATE))
        wb = jnp.concatenate([in_map(bb_re), in_map(bb_im)], axis=2).astype(_MXU_DTYPE)
        tables.append((acoef, wb))
    return tables, wc


def _to_time_major(y):
    w = y.shape[1]
    c = y[:CTX_ROWS].reshape(BATCH, CTX_LEN, w).transpose(1, 0, 2).reshape(CTX_ROWS, w)
    l = y[CTX_ROWS:].reshape(BATCH, SEQ, w).transpose(1, 0, 2).reshape(BATCH * SEQ, w)
    return jnp.concatenate([c, l], axis=0)


def _from_time_major(y):
    w = y.shape[1]
    c = y[:CTX_ROWS].reshape(CTX_LEN, BATCH, w).transpose(1, 0, 2).reshape(CTX_ROWS, w)
    l = y[CTX_ROWS:].reshape(SEQ, BATCH, w).transpose(1, 0, 2).reshape(BATCH * SEQ, w)
    return jnp.concatenate([c, l], axis=0)


def _s5(u, lam_re, lam_im, log_dt, b_re, b_im, c_re, c_im, d_skip, glu_w):
    (fw, bw), wc = _s5_tables(lam_re, lam_im, log_dt, b_re, b_im, c_re, c_im)
    u_tm = _to_time_major(u)
    y_fw = _s5_pass(False, u_tm, fw[0], fw[1], wc, (d_skip.astype(F32).reshape(1, C_WIDTH),))
    y = _s5_pass(True, u_tm, bw[0], bw[1], wc, (y_fw, glu_w.astype(_MXU_DTYPE)))
    return _from_time_major(y)


RW_LORA_OFF = 3 * R_WIDTH


def _softplus(z):
    return jnp.maximum(z, 0.0) + jnp.log1p(jnp.exp(-jnp.abs(z)))


def _rwkv_prep_kernel(p_ref, hp_ref, hn_ref, mu_ref, w0_ref, w2_ref, a0_ref, a2_ref, g2_ref, kk_ref, ka_ref, rk_ref,
                      ones_ref, r_o, v_o, g_o, bonus_o, kkn_o, lwf_o, kdf_o, bf_o, lwb_o, kdb_o, bb_o):
    i = pl.program_id(1)
    x = p_ref[...]
    rows = x.shape[0]
    rowi = lax.broadcasted_iota(jnp.int32, (rows, 1), 0)
    prev_row = jnp.where(i >= 2, hp_ref[7:8, :], 0.0)
    next_row = jnp.where((i >= 1) & (i < SEQ_TILES - 1), hn_ref[0:1, :], 0.0)
    prev = jnp.where(rowi == 0, prev_row, pltpu.roll(x, 1, 0))
    nxt = jnp.where(rowi == rows - 1, next_row, pltpu.roll(x, rows - 1, 0))
    x = x + mu_ref[...] * (0.5 * (prev + nxt) - x)

    r = x[:, 0:R_WIDTH]
    k = x[:, R_WIDTH:2 * R_WIDTH]
    v = x[:, 2 * R_WIDTH:3 * R_WIDTH]
    wd = x[:, RW_LORA_OFF:RW_LORA_OFF + 128]
    ad = x[:, RW_LORA_OFF + 128:RW_LORA_OFF + 256]
    gd = x[:, RW_LORA_OFF + 256:RW_LORA_OFF + 384]
    ones = ones_ref[...]

    r_o[...] = r
    v_o[...] = v
    g_o[...] = _bdot(_sigmoid(gd), g2_ref[...])
    bonus_o[...] = _dot(r * k * rk_ref[...], ones, precision=HIGHEST) * v
    kk = k * kk_ref[...]
    kkn = kk / jnp.maximum(jnp.sqrt(_dot(kk * kk, ones, precision=HIGHEST)), 1e-12)
    kkn_o[...] = kkn
    tw = jnp.tanh(wd)
    for d, (lw_o, kd_o, b_o) in enumerate(((lwf_o, kdf_o, bf_o), (lwb_o, kdb_o, bb_o))):
        w = -_softplus(-(w0_ref[d:d + 1, :] + _bdot(tw, w2_ref[d]))) - 0.5
        lw_o[...] = -jnp.exp(w)
        a = _sigmoid(a0_ref[d:d + 1, :] + _bdot(ad, a2_ref[d]))
        kd_o[...] = k * (1.0 + (a - 1.0) * ka_ref[...])
        b_o[...] = kkn * a


def _rwkv_prep(p, mu, w0, w2pad, a0, a2pad, g2, k_k, k_a, r_k, ones_blk):
    m, w = p.shape
    hb = ROW_TILE // 8
    seq = lambda width: pl.BlockSpec((ROW_TILE, width), lambda b, i: (_seq_block(b, i), 0))
    halo_prev = pl.BlockSpec((8, w), lambda b, i: (jnp.maximum(_seq_block(b, i) * hb - 1, 0), 0))
    halo_next = pl.BlockSpec((8, w), lambda b, i: (jnp.minimum((_seq_block(b, i) + 1) * hb, m // 8 - 1), 0))
    full = lambda a: pl.BlockSpec(a.shape, lambda b, i: (0,) * a.ndim)
    consts = [mu, w0, w2pad, a0, a2pad, g2, k_k, k_a, r_k, ones_blk]
    return pl.pallas_call(
        _rwkv_prep_kernel,
        grid=(BATCH, SEQ_TILES),
        in_specs=[seq(w), halo_prev, halo_next] + [full(c) for c in consts],
        out_specs=[seq(R_WIDTH)] * 11,
        out_shape=[jax.ShapeDtypeStruct((m, R_WIDTH), F32)] * 11,
        compiler_params=_params("parallel", "parallel"),
        name="rwkv_prep",
    )(p, p, p, *consts)


RW_CHUNK = 64


def _rwkv_scan_kernel(reverse, finish, *refs):
    if finish:
        (r_ref, kd_ref, v_ref, lw_ref, kkn_ref, b_ref, of_ref, g_ref, bonus_ref, lng_ref, lnb_ref,
         o_ref, st_ref) = refs
    else:
        r_ref, kd_ref, v_ref, lw_ref, kkn_ref, b_ref, o_ref, st_ref = refs
    C = RW_CHUNK
    n_chunks = ROW_TILE // C

    @pl.when(pl.program_id(1) == 0)
    def _():
        st_ref[...] = jnp.zeros_like(st_ref)

    row = lax.broadcasted_iota(jnp.int32, (C, C), 0)
    col = lax.broadcasted_iota(jnp.int32, (C, C), 1)
    incl = (row <= col) if reverse else (row >= col)
    strict = (row < col) if reverse else (row > col)
    tri = incl.astype(F32)

    def chunk(ci, carry):
        c = (n_chunks - 1 - ci) if reverse else ci
        rows = pl.ds(pl.multiple_of(c * C, C), C)
        for h in range(R_HEADS):
            sl = slice(h * R_HEAD, (h + 1) * R_HEAD)
            lw = lw_ref[rows, sl]
            gi = _dot(tri, lw, precision=HIGHEST)
            ge = gi - lw
            g_tot = gi[0:1] if reverse else gi[C - 1:C]
            v = v_ref[rows, sl]
            inv = jnp.exp(-gi)
            a_t = -kkn_ref[rows, sl] * jnp.exp(ge)
            r_t = r_ref[rows, sl] * jnp.exp(gi)
            b_t = b_ref[rows, sl] * inv
            k_t = kd_ref[rows, sl] * inv
            ar = jnp.concatenate([a_t, r_t], axis=0)
            p_b = _bdot(ar, b_t, trans_b=True)
            p_k = _bdot(ar, k_t, trans_b=True)
            st = st_ref[h]
            ah = _bdot(ar, st, trans_b=True)
            n_ab = jnp.where(strict, p_b[:C], 0.0)
            u = ah[:C] + _bdot(jnp.where(strict, p_k[:C], 0.0), v)
            npow = n_ab
            steps = int(math.log2(C))
            for it in range(steps):
                u = u + _bdot(npow, u)
                if it + 1 < steps:
                    npow = _bdot(npow, npow)
            o = ah[C:] + _bdot(jnp.where(incl, p_b[C:], 0.0), u) + _bdot(jnp.where(incl, p_k[C:], 0.0), v)
            tail = jnp.exp(g_tot - gi)
            st_ref[h] = (st * jnp.exp(g_tot)
                         + _bdot(u, b_ref[rows, sl] * tail, trans_a=True)
                         + _bdot(v, kd_ref[rows, sl] * tail, trans_a=True))
            if finish:
                o = o + of_ref[rows, sl]
                mu = jnp.mean(o, axis=1, keepdims=True)
                oc = o - mu
                var = jnp.mean(oc * oc, axis=1, keepdims=True)
                o = oc * lax.rsqrt(var + RWKV_GN_EPS) * lng_ref[:, sl] + lnb_ref[:, sl]
                o = (o + bonus_ref[rows, sl]) * g_ref[rows, sl]
            o_ref[rows, sl] = o
        return carry

    lax.fori_loop(0, n_chunks, chunk, 0)


def _rwkv_scan(reverse, r, kd, v, lw, kkn, bvec, finish_args=None):
    m, w = r.shape
    seq_spec = pl.BlockSpec((ROW_TILE, w), lambda b, t: (_seq_block(b, _tile_order(t, reverse)), 0))
    vec_spec = pl.BlockSpec((1, w), lambda b, t: (0, 0))
    ins = [r, kd, v, lw, kkn, bvec]
    specs = [seq_spec] * 6
    if finish_args is not None:
        ins += list(finish_args)
        specs += [seq_spec, seq_spec, seq_spec, vec_spec, vec_spec]
    return pl.pallas_call(
        functools.partial(_rwkv_scan_kernel, reverse, finish_args is not None),
        grid=(BATCH, SEQ_TILES),
        in_specs=specs,
        out_specs=seq_spec,
        out_shape=jax.ShapeDtypeStruct((m, w), F32),
        scratch_shapes=[pltpu.VMEM((R_HEADS, R_HEAD, R_HEAD), F32)],
        compiler_params=_params("parallel", "arbitrary"),
        name="rwkv_bwd" if reverse else "rwkv_fwd",
    )(*ins)


def _rwkv7(p, mu, w0, w2, a0, a2, g2, k_k, k_a, r_k, lnx_g, lnx_b):
    zeros_w = jnp.zeros((R_LORA_W, R_WIDTH), F32)
    w2pad = jnp.stack([jnp.concatenate([w2[0], zeros_w], 0), jnp.concatenate([zeros_w, w2[1]], 0)]).astype(_MXU_DTYPE)
    a2pad = jnp.stack([jnp.concatenate([a2[0], zeros_w], 0), jnp.concatenate([zeros_w, a2[1]], 0)]).astype(_MXU_DTYPE)
    head = np.arange(R_WIDTH) // R_HEAD
    ones_blk = jnp.asarray(head[:, None] == head[None, :], F32)
    row = lambda t: t.astype(F32).reshape(1, -1)
    r, v, g, bonus, kkn, lwf, kdf, bf, lwb, kdb, bb = _rwkv_prep(
        p, row(mu), w0.astype(F32), w2pad, a0.astype(F32), a2pad, g2.astype(_MXU_DTYPE),
        row(k_k), row(k_a), row(r_k), ones_blk)
    o_fw = _rwkv_scan(False, r, kdf, v, lwf, kkn, bf)
    return _rwkv_scan(True, r, kdb, v, lwb, kkn, bb, (o_fw, g, bonus, row(lnx_g), row(lnx_b)))


PROJ_TM = 256
FFN_TM = 512
FFN_TF = 256
EVEN_SPLITS = (A_WIDTH,) * 5 + (B_QW, B_KVW, B_KVW)
ODD_SPLITS = (C_WIDTH, R_IN)


def _mod_tiles(vec9, tm):
    idx = np.concatenate([np.full(CTX_ROWS // tm, BATCH), np.repeat(np.arange(BATCH), SEQ // tm)])
    return vec9[idx][:, None, :]


def kernel(x, c, ctx, c_ctx, ada_w, ada_b, ln_g, ln_b, ev_w_in, ev_w_out, hg_lb, hg_norm_g, attn_sink, ffn_w_gate, ffn_w_up, ffn_w_down, od_w_in, od_w_out, s5_lam_re, s5_lam_im, s5_log_dt, s5_b_re, s5_b_im, s5_c_re, s5_c_im, s5_d, s5_glu_w, rwkv_mu, rwkv_w0, rwkv_w2, rwkv_a0, rwkv_a2, rwkv_g2, rwkv_k_k, rwkv_k_a, rwkv_r_k, rwkv_ln_g, rwkv_ln_b, moe_router_w, moe_router_b, moe_w_gate, moe_w_up, moe_w_down):
    d = D_MODEL
    xs = jnp.concatenate([ctx.reshape(CTX_ROWS, d), x.reshape(BATCH * SEQ, d)], axis=0).astype(F32)
    cond = jnp.concatenate([c, c_ctx[None, :], jnp.zeros((16 - BATCH - 1, d), F32)], axis=0)
    ada = _ada_all(cond, ada_w, ada_b)
    lb_soft = jax.nn.softmax(hg_lb.astype(F32), axis=0)
    lb_all = jnp.cumsum(lb_soft, axis=0) - lb_soft[0:1]
    no_route = jnp.zeros((M_ROWS, 128), F32)

    for layer in range(DEPTH):
        j = layer // 2
        sh1, sc1, gt1, sh2, sc2, gt2 = [ada[layer, :BATCH + 1, n * d:(n + 1) * d] for n in range(6)]
        if layer % 2 == 0:
            q, f_fw, f_bw, i_in, g, aq, ak, av = _mod_matmul(
                xs, _mod_tiles(1.0 + sc1, PROJ_TM), _mod_tiles(sh1, PROJ_TM), ev_w_in[j].astype(BF16), EVEN_SPLITS, PROJ_TM)
            y1 = _hgrn2(q, f_fw, f_bw, i_in, g, lb_all[j], hg_norm_g[j])
            y2 = _window_attention(aq, ak, av, attn_sink[j])
            w_out = ev_w_out[j]
        else:
            u, p_rw = _mod_matmul(
                xs, _mod_tiles(1.0 + sc1, PROJ_TM), _mod_tiles(sh1, PROJ_TM), od_w_in[j].astype(BF16), ODD_SPLITS, PROJ_TM)
            y1 = _s5(u, s5_lam_re[j], s5_lam_im[j], s5_log_dt[j], s5_b_re[j], s5_b_im[j], s5_c_re[j], s5_c_im[j],
                     s5_d[j], s5_glu_w[j])
            y2 = _rwkv7(p_rw, rwkv_mu[j], rwkv_w0[j], rwkv_w2[j], rwkv_a0[j], rwkv_a2[j], rwkv_g2[j],
                        rwkv_k_k[j], rwkv_k_a[j], rwkv_r_k[j], rwkv_ln_g[j], rwkv_ln_b[j])
            w_out = od_w_out[j]
        xs = _out_proj_ln(y1, y2, w_out.astype(BF16), xs, _mod_tiles(gt1, PROJ_TM), ln_g[layer, 0], ln_b[layer, 0], PROJ_TM)
        scale2, shift2, gate2 = _mod_tiles(1.0 + sc2, FFN_TM), _mod_tiles(sh2, FFN_TM), _mod_tiles(gt2, FFN_TM)
        if layer % 2 == 0:
            xs = _ffn_ln(xs, scale2, shift2, gate2, no_route, ffn_w_gate[j][None].astype(BF16),
                         ffn_w_up[j][None].astype(BF16), ffn_w_down[j][None].astype(BF16),
                         ln_g[layer, 1], ln_b[layer, 1], FFN_TM, FFN_TF)
        else:
            w_pad = jnp.pad(moe_router_w[j].astype(F32), ((0, 0), (0, 128 - N_EXPERTS)))
            b_pad = jnp.pad(moe_router_b[j].astype(F32), (0, 128 - N_EXPERTS)).reshape(1, 128)
            route = _router(xs, scale2, shift2, w_pad, b_pad, FFN_TM)
            xs = _ffn_ln(xs, scale2, shift2, gate2, route, moe_w_gate[j].astype(BF16), moe_w_up[j].astype(BF16),
                         moe_w_down[j].astype(BF16), ln_g[layer, 1], ln_b[layer, 1], FFN_TM, FFN_TF)
    return xs[CTX_ROWS:].reshape(BATCH, SEQ, d)
```

```python
import functools
import math

import numpy as np
import jax
import jax.numpy as jnp
from jax import lax
from jax.experimental import pallas as pl
from jax.experimental.pallas import tpu as pltpu

F32 = jnp.float32
BF16 = jnp.bfloat16
HIGHEST = lax.Precision.HIGHEST

D_MODEL = 1024
BATCH = 8
SEQ = 2048
CTX_LEN = 256
DEPTH = 4
GRID_W = 64
ROW_TILE = 256
SEQ_TILES = (CTX_LEN + SEQ) // ROW_TILE
CTX_ROWS = BATCH * CTX_LEN
M_ROWS = BATCH * (CTX_LEN + SEQ)

A_HEADS, A_DK, A_WIDTH = 4, 128, 512
B_HEADS, B_KV_HEADS, B_HEAD_DIM = 8, 2, 64
B_QW, B_KVW = 512, 128
B_BLOCK = 128
ROPE_BASE = 10000.0
C_GROUP, C_GROUPS, C_WIDTH, C_STATE = 16, 32, 512, 64
R_HEADS, R_HEAD, R_WIDTH = 8, 64, 512
R_LORA_W, R_LORA_A, R_LORA_G = 64, 64, 128
R_IN = 3 * R_WIDTH + 2 * R_LORA_W + 2 * R_LORA_A + R_LORA_G
D_FF = 2816
N_EXPERTS = 8
DEEPNORM_ALPHA = (2 * DEPTH) ** 0.25
LN_EPS = 1e-5
RWKV_GN_EPS = 64e-5

VMEM_LIMIT_BYTES = 56 * 1024 * 1024


def _params(*sem):
    return pltpu.CompilerParams(dimension_semantics=sem, vmem_limit_bytes=VMEM_LIMIT_BYTES)


def _dot(a, b, *, trans_a=False, trans_b=False, precision=None):
    dn = (((0 if trans_a else 1,), (1 if trans_b else 0,)), ((), ()))
    return lax.dot_general(a, b, dn, preferred_element_type=F32, precision=precision)


_MXU_DTYPE = BF16


def _bdot(a, b, **kw):
    return _dot(a.astype(_MXU_DTYPE), b.astype(_MXU_DTYPE), **kw)


def _sigmoid(x):
    return 1.0 / (1.0 + jnp.exp(-x))


def _silu(x):
    return x * _sigmoid(x)


def _seq_block(b, i):
    return jnp.where(i == 0, b, BATCH + b * (SEQ // ROW_TILE) + i - 1)


def _tile_order(i, reverse):
    if not reverse:
        return i
    return jnp.where(i == 0, 0, SEQ_TILES - i)


def _ada_kernel(cond_ref, w_ref, b_ref, o_ref):
    o_ref[0] = _dot(_silu(cond_ref[...]), w_ref[0], precision=HIGHEST) + b_ref[0]


def _ada_all(cond, ada_w, ada_b):
    rows = cond.shape[0]
    tn = 1536
    return pl.pallas_call(
        _ada_kernel,
        grid=(DEPTH, 6 * D_MODEL // tn),
        in_specs=[pl.BlockSpec((rows, D_MODEL), lambda l, j: (0, 0)),
                  pl.BlockSpec((1, D_MODEL, tn), lambda l, j: (l, 0, j)),
                  pl.BlockSpec((1, 1, tn), lambda l, j: (l, 0, j))],
        out_specs=pl.BlockSpec((1, rows, tn), lambda l, j: (l, 0, j)),
        out_shape=jax.ShapeDtypeStruct((DEPTH, rows, 6 * D_MODEL), F32),
        compiler_params=_params("parallel", "parallel"),
        name="ada",
    )(cond, ada_w, ada_b.reshape(DEPTH, 1, 6 * D_MODEL))


def _mod_matmul_kernel(splits, x_ref, sc_ref, sh_ref, w_ref, *o_refs):
    h = (x_ref[...] * sc_ref[0] + sh_ref[0]).astype(BF16)
    off = 0
    for o_ref, width in zip(o_refs, splits):
        o_ref[...] = _dot(h, w_ref[:, off:off + width])
        off += width


def _mod_matmul(x, scale_t, shift_t, w_bf16, splits, tm):
    m, d = x.shape
    n = w_bf16.shape[1]
    assert sum(splits) == n and m % tm == 0
    return pl.pallas_call(
        functools.partial(_mod_matmul_kernel, splits),
        grid=(m // tm,),
        in_specs=[pl.BlockSpec((tm, d), lambda i: (i, 0)),
                  pl.BlockSpec((1, 1, d), lambda i: (i, 0, 0)),
                  pl.BlockSpec((1, 1, d), lambda i: (i, 0, 0)),
                  pl.BlockSpec((d, n), lambda i: (0, 0))],
        out_specs=[pl.BlockSpec((tm, w), lambda i: (i, 0)) for w in splits],
        out_shape=[jax.ShapeDtypeStruct((m, w), F32) for w in splits],
        compiler_params=_params("parallel"),
        name="mod_matmul",
    )(x, scale_t, shift_t, w_bf16)


def _layer_norm_rows(z, g, b):
    mu = jnp.mean(z, axis=-1, keepdims=True)
    zc = z - mu
    var = jnp.mean(zc * zc, axis=-1, keepdims=True)
    return zc * lax.rsqrt(var + LN_EPS) * g + b


def _out_proj_kernel(y1_ref, y2_ref, w_ref, x_ref, gt_ref, g_ref, b_ref, o_ref):
    k1 = y1_ref.shape[1]
    proj = _bdot(y1_ref[...], w_ref[:k1, :]) + _bdot(y2_ref[...], w_ref[k1:, :])
    z = DEEPNORM_ALPHA * x_ref[...] + gt_ref[0] * proj
    o_ref[...] = _layer_norm_rows(z, g_ref[...], b_ref[...])


def _out_proj_ln(y1, y2, w_bf16, x, gate_t, ln_g, ln_b, tm):
    m, d = x.shape
    k1, k2 = y1.shape[1], y2.shape[1]
    return pl.pallas_call(
        _out_proj_kernel,
        grid=(m // tm,),
        in_specs=[pl.BlockSpec((tm, k1), lambda i: (i, 0)),
                  pl.BlockSpec((tm, k2), lambda i: (i, 0)),
                  pl.BlockSpec((k1 + k2, d), lambda i: (0, 0)),
                  pl.BlockSpec((tm, d), lambda i: (i, 0)),
                  pl.BlockSpec((1, 1, d), lambda i: (i, 0, 0)),
                  pl.BlockSpec((1, d), lambda i: (0, 0)),
                  pl.BlockSpec((1, d), lambda i: (0, 0))],
        out_specs=pl.BlockSpec((tm, d), lambda i: (i, 0)),
        out_shape=jax.ShapeDtypeStruct((m, d), F32),
        compiler_params=_params("parallel"),
        name="out_proj_ln",
    )(y1, y2, w_bf16, x, gate_t, ln_g.reshape(1, d), ln_b.reshape(1, d))


def _ffn_kernel(n_exp, x_ref, sc_ref, sh_ref, gt_ref, route_ref, wg_ref, wu_ref, wd_ref,
                g_ref, b_ref, o_ref, h_ref, acc_ref):
    e, f = pl.program_id(1), pl.program_id(2)

    @pl.when((e == 0) & (f == 0))
    def _():
        h_ref[...] = (x_ref[...] * sc_ref[0] + sh_ref[0]).astype(BF16)
        acc_ref[...] = jnp.zeros_like(acc_ref)

    h = h_ref[...]
    a = _silu(_dot(h, wg_ref[0])) * _dot(h, wu_ref[0])
    if n_exp > 1:
        lane = lax.broadcasted_iota(jnp.int32, route_ref.shape, 1)
        a = a * jnp.sum(jnp.where(lane == e, route_ref[...], 0.0), axis=1, keepdims=True)
    acc_ref[...] += _bdot(a, wd_ref[0])

    @pl.when((e == n_exp - 1) & (f == pl.num_programs(2) - 1))
    def _():
        z = DEEPNORM_ALPHA * x_ref[...] + gt_ref[0] * acc_ref[...]
        o_ref[...] = _layer_norm_rows(z, g_ref[...], b_ref[...])


def _ffn_ln(x, scale_t, shift_t, gate_t, route, wg, wu, wd, ln_g, ln_b, tm, tf):
    m, d = x.shape
    n_exp, _, ff = wg.shape
    assert ff % tf == 0 and m % tm == 0
    return pl.pallas_call(
        functools.partial(_ffn_kernel, n_exp),
        grid=(m // tm, n_exp, ff // tf),
        in_specs=[pl.BlockSpec((tm, d), lambda i, e, f: (i, 0)),
                  pl.BlockSpec((1, 1, d), lambda i, e, f: (i, 0, 0)),
                  pl.BlockSpec((1, 1, d), lambda i, e, f: (i, 0, 0)),
                  pl.BlockSpec((1, 1, d), lambda i, e, f: (i, 0, 0)),
                  pl.BlockSpec((tm, 128), lambda i, e, f: (i, 0)),
                  pl.BlockSpec((1, d, tf), lambda i, e, f: (e, 0, f)),
                  pl.BlockSpec((1, d, tf), lambda i, e, f: (e, 0, f)),
                  pl.BlockSpec((1, tf, d), lambda i, e, f: (e, f, 0)),
                  pl.BlockSpec((1, d), lambda i, e, f: (0, 0)),
                  pl.BlockSpec((1, d), lambda i, e, f: (0, 0))],
        out_specs=pl.BlockSpec((tm, d), lambda i, e, f: (i, 0)),
        out_shape=jax.ShapeDtypeStruct((m, d), F32),
        scratch_shapes=[pltpu.VMEM((tm, d), BF16), pltpu.VMEM((tm, d), F32)],
        compiler_params=_params("parallel", "arbitrary", "arbitrary"),
        name="ffn_ln",
    )(x, scale_t, shift_t, gate_t, route, wg, wu, wd, ln_g.reshape(1, d), ln_b.reshape(1, d))


def _router_kernel(x_ref, sc_ref, sh_ref, w_ref, b_ref, o_ref, sel_ref, h_ref):
    h = x_ref[...] * sc_ref[0] + sh_ref[0]
    h_ref[...] = h
    logits = _dot(h, w_ref[...], precision=HIGHEST) + b_ref[...]
    lane = lax.broadcasted_iota(jnp.int32, logits.shape, 1)
    neg = jnp.float32(-jnp.inf)
    logits = jnp.where(lane < N_EXPERTS, logits, neg)
    v1 = jnp.max(logits, axis=1, keepdims=True)
    i1 = jnp.min(jnp.where(logits == v1, lane, 128), axis=1, keepdims=True)
    rest = jnp.where(lane == i1, neg, logits)
    v2 = jnp.max(rest, axis=1, keepdims=True)
    i2 = jnp.min(jnp.where(rest == v2, lane, 128), axis=1, keepdims=True)
    e2 = jnp.exp(v2 - v1)
    p1 = 1.0 / (1.0 + e2)
    p2 = e2 / (1.0 + e2)
    o_ref[...] = jnp.where(lane == i1, p1, 0.0) + jnp.where(lane == i2, p2, 0.0)
    sel_ref[...] = ((lane == i1) | (lane == i2)).astype(F32)


def _router(x, scale_t, shift_t, w_pad, b_pad, tm):
    m, d = x.shape
    lanes = pl.BlockSpec((tm, 128), lambda i: (i, 0))
    return pl.pallas_call(
        _router_kernel,
        grid=(m // tm,),
        in_specs=[pl.BlockSpec((tm, d), lambda i: (i, 0)),
                  pl.BlockSpec((1, 1, d), lambda i: (i, 0, 0)),
                  pl.BlockSpec((1, 1, d), lambda i: (i, 0, 0)),
                  pl.BlockSpec((d, 128), lambda i: (0, 0)),
                  pl.BlockSpec((1, 128), lambda i: (0, 0))],
        out_specs=[lanes, lanes, pl.BlockSpec((tm, d), lambda i: (i, 0))],
        out_shape=[jax.ShapeDtypeStruct((m, 128), F32), jax.ShapeDtypeStruct((m, 128), F32),
                   jax.ShapeDtypeStruct((m, d), F32)],
        compiler_params=_params("parallel"),
        name="router",
    )(x, scale_t, shift_t, w_pad, b_pad)


MOE_TM = 512
MOE_ROWS = 2 * M_ROWS + N_EXPERTS * MOE_TM
MOE_TILES = MOE_ROWS // MOE_TM
MOE_FF_CHUNK = 256
MOE_COMBINE_TM = 256


def _moe_plan(route, sel):
    sel8 = sel[:, :N_EXPERTS].astype(jnp.int32)
    counts = jnp.sum(sel8, axis=0)
    rank = jnp.cumsum(sel8, axis=0) - sel8
    padded = ((counts + MOE_TM - 1) // MOE_TM) * MOE_TM
    ends = jnp.cumsum(padded)
    pos = jnp.where(sel8 > 0, (ends - padded)[None, :] + rank, MOE_ROWS)
    tok = jnp.broadcast_to(jnp.arange(M_ROWS, dtype=jnp.int32)[:, None], pos.shape)
    src_token = jnp.zeros((MOE_ROWS,), jnp.int32).at[pos.ravel()].set(tok.ravel(), mode="drop")
    row_w = jnp.zeros((MOE_ROWS,), F32).at[pos.ravel()].set(route[:, :N_EXPERTS].ravel(), mode="drop")
    tile_start = jnp.arange(MOE_TILES, dtype=jnp.int32) * MOE_TM
    tile_expert = jnp.minimum(jnp.searchsorted(ends, tile_start, side="right"), N_EXPERTS - 1).astype(jnp.int32)
    n_used = (ends[-1] // MOE_TM).astype(jnp.int32).reshape(1)
    pos2 = jnp.sort(pos, axis=1)[:, :2].astype(jnp.int32)
    return src_token, row_w.reshape(MOE_ROWS, 1), tile_expert, n_used, pos2[:, 0], pos2[:, 1]


def _row_copy(src_hbm, dst_vmem, sem, r, row):
    return pltpu.make_async_copy(src_hbm.at[pl.ds(row, 1), :], dst_vmem.at[pl.ds(r, 1), :], sem)


def _gather_start(idx_ref, base, n, src_hbm, dst_vmem, sem):
    def body(r, carry):
        _row_copy(src_hbm, dst_vmem, sem, r, idx_ref[base + r]).start()
        return carry

    lax.fori_loop(0, n, body, 0)


def _gather_wait(n, src_hbm, dst_vmem, sem):
    def body(r, carry):
        _row_copy(src_hbm, dst_vmem, sem, r, 0).wait()
        return carry

    lax.fori_loop(0, n, body, 0)


def _moe_expert_kernel(src_ref, texp_ref, nused_ref, h_hbm, w_ref, wg_ref, wu_ref, wd_ref, y_ref, hbuf, sem):
    t = pl.program_id(0)

    @pl.when(t < nused_ref[0])
    def _():
        _gather_start(src_ref, t * MOE_TM, MOE_TM, h_hbm, hbuf, sem)
        _gather_wait(MOE_TM, h_hbm, hbuf, sem)
        h = hbuf[...].astype(BF16)
        acc = jnp.zeros((MOE_TM, D_MODEL), F32)
        for c in range(D_FF // MOE_FF_CHUNK):
            cols = slice(c * MOE_FF_CHUNK, (c + 1) * MOE_FF_CHUNK)
            a = _silu(_dot(h, wg_ref[0, :, cols])) * _dot(h, wu_ref[0, :, cols])
            acc = acc + _bdot(a, wd_ref[0, cols, :])
        y_ref[...] = acc * w_ref[...]

    @pl.when(t >= nused_ref[0])
    def _():
        y_ref[...] = jnp.zeros_like(y_ref)


def _moe_experts(h, src_token, row_w, tile_expert, n_used, wg, wu, wd):
    d = h.shape[1]
    ff = wg.shape[2]
    grid_spec = pltpu.PrefetchScalarGridSpec(
        num_scalar_prefetch=3,
        grid=(MOE_TILES,),
        in_specs=[pl.BlockSpec(memory_space=pl.ANY),
                  pl.BlockSpec((MOE_TM, 1), lambda t, src, te, nu: (t, 0)),
                  pl.BlockSpec((1, d, ff), lambda t, src, te, nu: (te[t], 0, 0)),
                  pl.BlockSpec((1, d, ff), lambda t, src, te, nu: (te[t], 0, 0)),
                  pl.BlockSpec((1, ff, d), lambda t, src, te, nu: (te[t], 0, 0))],
        out_specs=pl.BlockSpec((MOE_TM, d), lambda t, src, te, nu: (t, 0)),
        scratch_shapes=[pltpu.VMEM((MOE_TM, d), F32), pltpu.SemaphoreType.DMA(())],
    )
    return pl.pallas_call(
        _moe_expert_kernel,
        grid_spec=grid_spec,
        out_shape=jax.ShapeDtypeStruct((MOE_ROWS, d), F32),
        compiler_params=_params("arbitrary"),
        name="moe_experts",
    )(src_token, tile_expert, n_used, h, row_w, wg, wu, wd)


def _moe_combine_kernel(p1_ref, p2_ref, y_hbm, x_ref, gt_ref, g_ref, b_ref, o_ref, buf1, buf2, sem):
    base = pl.program_id(0) * MOE_COMBINE_TM
    _gather_start(p1_ref, base, MOE_COMBINE_TM, y_hbm, buf1, sem.at[0])
    _gather_start(p2_ref, base, MOE_COMBINE_TM, y_hbm, buf2, sem.at[1])
    _gather_wait(MOE_COMBINE_TM, y_hbm, buf1, sem.at[0])
    _gather_wait(MOE_COMBINE_TM, y_hbm, buf2, sem.at[1])
    z = DEEPNORM_ALPHA * x_ref[...] + gt_ref[0] * (buf1[...] + buf2[...])
    o_ref[...] = _layer_norm_rows(z, g_ref[...], b_ref[...])


def _moe_combine_ln(y_sorted, pos1, pos2, x, gate_t, ln_g, ln_b):
    m, d = x.shape
    tm = MOE_COMBINE_TM
    grid_spec = pltpu.PrefetchScalarGridSpec(
        num_scalar_prefetch=2,
        grid=(m // tm,),
        in_specs=[pl.BlockSpec(memory_space=pl.ANY),
                  pl.BlockSpec((tm, d), lambda i, p1, p2: (i, 0)),
                  pl.BlockSpec((1, 1, d), lambda i, p1, p2: (i, 0, 0)),
                  pl.BlockSpec((1, d), lambda i, p1, p2: (0, 0)),
                  pl.BlockSpec((1, d), lambda i, p1, p2: (0, 0))],
        out_specs=pl.BlockSpec((tm, d), lambda i, p1, p2: (i, 0)),
        scratch_shapes=[pltpu.VMEM((tm, d), F32), pltpu.VMEM((tm, d), F32), pltpu.SemaphoreType.DMA((2,))],
    )
    return pl.pallas_call(
        _moe_combine_kernel,
        grid_spec=grid_spec,
        out_shape=jax.ShapeDtypeStruct((m, d), F32),
        compiler_params=_params("arbitrary"),
        name="moe_combine_ln",
    )(pos1, pos2, y_sorted, x, gate_t, ln_g.reshape(1, d), ln_b.reshape(1, d))


HGRN_CHUNK = 16
HGRN_SUPER = 64


def _hgrn_kernel(reverse, finish, *refs):
    if finish:
        q_ref, f_ref, i_ref, lb_ref, of_ref, g_ref, ng_ref, o_ref, st_ref = refs
    else:
        q_ref, f_ref, i_ref, lb_ref, o_ref, st_ref = refs
    C, S = HGRN_CHUNK, HGRN_SUPER
    nsub = S // C
    n_super = ROW_TILE // S

    @pl.when(pl.program_id(1) == 0)
    def _():
        st_ref[...] = jnp.zeros_like(st_ref)

    row = lax.broadcasted_iota(jnp.int32, (S, S), 0)
    col = lax.broadcasted_iota(jnp.int32, (S, S), 1)
    blk_r, blk_c = jnp.zeros_like(row), jnp.zeros_like(col)
    for j in range(1, nsub):
        blk_r = blk_r + (row >= j * C).astype(jnp.int32)
        blk_c = blk_c + (col >= j * C).astype(jnp.int32)
    col_in = col - blk_c * C
    keep = (blk_r == blk_c) & ((row <= col) if reverse else (row >= col))
    tri = keep.astype(F32)
    heads = range(A_HEADS)
    sls = [slice(h * A_DK, (h + 1) * A_DK) for h in heads]
    chunk_rows = [slice(c * C, (c + 1) * C) for c in range(nsub)]
    width = A_HEADS * A_DK

    def per_chunk_row(t, offset):
        return jnp.concatenate(
            [jnp.broadcast_to(t[c * C + offset:c * C + offset + 1], (C, width)) for c in range(nsub)], axis=0)

    def superchunk(si, carry):
        sc = (n_super - 1 - si) if reverse else si
        rows = pl.ds(pl.multiple_of(sc * S, S), S)
        lb = lb_ref[...]
        q = _silu(q_ref[rows, :]) * A_DK ** -0.5
        f = lb + (1.0 - lb) * _sigmoid(f_ref[rows, :])
        k = 1.0 - f
        v = i_ref[rows, :]
        b = _dot(tri, jnp.log(f), precision=HIGHEST)
        last = 0 if reverse else C - 1
        b_tot = per_chunk_row(b, last)
        qe = q * jnp.exp(b)
        kt = k * jnp.exp(b_tot - b)
        atts = [jnp.zeros((S, S), F32) for _ in heads]
        for s in range(C):
            tmp = q * jnp.exp(jnp.minimum(b - per_chunk_row(b, s), 0.0)) * per_chunk_row(k, s)
            atts = [jnp.where(col_in == s, jnp.sum(tmp[:, sl], axis=1, keepdims=True), att)
                    for sl, att in zip(sls, atts)]
        intra = [_bdot(jnp.where(keep, att, 0.0), v[:, sl]) for sl, att in zip(sls, atts)]
        upds = [[_bdot(v[cr, sl], kt[cr, sl], trans_a=True) for cr in chunk_rows] for sl in sls]
        sts = [st_ref[h] for h in heads]
        inter = [[None] * nsub for _ in heads]
        for c in (range(nsub - 1, -1, -1) if reverse else range(nsub)):
            cr = chunk_rows[c]
            decay = jnp.exp(b[c * C + last:c * C + last + 1])
            for h in heads:
                inter[h][c] = _bdot(qe[cr, sls[h]], sts[h], trans_b=True)
                sts[h] = sts[h] * decay[:, sls[h]] + upds[h][c]
        outs = []
        for h in heads:
            st_ref[h] = sts[h]
            o = intra[h] + jnp.concatenate(inter[h], axis=0)
            if finish:
                o = o + of_ref[rows, sls[h]]
                o = o * lax.rsqrt(jnp.mean(o * o, axis=1, keepdims=True) + 1e-6)
                o = o * ng_ref[:, sls[h]] * _silu(g_ref[rows, sls[h]])
            outs.append(o)
        for h in heads:
            o_ref[rows, sls[h]] = outs[h]
        return carry

    lax.fori_loop(0, n_super, superchunk, 0)


def _hgrn_pass(reverse, q, f, i, lb, finish_args=None):
    m, w = q.shape
    seq_spec = pl.BlockSpec((ROW_TILE, w), lambda b, t: (_seq_block(b, _tile_order(t, reverse)), 0))
    vec_spec = pl.BlockSpec((1, w), lambda b, t: (0, 0))
    ins = [q, f, i, lb]
    specs = [seq_spec, seq_spec, seq_spec, vec_spec]
    if finish_args is not None:
        of, g, ng = finish_args
        ins += [of, g, ng]
        specs += [seq_spec, seq_spec, vec_spec]
    return pl.pallas_call(
        functools.partial(_hgrn_kernel, reverse, finish_args is not None),
        grid=(BATCH, SEQ_TILES),
        in_specs=specs,
        out_specs=seq_spec,
        out_shape=jax.ShapeDtypeStruct((m, w), F32),
        scratch_shapes=[pltpu.VMEM((A_HEADS, A_DK, A_DK), F32)],
        compiler_params=_params("parallel", "arbitrary"),
        name="hgrn_bwd" if reverse else "hgrn_fwd",
    )(*ins)


def _rope_kernel(q_ref, k_ref, cos_ref, sa_ref, sb_ref, qo_ref, ko_ref):
    cos, sa, sb = cos_ref[...], sa_ref[...], sb_ref[...]

    def rot(x):
        return x * cos + pltpu.roll(x, 96, 1) * sa + pltpu.roll(x, 32, 1) * sb

    for j in range(B_QW // 128):
        qo_ref[:, j * 128:(j + 1) * 128] = rot(q_ref[:, j * 128:(j + 1) * 128])
    ko_ref[...] = rot(k_ref[...])


def _rope(q, k, cos_t, sa_t, sb_t):
    m = q.shape[0]
    seq = lambda w: pl.BlockSpec((ROW_TILE, w), lambda b, t: (_seq_block(b, t), 0))
    tab = pl.BlockSpec((ROW_TILE, 128), lambda b, t: (t, 0))
    return pl.pallas_call(
        _rope_kernel,
        grid=(BATCH, SEQ_TILES),
        in_specs=[seq(B_QW), seq(B_KVW), tab, tab, tab],
        out_specs=[seq(B_QW), seq(B_KVW)],
        out_shape=[jax.ShapeDtypeStruct((m, B_QW), F32), jax.ShapeDtypeStruct((m, B_KVW), F32)],
        compiler_params=_params("parallel", "parallel"),
        name="rope",
    )(q, k, cos_t, sa_t, sb_t)


def _rope_tables():
    rows = SEQ // GRID_W
    row = jnp.repeat(jnp.arange(rows, dtype=F32), GRID_W)
    colp = jnp.tile(jnp.arange(GRID_W, dtype=F32), rows)
    n_freq = B_HEAD_DIM // 4
    inv = ROPE_BASE ** (-jnp.arange(n_freq, dtype=F32) / n_freq)
    ang = jnp.concatenate([row[:, None] * inv, colp[:, None] * inv], axis=-1)
    cos, sin = jnp.cos(ang), jnp.sin(ang)
    zero = jnp.zeros_like(sin)
    cos_l = jnp.tile(cos, (1, 4))
    sa_l = jnp.tile(jnp.concatenate([-sin, zero], axis=-1), (1, 2))
    sb_l = jnp.tile(jnp.concatenate([zero, sin], axis=-1), (1, 2))
    pad = lambda t, v: jnp.concatenate([jnp.full((CTX_LEN, 128), v, F32), t], axis=0)
    return pad(cos_l, 1.0), pad(sa_l, 0.0), pad(sb_l, 0.0)


def _attend(q_ref, o_ref, sink_ref, segments):
    tq = q_ref.shape[0]
    lane = lax.broadcasted_iota(jnp.int32, (tq, 128), 1)
    left = lane < B_HEAD_DIM
    neg = jnp.float32(-jnp.inf)
    scale = B_HEAD_DIM ** -0.5
    dup = []
    for g in range(B_KV_HEADS):
        segs = []
        for k, v, mask in segments:
            lk = lax.broadcasted_iota(jnp.int32, k.shape, 1) < B_HEAD_DIM
            kr, vr = pltpu.roll(k, 64, 1), pltpu.roll(v, 64, 1)
            kd = jnp.where(lk, k, kr) if g == 0 else jnp.where(lk, kr, k)
            vd = jnp.where(lk, v, vr) if g == 0 else jnp.where(lk, vr, v)
            segs.append((kd.astype(_MXU_DTYPE), vd, lk, mask))
        dup.append(segs)
    pairs = B_HEADS // 2
    for p in range(pairs):
        segs = dup[p // (pairs // B_KV_HEADS)]
        q_pair = q_ref[:, p * 128:(p + 1) * 128] * scale
        out = jnp.zeros((tq, 128), F32)
        for half in range(2):
            sel = left if half == 0 else jnp.logical_not(left)
            sink = sink_ref[2 * p + half:2 * p + half + 1, 0:1]
            qm = jnp.where(sel, q_pair, 0.0).astype(_MXU_DTYPE)
            scores = []
            for kd, vd, lk, mask in segs:
                s = _dot(qm, kd, trans_b=True)
                scores.append(s if mask is None else jnp.where(mask, s, neg))
            mx = sink
            for s in scores:
                mx = jnp.maximum(mx, jnp.max(s, axis=1, keepdims=True))
            denom = jnp.exp(sink - mx)
            acc = jnp.zeros((tq, 128), F32)
            for s, (kd, vd, lk, mask) in zip(scores, segs):
                e = jnp.exp(s - mx)
                denom = denom + jnp.sum(e, axis=1, keepdims=True)
                vsel = jnp.where(lk if half == 0 else jnp.logical_not(lk), vd, 0.0)
                acc = acc + _bdot(e, vsel)
            out = out + acc / denom
        o_ref[:, p * 128:(p + 1) * 128] = out


ATTN_LATENT_BLOCKS = SEQ // B_BLOCK
ATTN_CTX_BLOCKS = CTX_LEN // B_BLOCK


def _attn_kernel(q_ref, kp_ref, kc_ref, kn_ref, vp_ref, vc_ref, vn_ref, kx_ref, vx_ref, sink_ref, o_ref):
    n = pl.program_id(1)
    nb = ATTN_LATENT_BLOCKS

    @pl.when(n < nb)
    def _():
        row = lax.broadcasted_iota(jnp.int32, (B_BLOCK, B_BLOCK), 0)
        col = lax.broadcasted_iota(jnp.int32, (B_BLOCK, B_BLOCK), 1)
        segments = [
            (kp_ref[...], vp_ref[...], (col >= row) & (n > 0)),
            (kc_ref[...], vc_ref[...], None),
            (kn_ref[...], vn_ref[...], (col <= row) & (n < nb - 1)),
            (kx_ref[...], vx_ref[...], None),
        ]
        _attend(q_ref, o_ref, sink_ref, segments)

    @pl.when(n >= nb)
    def _():
        _attend(q_ref, o_ref, sink_ref, [(kx_ref[...], vx_ref[...], None)])


def _attention(q, k, v, sink_rows):
    m = q.shape[0]
    nb, nc = ATTN_LATENT_BLOCKS, ATTN_CTX_BLOCKS
    base = CTX_ROWS // B_BLOCK

    def q_block(b, n):
        return jnp.where(n < nb, base + b * nb + n, b * nc + n - nb)

    qspec = pl.BlockSpec((B_BLOCK, B_QW), lambda b, n: (q_block(b, n), 0))

    def kv(shift):
        return pl.BlockSpec((B_BLOCK, B_KVW), lambda b, n: (base + b * nb + jnp.clip(n + shift, 0, nb - 1), 0))

    ctx_kv = pl.BlockSpec((CTX_LEN, B_KVW), lambda b, n: (b, 0))
    sink_spec = pl.BlockSpec((B_HEADS, 128), lambda b, n: (0, 0))
    return pl.pallas_call(
        _attn_kernel,
        grid=(BATCH, nb + nc),
        in_specs=[qspec, kv(-1), kv(0), kv(1), kv(-1), kv(0), kv(1), ctx_kv, ctx_kv, sink_spec],
        out_specs=qspec,
        out_shape=jax.ShapeDtypeStruct((m, B_QW), F32),
        compiler_params=_params("parallel", "parallel"),
        name="attention",
    )(q, k, k, k, v, v, v, k, v, sink_rows)


def _window_attention(q, k, v, sink):
    cos_t, sa_t, sb_t = _rope_tables()
    qr, kr = _rope(q, k, cos_t, sa_t, sb_t)
    sink_rows = jnp.broadcast_to(sink.astype(F32)[:, None], (B_HEADS, 128))
    return _attention(qr, kr, v, sink_rows)


def _hgrn2(q, f_fw, f_bw, i, g, lb, norm_g):
    lb = lb.reshape(1, A_WIDTH)
    o_fw = _hgrn_pass(False, q, f_fw, i, lb)
    return _hgrn_pass(True, q, f_bw, i, lb, (o_fw, g, norm_g.reshape(1, A_WIDTH)))


S5_STEPS = 64
S5_ROWS = S5_STEPS * BATCH
S5_TILES = (CTX_LEN + SEQ) // S5_STEPS
S5_CTX_TILES = CTX_LEN // S5_STEPS
S5_BLOCKS = 4
S5_BLOCK_IN = C_WIDTH // S5_BLOCKS
S5_BLOCK_STATE = C_GROUPS * C_STATE // S5_BLOCKS


def _s5_kernel(reverse, finish, *refs):
    if finish:
        u_ref, a_ref, wb_ref, wc_ref, yf_ref, glu_ref, y_ref, x_ref, st_ref = refs
    else:
        u_ref, a_ref, wb_ref, wc_ref, d_ref, y_ref, x_ref, st_ref = refs
    ns = S5_BLOCK_STATE

    @pl.when(pl.program_id(0) == 0)
    def _():
        st_ref[...] = jnp.zeros_like(st_ref)

    for k in range(S5_BLOCKS):
        x_ref[:, 2 * ns * k:2 * ns * (k + 1)] = _bdot(u_ref[:, S5_BLOCK_IN * k:S5_BLOCK_IN * (k + 1)], wb_ref[k])

    def step(tt, carry):
        t = (S5_STEPS - 1 - tt) if reverse else tt
        rows = pl.ds(pl.multiple_of(t * BATCH, BATCH), BATCH)
        for k in range(S5_BLOCKS):
            re = slice(2 * ns * k, 2 * ns * k + ns)
            im = slice(2 * ns * k + ns, 2 * ns * (k + 1))
            ar, ai = a_ref[:, re], a_ref[:, im]
            sr, si = st_ref[:, re], st_ref[:, im]
            nr = ar * sr - ai * si + x_ref[rows, re]
            ni = ar * si + ai * sr + x_ref[rows, im]
            st_ref[:, re] = nr
            st_ref[:, im] = ni
            x_ref[rows, re] = nr
            x_ref[rows, im] = ni
        return carry

    lax.fori_loop(0, S5_STEPS, step, 0)

    for k in range(S5_BLOCKS):
        cols = slice(S5_BLOCK_IN * k, S5_BLOCK_IN * (k + 1))
        y = _bdot(x_ref[:, 2 * ns * k:2 * ns * (k + 1)], wc_ref[k])
        if finish:
            y_ref[:, cols] = y + yf_ref[:, cols]
        else:
            y_ref[:, cols] = y + d_ref[:, cols] * u_ref[:, cols]
    if finish:
        y = jax.nn.gelu(y_ref[...])
        y_ref[...] = y * _sigmoid(_bdot(y, glu_ref[...]))


def _s5_tile_order(i, reverse):
    if not reverse:
        return i
    return jnp.where(i < S5_CTX_TILES, S5_CTX_TILES - 1 - i, S5_TILES + S5_CTX_TILES - 1 - i)


def _s5_pass(reverse, u_tm, acoef, wb, wc, extra):
    m, w = u_tm.shape
    nstate = 2 * C_GROUPS * C_STATE
    row_spec = pl.BlockSpec((S5_ROWS, w), lambda i: (_s5_tile_order(i, reverse), 0))
    full = lambda a: pl.BlockSpec(a.shape, lambda i: (0,) * a.ndim)
    finish = reverse
    if finish:
        yf, glu_w = extra
        ins, specs = [u_tm, acoef, wb, wc, yf, glu_w], [row_spec, full(acoef), full(wb), full(wc), row_spec, full(glu_w)]
    else:
        (dskip,) = extra
        ins, specs = [u_tm, acoef, wb, wc, dskip], [row_spec, full(acoef), full(wb), full(wc), full(dskip)]
    return pl.pallas_call(
        functools.partial(_s5_kernel, reverse, finish),
        grid=(S5_TILES,),
        in_specs=specs,
        out_specs=row_spec,
        out_shape=jax.ShapeDtypeStruct((m, w), F32),
        scratch_shapes=[pltpu.VMEM((S5_ROWS, nstate), F32), pltpu.VMEM((BATCH, nstate), F32)],
        compiler_params=_params("arbitrary"),
        name="s5_bwd" if reverse else "s5_fwd",
    )(*ins)


def _s5_discretize(lam_re, lam_im, log_dt, b_re, b_im):
    lam_re = jnp.minimum(lam_re.astype(F32), -1e-4)
    lam_im = lam_im.astype(F32)
    dt = jnp.exp(log_dt.astype(F32))[:, None]
    mag = jnp.exp(lam_re * dt)
    ab_re, ab_im = mag * jnp.cos(lam_im * dt), mag * jnp.sin(lam_im * dt)
    den = lam_re ** 2 + lam_im ** 2
    nr = ab_re - 1.0
    co_re = (nr * lam_re + ab_im * lam_im) / den
    co_im = (ab_im * lam_re - nr * lam_im) / den
    bb_re = co_re[..., None] * b_re - co_im[..., None] * b_im
    bb_im = co_re[..., None] * b_im + co_im[..., None] * b_re
    return ab_re, ab_im, bb_re, bb_im


def _s5_tables(lam_re, lam_im, log_dt, b_re, b_im, c_re, c_im):
    eye = jnp.eye(C_GROUPS // S5_BLOCKS, dtype=F32)
    gb = C_GROUPS // S5_BLOCKS

    def in_map(bb):
        return jnp.einsum('kgph,gG->kghGp', bb.reshape(S5_BLOCKS, gb, C_STATE, C_GROUP), eye).reshape(
            S5_BLOCKS, S5_BLOCK_IN, S5_BLOCK_STATE)

    def out_map(cc):
        return jnp.einsum('kghp,gG->kgpGh', cc.reshape(S5_BLOCKS, gb, C_GROUP, C_STATE), eye).reshape(
            S5_BLOCKS, S5_BLOCK_STATE, S5_BLOCK_IN)

    wc = jnp.concatenate([out_map(c_re.astype(F32)), -out_map(c_im.astype(F32))], axis=1).astype(_MXU_DTYPE)
    tables = []
    for d in range(2):
        ab_re, ab_im, bb_re, bb_im = _s5_discretize(lam_re[d], lam_im[d], log_dt[d], b_re.astype(F32), b_im.astype(F32))
        a = jnp.concatenate([ab_re.reshape(S5_BLOCKS, S5_BLOCK_STATE), ab_im.reshape(S5_BLOCKS, S5_BLOCK_STATE)], axis=1)
        acoef = jnp.broadcast_to(a.reshape(1, -1), (BATCH, 2 * C_GROUPS * C_STATE))
        wb = jnp.concatenate([in_map(bb_re), in_map(bb_im)], axis=2).astype(_MXU_DTYPE)
        tables.append((acoef, wb))
    return tables, wc


def _to_time_major(y):
    w = y.shape[1]
    c = y[:CTX_ROWS].reshape(BATCH, CTX_LEN, w).transpose(1, 0, 2).reshape(CTX_ROWS, w)
    l = y[CTX_ROWS:].reshape(BATCH, SEQ, w).transpose(1, 0, 2).reshape(BATCH * SEQ, w)
    return jnp.concatenate([c, l], axis=0)


def _from_time_major(y):
    w = y.shape[1]
    c = y[:CTX_ROWS].reshape(CTX_LEN, BATCH, w).transpose(1, 0, 2).reshape(CTX_ROWS, w)
    l = y[CTX_ROWS:].reshape(SEQ, BATCH, w).transpose(1, 0, 2).reshape(BATCH * SEQ, w)
    return jnp.concatenate([c, l], axis=0)


def _s5(u, lam_re, lam_im, log_dt, b_re, b_im, c_re, c_im, d_skip, glu_w):
    (fw, bw), wc = _s5_tables(lam_re, lam_im, log_dt, b_re, b_im, c_re, c_im)
    u_tm = _to_time_major(u)
    y_fw = _s5_pass(False, u_tm, fw[0], fw[1], wc, (d_skip.astype(F32).reshape(1, C_WIDTH),))
    y = _s5_pass(True, u_tm, bw[0], bw[1], wc, (y_fw, glu_w.astype(_MXU_DTYPE)))
    return _from_time_major(y)


RW_LORA_OFF = 3 * R_WIDTH


def _softplus(z):
    return jnp.maximum(z, 0.0) + jnp.log1p(jnp.exp(-jnp.abs(z)))


def _rwkv_prep_kernel(p_ref, hp_ref, hn_ref, mu_ref, w0_ref, w2_ref, a0_ref, a2_ref, g2_ref, kk_ref, ka_ref, rk_ref,
                      ones_ref, r_o, v_o, g_o, bonus_o, kkn_o, lwf_o, kdf_o, bf_o, lwb_o, kdb_o, bb_o):
    i = pl.program_id(1)
    x = p_ref[...]
    rows = x.shape[0]
    rowi = lax.broadcasted_iota(jnp.int32, (rows, 1), 0)
    prev_row = jnp.where(i >= 2, hp_ref[7:8, :], 0.0)
    next_row = jnp.where((i >= 1) & (i < SEQ_TILES - 1), hn_ref[0:1, :], 0.0)
    prev = jnp.where(rowi == 0, prev_row, pltpu.roll(x, 1, 0))
    nxt = jnp.where(rowi == rows - 1, next_row, pltpu.roll(x, rows - 1, 0))
    x = x + mu_ref[...] * (0.5 * (prev + nxt) - x)

    r = x[:, 0:R_WIDTH]
    k = x[:, R_WIDTH:2 * R_WIDTH]
    v = x[:, 2 * R_WIDTH:3 * R_WIDTH]
    wd = x[:, RW_LORA_OFF:RW_LORA_OFF + 128]
    ad = x[:, RW_LORA_OFF + 128:RW_LORA_OFF + 256]
    gd = x[:, RW_LORA_OFF + 256:RW_LORA_OFF + 384]
    ones = ones_ref[...]

    r_o[...] = r
    v_o[...] = v
    g_o[...] = _bdot(_sigmoid(gd), g2_ref[...])
    bonus_o[...] = _dot(r * k * rk_ref[...], ones, precision=HIGHEST) * v
    kk = k * kk_ref[...]
    kkn = kk / jnp.maximum(jnp.sqrt(_dot(kk * kk, ones, precision=HIGHEST)), 1e-12)
    kkn_o[...] = kkn
    tw = jnp.tanh(wd)
    for d, (lw_o, kd_o, b_o) in enumerate(((lwf_o, kdf_o, bf_o), (lwb_o, kdb_o, bb_o))):
        w = -_softplus(-(w0_ref[d:d + 1, :] + _bdot(tw, w2_ref[d]))) - 0.5
        lw_o[...] = -jnp.exp(w)
        a = _sigmoid(a0_ref[d:d + 1, :] + _bdot(ad, a2_ref[d]))
        kd_o[...] = k * (1.0 + (a - 1.0) * ka_ref[...])
        b_o[...] = kkn * a


def _rwkv_prep(p, mu, w0, w2pad, a0, a2pad, g2, k_k, k_a, r_k, ones_blk):
    m, w = p.shape
    hb = ROW_TILE // 8
    seq = lambda width: pl.BlockSpec((ROW_TILE, width), lambda b, i: (_seq_block(b, i), 0))
    halo_prev = pl.BlockSpec((8, w), lambda b, i: (jnp.maximum(_seq_block(b, i) * hb - 1, 0), 0))
    halo_next = pl.BlockSpec((8, w), lambda b, i: (jnp.minimum((_seq_block(b, i) + 1) * hb, m // 8 - 1), 0))
    full = lambda a: pl.BlockSpec(a.shape, lambda b, i: (0,) * a.ndim)
    consts = [mu, w0, w2pad, a0, a2pad, g2, k_k, k_a, r_k, ones_blk]
    return pl.pallas_call(
        _rwkv_prep_kernel,
        grid=(BATCH, SEQ_TILES),
        in_specs=[seq(w), halo_prev, halo_next] + [full(c) for c in consts],
        out_specs=[seq(R_WIDTH)] * 11,
        out_shape=[jax.ShapeDtypeStruct((m, R_WIDTH), F32)] * 11,
        compiler_params=_params("parallel", "parallel"),
        name="rwkv_prep",
    )(p, p, p, *consts)


RW_CHUNK = 64


def _rwkv_scan_kernel(reverse, finish, *refs):
    if finish:
        (r_ref, kd_ref, v_ref, lw_ref, kkn_ref, b_ref, of_ref, g_ref, bonus_ref, lng_ref, lnb_ref,
         o_ref, st_ref) = refs
    else:
        r_ref, kd_ref, v_ref, lw_ref, kkn_ref, b_ref, o_ref, st_ref = refs
    C = RW_CHUNK
    P = 2 * R_HEAD
    n_chunks = ROW_TILE // C

    @pl.when(pl.program_id(1) == 0)
    def _():
        st_ref[...] = jnp.zeros_like(st_ref)

    row = lax.broadcasted_iota(jnp.int32, (C, C), 0)
    col = lax.broadcasted_iota(jnp.int32, (C, C), 1)
    incl = (row <= col) if reverse else (row >= col)
    strict = (row < col) if reverse else (row > col)
    tri = incl.astype(F32)
    row2 = lax.broadcasted_iota(jnp.int32, (C, 2 * C), 0)
    col2 = lax.broadcasted_iota(jnp.int32, (C, 2 * C), 1)
    s2 = jnp.where(col2 < C, col2, col2 - C)
    incl2 = (row2 <= s2) if reverse else (row2 >= s2)
    strict_k = ((row2 < s2) if reverse else (row2 > s2)) & (col2 >= C)
    left = lax.broadcasted_iota(jnp.int32, (C, P), 1) < R_HEAD
    left2 = lax.broadcasted_iota(jnp.int32, (2 * C, P), 1) < R_HEAD
    blockdiag = ((lax.broadcasted_iota(jnp.int32, (P, P), 0) < R_HEAD)
                 == (lax.broadcasted_iota(jnp.int32, (P, P), 1) < R_HEAD))
    steps = int(math.log2(C))

    def chunk(ci, carry):
        c = (n_chunks - 1 - ci) if reverse else ci
        rows = pl.ds(pl.multiple_of(c * C, C), C)
        lw = lw_ref[rows, :]
        kd, bv, v_all = kd_ref[rows, :], b_ref[rows, :], v_ref[rows, :]
        gi = _dot(tri, lw, precision=HIGHEST)
        g_tot = gi[0:1] if reverse else gi[C - 1:C]
        inv = jnp.exp(-gi)
        a_t = -kkn_ref[rows, :] * jnp.exp(gi - lw)
        r_t = r_ref[rows, :] * jnp.exp(gi)
        b_t, k_t = bv * inv, kd * inv
        tail = jnp.exp(g_tot - gi)
        b_tail, k_tail = bv * tail, kd * tail
        decay = jnp.exp(g_tot)
        if finish:
            of_all, g_all, bonus_all = of_ref[rows, :], g_ref[rows, :], bonus_ref[rows, :]
        pairs = range(R_HEADS // 2)
        heads = [(p, half) for p in pairs for half in range(2)]
        sls = [slice(p * P, (p + 1) * P) for p in pairs]
        sts = [st_ref[p] for p in pairs]
        vps = [v_all[:, sl] for sl in sls]
        rhss = [jnp.concatenate([b_t[:, sl], k_t[:, sl], st], axis=0).astype(_MXU_DTYPE) for sl, st in zip(sls, sts)]
        ars = [jnp.concatenate([a_t[:, sl], r_t[:, sl]], axis=0) for sl in sls]
        zvs = [jnp.concatenate([jnp.zeros((C, P), F32), vp], axis=0).astype(_MXU_DTYPE) for vp in vps]
        prods = [_dot(jnp.where(left2 if half == 0 else jnp.logical_not(left2), ars[p], 0.0).astype(_MXU_DTYPE),
                      rhss[p], trans_b=True) for p, half in heads]
        ahs = [prod[:, 2 * C:] for prod in prods]
        us = [ah[:C] + _dot(jnp.where(strict_k, prod[:C, :2 * C], 0.0).astype(_MXU_DTYPE), zvs[p])
              for (p, half), prod, ah in zip(heads, prods, ahs)]
        npows = [jnp.where(strict, prod[:C, :C], 0.0) for prod in prods]
        for it in range(steps):
            us = [u + _bdot(npow, u) for u, npow in zip(us, npows)]
            if it + 1 < steps:
                npows = [_bdot(npow, npow) for npow in npows]
        os_ = [ah[C:] + _bdot(jnp.where(incl2, prod[C:, :2 * C], 0.0), jnp.concatenate([u, vps[p]], axis=0))
               for (p, half), prod, ah, u in zip(heads, prods, ahs, us)]
        outs = []
        for p in pairs:
            sl, st, vp = sls[p], sts[p], vps[p]
            u_pair = jnp.where(left, us[2 * p], us[2 * p + 1])
            o = jnp.where(left, os_[2 * p], os_[2 * p + 1])
            upd = _bdot(jnp.concatenate([u_pair, vp], axis=0),
                        jnp.concatenate([b_tail[:, sl], k_tail[:, sl]], axis=0), trans_a=True)
            st_ref[p] = jnp.where(blockdiag, st * decay[:, sl] + upd, 0.0)
            if finish:
                o = o + of_all[:, sl]

                def head_mean(t):
                    tot = jnp.sum(t, axis=1, keepdims=True)
                    lsum = jnp.sum(jnp.where(left, t, 0.0), axis=1, keepdims=True)
                    return jnp.where(left, lsum, tot - lsum) * (1.0 / R_HEAD)

                oc = o - head_mean(o)
                o = oc * lax.rsqrt(head_mean(oc * oc) + RWKV_GN_EPS) * lng_ref[:, sl] + lnb_ref[:, sl]
                o = (o + bonus_all[:, sl]) * g_all[:, sl]
            outs.append(o)
        for p, o in enumerate(outs):
            o_ref[rows, p * P:(p + 1) * P] = o
        return carry

    lax.fori_loop(0, n_chunks, chunk, 0)


def _rwkv_scan(reverse, r, kd, v, lw, kkn, bvec, finish_args=None):
    m, w = r.shape
    seq_spec = pl.BlockSpec((ROW_TILE, w), lambda b, t: (_seq_block(b, _tile_order(t, reverse)), 0))
    vec_spec = pl.BlockSpec((1, w), lambda b, t: (0, 0))
    ins = [r, kd, v, lw, kkn, bvec]
    specs = [seq_spec] * 6
    if finish_args is not None:
        ins += list(finish_args)
        specs += [seq_spec, seq_spec, seq_spec, vec_spec, vec_spec]
    return pl.pallas_call(
        functools.partial(_rwkv_scan_kernel, reverse, finish_args is not None),
        grid=(BATCH, SEQ_TILES),
        in_specs=specs,
        out_specs=seq_spec,
        out_shape=jax.ShapeDtypeStruct((m, w), F32),
        scratch_shapes=[pltpu.VMEM((R_HEADS // 2, 2 * R_HEAD, 2 * R_HEAD), F32)],
        compiler_params=_params("parallel", "arbitrary"),
        name="rwkv_bwd" if reverse else "rwkv_fwd",
    )(*ins)


def _rwkv7(p, mu, w0, w2, a0, a2, g2, k_k, k_a, r_k, lnx_g, lnx_b):
    zeros_w = jnp.zeros((R_LORA_W, R_WIDTH), F32)
    w2pad = jnp.stack([jnp.concatenate([w2[0], zeros_w], 0), jnp.concatenate([zeros_w, w2[1]], 0)]).astype(_MXU_DTYPE)
    a2pad = jnp.stack([jnp.concatenate([a2[0], zeros_w], 0), jnp.concatenate([zeros_w, a2[1]], 0)]).astype(_MXU_DTYPE)
    head = np.arange(R_WIDTH) // R_HEAD
    ones_blk = jnp.asarray(head[:, None] == head[None, :], F32)
    row = lambda t: t.astype(F32).reshape(1, -1)
    r, v, g, bonus, kkn, lwf, kdf, bf, lwb, kdb, bb = _rwkv_prep(
        p, row(mu), w0.astype(F32), w2pad, a0.astype(F32), a2pad, g2.astype(_MXU_DTYPE),
        row(k_k), row(k_a), row(r_k), ones_blk)
    o_fw = _rwkv_scan(False, r, kdf, v, lwf, kkn, bf)
    return _rwkv_scan(True, r, kdb, v, lwb, kkn, bb, (o_fw, g, bonus, row(lnx_g), row(lnx_b)))


PROJ_TM = 256
FFN_TM = 512
FFN_TF = 256
EVEN_SPLITS = (A_WIDTH,) * 5 + (B_QW, B_KVW, B_KVW)
ODD_SPLITS = (C_WIDTH, R_IN)


def _mod_tiles(vec9, tm):
    idx = np.concatenate([np.full(CTX_ROWS // tm, BATCH), np.repeat(np.arange(BATCH), SEQ // tm)])
    return vec9[idx][:, None, :]


def kernel(x, c, ctx, c_ctx, ada_w, ada_b, ln_g, ln_b, ev_w_in, ev_w_out, hg_lb, hg_norm_g, attn_sink, ffn_w_gate, ffn_w_up, ffn_w_down, od_w_in, od_w_out, s5_lam_re, s5_lam_im, s5_log_dt, s5_b_re, s5_b_im, s5_c_re, s5_c_im, s5_d, s5_glu_w, rwkv_mu, rwkv_w0, rwkv_w2, rwkv_a0, rwkv_a2, rwkv_g2, rwkv_k_k, rwkv_k_a, rwkv_r_k, rwkv_ln_g, rwkv_ln_b, moe_router_w, moe_router_b, moe_w_gate, moe_w_up, moe_w_down):
    d = D_MODEL
    xs = jnp.concatenate([ctx.reshape(CTX_ROWS, d), x.reshape(BATCH * SEQ, d)], axis=0).astype(F32)
    cond = jnp.concatenate([c, c_ctx[None, :], jnp.zeros((16 - BATCH - 1, d), F32)], axis=0)
    ada = _ada_all(cond, ada_w, ada_b)
    lb_soft = jax.nn.softmax(hg_lb.astype(F32), axis=0)
    lb_all = jnp.cumsum(lb_soft, axis=0) - lb_soft[0:1]
    no_route = jnp.zeros((M_ROWS, 128), F32)

    for layer in range(DEPTH):
        j = layer // 2
        sh1, sc1, gt1, sh2, sc2, gt2 = [ada[layer, :BATCH + 1, n * d:(n + 1) * d] for n in range(6)]
        if layer % 2 == 0:
            q, f_fw, f_bw, i_in, g, aq, ak, av = _mod_matmul(
                xs, _mod_tiles(1.0 + sc1, PROJ_TM), _mod_tiles(sh1, PROJ_TM), ev_w_in[j].astype(BF16), EVEN_SPLITS, PROJ_TM)
            y1 = _hgrn2(q, f_fw, f_bw, i_in, g, lb_all[j], hg_norm_g[j])
            y2 = _window_attention(aq, ak, av, attn_sink[j])
            w_out = ev_w_out[j]
        else:
            u, p_rw = _mod_matmul(
                xs, _mod_tiles(1.0 + sc1, PROJ_TM), _mod_tiles(sh1, PROJ_TM), od_w_in[j].astype(BF16), ODD_SPLITS, PROJ_TM)
            y1 = _s5(u, s5_lam_re[j], s5_lam_im[j], s5_log_dt[j], s5_b_re[j], s5_b_im[j], s5_c_re[j], s5_c_im[j],
                     s5_d[j], s5_glu_w[j])
            y2 = _rwkv7(p_rw, rwkv_mu[j], rwkv_w0[j], rwkv_w2[j], rwkv_a0[j], rwkv_a2[j], rwkv_g2[j],
                        rwkv_k_k[j], rwkv_k_a[j], rwkv_r_k[j], rwkv_ln_g[j], rwkv_ln_b[j])
            w_out = od_w_out[j]
        xs = _out_proj_ln(y1, y2, w_out.astype(BF16), xs, _mod_tiles(gt1, PROJ_TM), ln_g[layer, 0], ln_b[layer, 0], PROJ_TM)
        scale2, shift2, gate2 = _mod_tiles(1.0 + sc2, FFN_TM), _mod_tiles(sh2, FFN_TM), _mod_tiles(gt2, FFN_TM)
        if layer % 2 == 0:
            xs = _ffn_ln(xs, scale2, shift2, gate2, no_route, ffn_w_gate[j][None].astype(BF16),
                         ffn_w_up[j][None].astype(BF16), ffn_w_down[j][None].astype(BF16),
                         ln_g[layer, 1], ln_b[layer, 1], FFN_TM, FFN_TF)
        else:
            w_pad = jnp.pad(moe_router_w[j].astype(F32), ((0, 0), (0, 128 - N_EXPERTS)))
            b_pad = jnp.pad(moe_router_b[j].astype(F32), (0, 128 - N_EXPERTS)).reshape(1, 128)
            route, sel, h = _router(xs, scale2, shift2, w_pad, b_pad, FFN_TM)
            src_token, row_w, tile_expert, n_used, pos1, pos2 = _moe_plan(route, sel)
            y_sorted = _moe_experts(h, src_token, row_w, tile_expert, n_used, moe_w_gate[j].astype(BF16),
                                    moe_w_up[j].astype(BF16), moe_w_down[j].astype(BF16))
            xs = _moe_combine_ln(y_sorted, pos1, pos2, xs, _mod_tiles(gt2, MOE_COMBINE_TM),
                                 ln_g[layer, 1], ln_b[layer, 1])
    return xs[CTX_ROWS:].reshape(BATCH, SEQ, d)
```

```python
import functools
import math

import numpy as np
import jax
import jax.numpy as jnp
from jax import lax
from jax.experimental import pallas as pl
from jax.experimental.pallas import tpu as pltpu

F32 = jnp.float32
BF16 = jnp.bfloat16
HIGHEST = lax.Precision.HIGHEST

D_MODEL = 1024
BATCH = 8
SEQ = 2048
CTX_LEN = 256
DEPTH = 4
GRID_W = 64
ROW_TILE = 256
SEQ_TILES = (CTX_LEN + SEQ) // ROW_TILE
CTX_ROWS = BATCH * CTX_LEN
M_ROWS = BATCH * (CTX_LEN + SEQ)

A_HEADS, A_DK, A_WIDTH = 4, 128, 512
B_HEADS, B_KV_HEADS, B_HEAD_DIM = 8, 2, 64
B_QW, B_KVW = 512, 128
B_BLOCK = 128
ROPE_BASE = 10000.0
C_GROUP, C_GROUPS, C_WIDTH, C_STATE = 16, 32, 512, 64
R_HEADS, R_HEAD, R_WIDTH = 8, 64, 512
R_LORA_W, R_LORA_A, R_LORA_G = 64, 64, 128
R_IN = 3 * R_WIDTH + 2 * R_LORA_W + 2 * R_LORA_A + R_LORA_G
D_FF = 2816
N_EXPERTS = 8
DEEPNORM_ALPHA = (2 * DEPTH) ** 0.25
LN_EPS = 1e-5
RWKV_GN_EPS = 64e-5

VMEM_LIMIT_BYTES = 56 * 1024 * 1024


def _params(*sem):
    return pltpu.CompilerParams(dimension_semantics=sem, vmem_limit_bytes=VMEM_LIMIT_BYTES)


def _dot(a, b, *, trans_a=False, trans_b=False, precision=None):
    dn = (((0 if trans_a else 1,), (1 if trans_b else 0,)), ((), ()))
    return lax.dot_general(a, b, dn, preferred_element_type=F32, precision=precision)


_MXU_DTYPE = BF16


def _bdot(a, b, **kw):
    return _dot(a.astype(_MXU_DTYPE), b.astype(_MXU_DTYPE), **kw)


def _sigmoid(x):
    return 1.0 / (1.0 + jnp.exp(-x))


def _silu(x):
    return x * _sigmoid(x)


def _seq_block(b, i):
    return jnp.where(i == 0, b, BATCH + b * (SEQ // ROW_TILE) + i - 1)


def _tile_order(i, reverse):
    if not reverse:
        return i
    return jnp.where(i == 0, 0, SEQ_TILES - i)


def _ada_kernel(cond_ref, w_ref, b_ref, o_ref):
    o_ref[0] = _dot(_silu(cond_ref[...]), w_ref[0], precision=HIGHEST) + b_ref[0]


def _ada_all(cond, ada_w, ada_b):
    rows = cond.shape[0]
    tn = 1536
    return pl.pallas_call(
        _ada_kernel,
        grid=(DEPTH, 6 * D_MODEL // tn),
        in_specs=[pl.BlockSpec((rows, D_MODEL), lambda l, j: (0, 0)),
                  pl.BlockSpec((1, D_MODEL, tn), lambda l, j: (l, 0, j)),
                  pl.BlockSpec((1, 1, tn), lambda l, j: (l, 0, j))],
        out_specs=pl.BlockSpec((1, rows, tn), lambda l, j: (l, 0, j)),
        out_shape=jax.ShapeDtypeStruct((DEPTH, rows, 6 * D_MODEL), F32),
        compiler_params=_params("parallel", "parallel"),
        name="ada",
    )(cond, ada_w, ada_b.reshape(DEPTH, 1, 6 * D_MODEL))


def _mod_matmul_kernel(splits, x_ref, sc_ref, sh_ref, w_ref, *o_refs):
    h = (x_ref[...] * sc_ref[0] + sh_ref[0]).astype(BF16)
    off = 0
    for o_ref, width in zip(o_refs, splits):
        o_ref[...] = _dot(h, w_ref[:, off:off + width])
        off += width


def _mod_matmul(x, scale_t, shift_t, w_bf16, splits, tm):
    m, d = x.shape
    n = w_bf16.shape[1]
    assert sum(splits) == n and m % tm == 0
    return pl.pallas_call(
        functools.partial(_mod_matmul_kernel, splits),
        grid=(m // tm,),
        in_specs=[pl.BlockSpec((tm, d), lambda i: (i, 0)),
                  pl.BlockSpec((1, 1, d), lambda i: (i, 0, 0)),
                  pl.BlockSpec((1, 1, d), lambda i: (i, 0, 0)),
                  pl.BlockSpec((d, n), lambda i: (0, 0))],
        out_specs=[pl.BlockSpec((tm, w), lambda i: (i, 0)) for w in splits],
        out_shape=[jax.ShapeDtypeStruct((m, w), F32) for w in splits],
        compiler_params=_params("parallel"),
        name="mod_matmul",
    )(x, scale_t, shift_t, w_bf16)


def _layer_norm_rows(z, g, b):
    mu = jnp.mean(z, axis=-1, keepdims=True)
    zc = z - mu
    var = jnp.mean(zc * zc, axis=-1, keepdims=True)
    return zc * lax.rsqrt(var + LN_EPS) * g + b


def _out_proj_kernel(y1_ref, y2_ref, w_ref, x_ref, gt_ref, g_ref, b_ref, o_ref):
    k1 = y1_ref.shape[1]
    proj = _bdot(y1_ref[...], w_ref[:k1, :]) + _bdot(y2_ref[...], w_ref[k1:, :])
    z = DEEPNORM_ALPHA * x_ref[...] + gt_ref[0] * proj
    o_ref[...] = _layer_norm_rows(z, g_ref[...], b_ref[...])


def _out_proj_ln(y1, y2, w_bf16, x, gate_t, ln_g, ln_b, tm):
    m, d = x.shape
    k1, k2 = y1.shape[1], y2.shape[1]
    return pl.pallas_call(
        _out_proj_kernel,
        grid=(m // tm,),
        in_specs=[pl.BlockSpec((tm, k1), lambda i: (i, 0)),
                  pl.BlockSpec((tm, k2), lambda i: (i, 0)),
                  pl.BlockSpec((k1 + k2, d), lambda i: (0, 0)),
                  pl.BlockSpec((tm, d), lambda i: (i, 0)),
                  pl.BlockSpec((1, 1, d), lambda i: (i, 0, 0)),
                  pl.BlockSpec((1, d), lambda i: (0, 0)),
                  pl.BlockSpec((1, d), lambda i: (0, 0))],
        out_specs=pl.BlockSpec((tm, d), lambda i: (i, 0)),
        out_shape=jax.ShapeDtypeStruct((m, d), F32),
        compiler_params=_params("parallel"),
        name="out_proj_ln",
    )(y1, y2, w_bf16, x, gate_t, ln_g.reshape(1, d), ln_b.reshape(1, d))


def _ffn_kernel(n_exp, x_ref, sc_ref, sh_ref, gt_ref, route_ref, wg_ref, wu_ref, wd_ref,
                g_ref, b_ref, o_ref, h_ref, acc_ref):
    e, f = pl.program_id(1), pl.program_id(2)

    @pl.when((e == 0) & (f == 0))
    def _():
        h_ref[...] = (x_ref[...] * sc_ref[0] + sh_ref[0]).astype(BF16)
        acc_ref[...] = jnp.zeros_like(acc_ref)

    h = h_ref[...]
    a = _silu(_dot(h, wg_ref[0])) * _dot(h, wu_ref[0])
    if n_exp > 1:
        lane = lax.broadcasted_iota(jnp.int32, route_ref.shape, 1)
        a = a * jnp.sum(jnp.where(lane == e, route_ref[...], 0.0), axis=1, keepdims=True)
    acc_ref[...] += _bdot(a, wd_ref[0])

    @pl.when((e == n_exp - 1) & (f == pl.num_programs(2) - 1))
    def _():
        z = DEEPNORM_ALPHA * x_ref[...] + gt_ref[0] * acc_ref[...]
        o_ref[...] = _layer_norm_rows(z, g_ref[...], b_ref[...])


def _ffn_ln(x, scale_t, shift_t, gate_t, route, wg, wu, wd, ln_g, ln_b, tm, tf):
    m, d = x.shape
    n_exp, _, ff = wg.shape
    assert ff % tf == 0 and m % tm == 0
    return pl.pallas_call(
        functools.partial(_ffn_kernel, n_exp),
        grid=(m // tm, n_exp, ff // tf),
        in_specs=[pl.BlockSpec((tm, d), lambda i, e, f: (i, 0)),
                  pl.BlockSpec((1, 1, d), lambda i, e, f: (i, 0, 0)),
                  pl.BlockSpec((1, 1, d), lambda i, e, f: (i, 0, 0)),
                  pl.BlockSpec((1, 1, d), lambda i, e, f: (i, 0, 0)),
                  pl.BlockSpec((tm, 128), lambda i, e, f: (i, 0)),
                  pl.BlockSpec((1, d, tf), lambda i, e, f: (e, 0, f)),
                  pl.BlockSpec((1, d, tf), lambda i, e, f: (e, 0, f)),
                  pl.BlockSpec((1, tf, d), lambda i, e, f: (e, f, 0)),
                  pl.BlockSpec((1, d), lambda i, e, f: (0, 0)),
                  pl.BlockSpec((1, d), lambda i, e, f: (0, 0))],
        out_specs=pl.BlockSpec((tm, d), lambda i, e, f: (i, 0)),
        out_shape=jax.ShapeDtypeStruct((m, d), F32),
        scratch_shapes=[pltpu.VMEM((tm, d), BF16), pltpu.VMEM((tm, d), F32)],
        compiler_params=_params("parallel", "arbitrary", "arbitrary"),
        name="ffn_ln",
    )(x, scale_t, shift_t, gate_t, route, wg, wu, wd, ln_g.reshape(1, d), ln_b.reshape(1, d))


def _router_kernel(x_ref, sc_ref, sh_ref, w_ref, b_ref, o_ref, sel_ref, h_ref):
    h = x_ref[...] * sc_ref[0] + sh_ref[0]
    h_ref[...] = h
    logits = _dot(h, w_ref[...], precision=HIGHEST) + b_ref[...]
    lane = lax.broadcasted_iota(jnp.int32, logits.shape, 1)
    neg = jnp.float32(-jnp.inf)
    logits = jnp.where(lane < N_EXPERTS, logits, neg)
    v1 = jnp.max(logits, axis=1, keepdims=True)
    i1 = jnp.min(jnp.where(logits == v1, lane, 128), axis=1, keepdims=True)
    rest = jnp.where(lane == i1, neg, logits)
    v2 = jnp.max(rest, axis=1, keepdims=True)
    i2 = jnp.min(jnp.where(rest == v2, lane, 128), axis=1, keepdims=True)
    e2 = jnp.exp(v2 - v1)
    p1 = 1.0 / (1.0 + e2)
    p2 = e2 / (1.0 + e2)
    o_ref[...] = jnp.where(lane == i1, p1, 0.0) + jnp.where(lane == i2, p2, 0.0)
    sel_ref[...] = ((lane == i1) | (lane == i2)).astype(F32)


def _router(x, scale_t, shift_t, w_pad, b_pad, tm):
    m, d = x.shape
    lanes = pl.BlockSpec((tm, 128), lambda i: (i, 0))
    return pl.pallas_call(
        _router_kernel,
        grid=(m // tm,),
        in_specs=[pl.BlockSpec((tm, d), lambda i: (i, 0)),
                  pl.BlockSpec((1, 1, d), lambda i: (i, 0, 0)),
                  pl.BlockSpec((1, 1, d), lambda i: (i, 0, 0)),
                  pl.BlockSpec((d, 128), lambda i: (0, 0)),
                  pl.BlockSpec((1, 128), lambda i: (0, 0))],
        out_specs=[lanes, lanes, pl.BlockSpec((tm, d), lambda i: (i, 0))],
        out_shape=[jax.ShapeDtypeStruct((m, 128), F32), jax.ShapeDtypeStruct((m, 128), F32),
                   jax.ShapeDtypeStruct((m, d), F32)],
        compiler_params=_params("parallel"),
        name="router",
    )(x, scale_t, shift_t, w_pad, b_pad)


MOE_TM = 512
MOE_ROWS = 2 * M_ROWS + N_EXPERTS * MOE_TM
MOE_TILES = MOE_ROWS // MOE_TM
MOE_FF_CHUNK = 256
MOE_COMBINE_TM = 256
MOE_DISPATCH_TM = 256


def _moe_plan(route, sel):
    sel8 = sel[:, :N_EXPERTS].astype(jnp.int32)
    counts = jnp.sum(sel8, axis=0)
    rank = jnp.cumsum(sel8, axis=0) - sel8
    padded = ((counts + MOE_TM - 1) // MOE_TM) * MOE_TM
    ends = jnp.cumsum(padded)
    starts = ends - padded
    pos = jnp.where(sel8 > 0, starts[None, :] + rank, MOE_ROWS)
    pos1 = jnp.min(pos, axis=1)
    pos2 = jnp.min(jnp.where(pos == pos1[:, None], MOE_ROWS, pos), axis=1)
    route8 = route[:, :N_EXPERTS]
    w1 = jnp.sum(jnp.where(pos == pos1[:, None], route8, 0.0), axis=1, keepdims=True)
    w2 = jnp.sum(jnp.where(pos == pos2[:, None], route8, 0.0), axis=1, keepdims=True)
    tile_start = jnp.arange(MOE_TILES, dtype=jnp.int32) * MOE_TM
    tile_expert = jnp.minimum(jnp.sum((tile_start[:, None] >= ends[None, :]).astype(jnp.int32), axis=1),
                              N_EXPERTS - 1).astype(jnp.int32)
    n_used = (ends[-1] // MOE_TM).astype(jnp.int32).reshape(1)
    pad_lo, pad_hi = (starts + counts).astype(jnp.int32), ends.astype(jnp.int32)
    return pos1.astype(jnp.int32), pos2.astype(jnp.int32), w1, w2, tile_expert, n_used, pad_lo, pad_hi


def _moe_dispatch_kernel(p1_ref, p2_ref, lo_ref, hi_ref, nused_ref, h_ref, hs_hbm, zero_ref, sem, zsem):
    i = pl.program_id(0)
    base = i * MOE_DISPATCH_TM

    def row_copy(r, dst_row):
        return pltpu.make_async_copy(h_ref.at[pl.ds(r, 1), :], hs_hbm.at[pl.ds(dst_row, 1), :], sem)

    def start(r, carry):
        row_copy(r, p1_ref[base + r]).start()
        row_copy(r, p2_ref[base + r]).start()
        return carry

    def wait(r, carry):
        row_copy(0, 0).wait()
        row_copy(0, 0).wait()
        return carry

    lax.fori_loop(0, MOE_DISPATCH_TM, start, 0, unroll=8)
    lax.fori_loop(0, MOE_DISPATCH_TM, wait, 0, unroll=8)

    @pl.when(i == pl.num_programs(0) - 1)
    def _():
        zero_ref[...] = jnp.zeros_like(zero_ref)

        def zero_copy(dst_row):
            return pltpu.make_async_copy(zero_ref.at[pl.ds(0, 1), :], hs_hbm.at[pl.ds(dst_row, 1), :], zsem)

        for e in range(N_EXPERTS):
            lo, hi = lo_ref[e], hi_ref[e]

            def zstart(r, carry):
                zero_copy(r).start()
                return carry

            def zwait(r, carry):
                zero_copy(0).wait()
                return carry

            lax.fori_loop(lo, hi, zstart, 0)
            lax.fori_loop(lo, hi, zwait, 0)

        def tile_copy(t):
            return pltpu.make_async_copy(zero_ref, hs_hbm.at[pl.ds(t * MOE_TM, MOE_TM), :], zsem)

        def tstart(t, carry):
            tile_copy(t).start()
            return carry

        def twait(t, carry):
            tile_copy(0).wait()
            return carry

        lax.fori_loop(nused_ref[0], MOE_TILES, tstart, 0)
        lax.fori_loop(nused_ref[0], MOE_TILES, twait, 0)


def _moe_dispatch(h, pos1, pos2, pad_lo, pad_hi, n_used):
    m, d = h.shape
    grid_spec = pltpu.PrefetchScalarGridSpec(
        num_scalar_prefetch=5,
        grid=(m // MOE_DISPATCH_TM,),
        in_specs=[pl.BlockSpec((MOE_DISPATCH_TM, d), lambda i, *_: (i, 0))],
        out_specs=pl.BlockSpec(memory_space=pl.ANY),
        scratch_shapes=[pltpu.VMEM((MOE_TM, d), F32), pltpu.SemaphoreType.DMA(()), pltpu.SemaphoreType.DMA(())],
    )
    return pl.pallas_call(
        _moe_dispatch_kernel,
        grid_spec=grid_spec,
        out_shape=jax.ShapeDtypeStruct((MOE_ROWS, d), F32),
        compiler_params=_params("arbitrary"),
        name="moe_dispatch",
    )(pos1, pos2, pad_lo, pad_hi, n_used, h)


def _moe_expert_kernel(texp_ref, nused_ref, hs_ref, wg_ref, wu_ref, wd_ref, y_ref):
    t = pl.program_id(0)

    @pl.when(t < nused_ref[0])
    def _():
        h = hs_ref[...].astype(BF16)
        acc = jnp.zeros((MOE_TM, D_MODEL), F32)
        chunks = [slice(c * MOE_FF_CHUNK, (c + 1) * MOE_FF_CHUNK) for c in range(D_FF // MOE_FF_CHUNK)]
        pending = None
        for cols in chunks:
            g, u = _dot(h, wg_ref[0, :, cols]), _dot(h, wu_ref[0, :, cols])
            if pending is not None:
                acc = acc + _dot(pending[0], wd_ref[0, pending[1], :])
            pending = ((_silu(g) * u).astype(BF16), cols)
        y_ref[...] = acc + _dot(pending[0], wd_ref[0, pending[1], :])

    @pl.when(t >= nused_ref[0])
    def _():
        y_ref[...] = jnp.zeros_like(y_ref)


def _moe_experts(hs, tile_expert, n_used, wg, wu, wd):
    d = hs.shape[1]
    ff = wg.shape[2]
    grid_spec = pltpu.PrefetchScalarGridSpec(
        num_scalar_prefetch=2,
        grid=(MOE_TILES,),
        in_specs=[pl.BlockSpec((MOE_TM, d), lambda t, te, nu: (jnp.minimum(t, nu[0] - 1), 0)),
                  pl.BlockSpec((1, d, ff), lambda t, te, nu: (te[t], 0, 0)),
                  pl.BlockSpec((1, d, ff), lambda t, te, nu: (te[t], 0, 0)),
                  pl.BlockSpec((1, ff, d), lambda t, te, nu: (te[t], 0, 0))],
        out_specs=pl.BlockSpec((MOE_TM, d), lambda t, te, nu: (t, 0)),
    )
    return pl.pallas_call(
        _moe_expert_kernel,
        grid_spec=grid_spec,
        out_shape=jax.ShapeDtypeStruct((MOE_ROWS, d), F32),
        compiler_params=_params("arbitrary"),
        name="moe_experts",
    )(tile_expert, n_used, hs, wg, wu, wd)


def _moe_combine_kernel(p1_ref, p2_ref, y_hbm, x_ref, w1_ref, w2_ref, gt_ref, g_ref, b_ref, o_ref, buf1, buf2, sem):
    base = pl.program_id(0) * MOE_COMBINE_TM

    def row_copy(buf, k, r, src_row):
        return pltpu.make_async_copy(y_hbm.at[pl.ds(src_row, 1), :], buf.at[pl.ds(r, 1), :], sem.at[k])

    def start(r, carry):
        row_copy(buf1, 0, r, p1_ref[base + r]).start()
        row_copy(buf2, 1, r, p2_ref[base + r]).start()
        return carry

    def wait(r, carry):
        row_copy(buf1, 0, 0, 0).wait()
        row_copy(buf2, 1, 0, 0).wait()
        return carry

    lax.fori_loop(0, MOE_COMBINE_TM, start, 0, unroll=8)
    lax.fori_loop(0, MOE_COMBINE_TM, wait, 0, unroll=8)
    f = w1_ref[...] * buf1[...] + w2_ref[...] * buf2[...]
    z = DEEPNORM_ALPHA * x_ref[...] + gt_ref[0] * f
    o_ref[...] = _layer_norm_rows(z, g_ref[...], b_ref[...])


def _moe_combine_ln(y_sorted, pos1, pos2, w1, w2, x, gate_t, ln_g, ln_b):
    m, d = x.shape
    tm = MOE_COMBINE_TM
    grid_spec = pltpu.PrefetchScalarGridSpec(
        num_scalar_prefetch=2,
        grid=(m // tm,),
        in_specs=[pl.BlockSpec(memory_space=pl.ANY),
                  pl.BlockSpec((tm, d), lambda i, p1, p2: (i, 0)),
                  pl.BlockSpec((tm, 1), lambda i, p1, p2: (i, 0)),
                  pl.BlockSpec((tm, 1), lambda i, p1, p2: (i, 0)),
                  pl.BlockSpec((1, 1, d), lambda i, p1, p2: (i, 0, 0)),
                  pl.BlockSpec((1, d), lambda i, p1, p2: (0, 0)),
                  pl.BlockSpec((1, d), lambda i, p1, p2: (0, 0))],
        out_specs=pl.BlockSpec((tm, d), lambda i, p1, p2: (i, 0)),
        scratch_shapes=[pltpu.VMEM((tm, d), F32), pltpu.VMEM((tm, d), F32), pltpu.SemaphoreType.DMA((2,))],
    )
    return pl.pallas_call(
        _moe_combine_kernel,
        grid_spec=grid_spec,
        out_shape=jax.ShapeDtypeStruct((m, d), F32),
        compiler_params=_params("arbitrary"),
        name="moe_combine_ln",
    )(pos1, pos2, y_sorted, x, w1, w2, gate_t, ln_g.reshape(1, d), ln_b.reshape(1, d))


HGRN_CHUNK = 16
HGRN_SUPER = 64


def _hgrn_kernel(reverse, finish, *refs):
    if finish:
        q_ref, f_ref, i_ref, lb_ref, of_ref, g_ref, ng_ref, o_ref, st_ref = refs
    else:
        q_ref, f_ref, i_ref, lb_ref, o_ref, st_ref = refs
    C, S = HGRN_CHUNK, HGRN_SUPER
    nsub = S // C
    n_super = ROW_TILE // S

    @pl.when(pl.program_id(1) == 0)
    def _():
        st_ref[...] = jnp.zeros_like(st_ref)

    row = lax.broadcasted_iota(jnp.int32, (S, S), 0)
    col = lax.broadcasted_iota(jnp.int32, (S, S), 1)
    blk_r, blk_c = jnp.zeros_like(row), jnp.zeros_like(col)
    for j in range(1, nsub):
        blk_r = blk_r + (row >= j * C).astype(jnp.int32)
        blk_c = blk_c + (col >= j * C).astype(jnp.int32)
    col_in = col - blk_c * C
    keep = (blk_r == blk_c) & ((row <= col) if reverse else (row >= col))
    tri = keep.astype(F32)
    heads = range(A_HEADS)
    sls = [slice(h * A_DK, (h + 1) * A_DK) for h in heads]
    chunk_rows = [slice(c * C, (c + 1) * C) for c in range(nsub)]
    width = A_HEADS * A_DK

    def per_chunk_row(t, offset):
        return jnp.concatenate(
            [jnp.broadcast_to(t[c * C + offset:c * C + offset + 1], (C, width)) for c in range(nsub)], axis=0)

    def superchunk(si, carry):
        sc = (n_super - 1 - si) if reverse else si
        rows = pl.ds(pl.multiple_of(sc * S, S), S)
        lb = lb_ref[...]
        q = _silu(q_ref[rows, :]) * A_DK ** -0.5
        f = lb + (1.0 - lb) * _sigmoid(f_ref[rows, :])
        k = 1.0 - f
        v = i_ref[rows, :]
        b = _dot(tri, jnp.log(f), precision=HIGHEST)
        last = 0 if reverse else C - 1
        b_tot = per_chunk_row(b, last)
        qe = q * jnp.exp(b)
        kt = k * jnp.exp(b_tot - b)
        atts = [jnp.zeros((S, S), F32) for _ in heads]
        for s in range(C):
            tmp = q * jnp.exp(jnp.minimum(b - per_chunk_row(b, s), 0.0)) * per_chunk_row(k, s)
            atts = [jnp.where(col_in == s, jnp.sum(tmp[:, sl], axis=1, keepdims=True), att)
                    for sl, att in zip(sls, atts)]
        intra = [_bdot(jnp.where(keep, att, 0.0), v[:, sl]) for sl, att in zip(sls, atts)]
        upds = [[_bdot(v[cr, sl], kt[cr, sl], trans_a=True) for cr in chunk_rows] for sl in sls]
        sts = [st_ref[h] for h in heads]
        inter = [[None] * nsub for _ in heads]
        for c in (range(nsub - 1, -1, -1) if reverse else range(nsub)):
            cr = chunk_rows[c]
            decay = jnp.exp(b[c * C + last:c * C + last + 1])
            for h in heads:
                inter[h][c] = _bdot(qe[cr, sls[h]], sts[h], trans_b=True)
                sts[h] = sts[h] * decay[:, sls[h]] + upds[h][c]
        outs = []
        for h in heads:
            st_ref[h] = sts[h]
            o = intra[h] + jnp.concatenate(inter[h], axis=0)
            if finish:
                o = o + of_ref[rows, sls[h]]
                o = o * lax.rsqrt(jnp.mean(o * o, axis=1, keepdims=True) + 1e-6)
                o = o * ng_ref[:, sls[h]] * _silu(g_ref[rows, sls[h]])
            outs.append(o)
        for h in heads:
            o_ref[rows, sls[h]] = outs[h]
        return carry

    lax.fori_loop(0, n_super, superchunk, 0)


def _hgrn_pass(reverse, q, f, i, lb, finish_args=None):
    m, w = q.shape
    seq_spec = pl.BlockSpec((ROW_TILE, w), lambda b, t: (_seq_block(b, _tile_order(t, reverse)), 0))
    vec_spec = pl.BlockSpec((1, w), lambda b, t: (0, 0))
    ins = [q, f, i, lb]
    specs = [seq_spec, seq_spec, seq_spec, vec_spec]
    if finish_args is not None:
        of, g, ng = finish_args
        ins += [of, g, ng]
        specs += [seq_spec, seq_spec, vec_spec]
    return pl.pallas_call(
        functools.partial(_hgrn_kernel, reverse, finish_args is not None),
        grid=(BATCH, SEQ_TILES),
        in_specs=specs,
        out_specs=seq_spec,
        out_shape=jax.ShapeDtypeStruct((m, w), F32),
        scratch_shapes=[pltpu.VMEM((A_HEADS, A_DK, A_DK), F32)],
        compiler_params=_params("parallel", "arbitrary"),
        name="hgrn_bwd" if reverse else "hgrn_fwd",
    )(*ins)


def _rope_kernel(q_ref, k_ref, cos_ref, sa_ref, sb_ref, qo_ref, ko_ref):
    cos, sa, sb = cos_ref[...], sa_ref[...], sb_ref[...]

    def rot(x):
        return x * cos + pltpu.roll(x, 96, 1) * sa + pltpu.roll(x, 32, 1) * sb

    for j in range(B_QW // 128):
        qo_ref[:, j * 128:(j + 1) * 128] = rot(q_ref[:, j * 128:(j + 1) * 128])
    ko_ref[...] = rot(k_ref[...])


def _rope(q, k, cos_t, sa_t, sb_t):
    m = q.shape[0]
    seq = lambda w: pl.BlockSpec((ROW_TILE, w), lambda b, t: (_seq_block(b, t), 0))
    tab = pl.BlockSpec((ROW_TILE, 128), lambda b, t: (t, 0))
    return pl.pallas_call(
        _rope_kernel,
        grid=(BATCH, SEQ_TILES),
        in_specs=[seq(B_QW), seq(B_KVW), tab, tab, tab],
        out_specs=[seq(B_QW), seq(B_KVW)],
        out_shape=[jax.ShapeDtypeStruct((m, B_QW), F32), jax.ShapeDtypeStruct((m, B_KVW), F32)],
        compiler_params=_params("parallel", "parallel"),
        name="rope",
    )(q, k, cos_t, sa_t, sb_t)


def _rope_tables():
    rows = SEQ // GRID_W
    row = jnp.repeat(jnp.arange(rows, dtype=F32), GRID_W)
    colp = jnp.tile(jnp.arange(GRID_W, dtype=F32), rows)
    n_freq = B_HEAD_DIM // 4
    inv = ROPE_BASE ** (-jnp.arange(n_freq, dtype=F32) / n_freq)
    ang = jnp.concatenate([row[:, None] * inv, colp[:, None] * inv], axis=-1)
    cos, sin = jnp.cos(ang), jnp.sin(ang)
    zero = jnp.zeros_like(sin)
    cos_l = jnp.tile(cos, (1, 4))
    sa_l = jnp.tile(jnp.concatenate([-sin, zero], axis=-1), (1, 2))
    sb_l = jnp.tile(jnp.concatenate([zero, sin], axis=-1), (1, 2))
    pad = lambda t, v: jnp.concatenate([jnp.full((CTX_LEN, 128), v, F32), t], axis=0)
    return pad(cos_l, 1.0), pad(sa_l, 0.0), pad(sb_l, 0.0)


def _attend(q_ref, o_ref, sink_ref, k, v, mask):
    tq = q_ref.shape[0]
    left = lax.broadcasted_iota(jnp.int32, (tq, 128), 1) < B_HEAD_DIM
    lk = lax.broadcasted_iota(jnp.int32, k.shape, 1) < B_HEAD_DIM
    neg = jnp.float32(-jnp.inf)
    scale = B_HEAD_DIM ** -0.5
    kr, vr = pltpu.roll(k, 64, 1), pltpu.roll(v, 64, 1)
    kds = [jnp.where(lk, k, kr).astype(_MXU_DTYPE), jnp.where(lk, kr, k).astype(_MXU_DTYPE)]
    vds = [jnp.where(lk, v, vr), jnp.where(lk, vr, v)]
    vsel = [[jnp.where(lk if half == 0 else jnp.logical_not(lk), vd, 0.0).astype(_MXU_DTYPE) for half in range(2)]
            for vd in vds]
    pairs = B_HEADS // 2
    heads = [(p, half) for p in range(pairs) for half in range(2)]
    group = lambda p: p // (pairs // B_KV_HEADS)
    qms = [jnp.where(left if half == 0 else jnp.logical_not(left), q_ref[:, p * 128:(p + 1) * 128] * scale, 0.0)
           .astype(_MXU_DTYPE) for p, half in heads]
    scores = [_dot(qm, kds[group(p)], trans_b=True) for (p, half), qm in zip(heads, qms)]
    if mask is not None:
        scores = [jnp.where(mask, s, neg) for s in scores]
    sinks = [sink_ref[2 * p + half:2 * p + half + 1, 0:1] for p, half in heads]
    mxs = [jnp.maximum(sink, jnp.max(s, axis=1, keepdims=True)) for s, sink in zip(scores, sinks)]
    es = [jnp.exp(s - mx) for s, mx in zip(scores, mxs)]
    denoms = [jnp.exp(sink - mx) + jnp.sum(e, axis=1, keepdims=True) for e, mx, sink in zip(es, mxs, sinks)]
    accs = [_dot(e.astype(_MXU_DTYPE), vsel[group(p)][half]) for (p, half), e in zip(heads, es)]
    for p in range(pairs):
        o_ref[:, p * 128:(p + 1) * 128] = accs[2 * p] / denoms[2 * p] + accs[2 * p + 1] / denoms[2 * p + 1]


ATTN_LATENT_BLOCKS = SEQ // B_BLOCK
ATTN_CTX_BLOCKS = CTX_LEN // B_BLOCK


def _attn_kernel(q_ref, kp_ref, kc_ref, kn_ref, vp_ref, vc_ref, vn_ref, kx_ref, vx_ref, sink_ref, o_ref):
    n = pl.program_id(1)
    nb = ATTN_LATENT_BLOCKS

    @pl.when(n < nb)
    def _():
        row = lax.broadcasted_iota(jnp.int32, (B_BLOCK, B_BLOCK), 0)
        col = lax.broadcasted_iota(jnp.int32, (B_BLOCK, B_BLOCK), 1)
        mask = jnp.concatenate([(col >= row) & (n > 0), jnp.ones((B_BLOCK, B_BLOCK), jnp.bool_),
                                (col <= row) & (n < nb - 1), jnp.ones((B_BLOCK, CTX_LEN), jnp.bool_)], axis=1)
        k = jnp.concatenate([kp_ref[...], kc_ref[...], kn_ref[...], kx_ref[...]], axis=0)
        v = jnp.concatenate([vp_ref[...], vc_ref[...], vn_ref[...], vx_ref[...]], axis=0)
        _attend(q_ref, o_ref, sink_ref, k, v, mask)

    @pl.when(n >= nb)
    def _():
        _attend(q_ref, o_ref, sink_ref, kx_ref[...], vx_ref[...], None)


def _attention(q, k, v, sink_rows):
    m = q.shape[0]
    nb, nc = ATTN_LATENT_BLOCKS, ATTN_CTX_BLOCKS
    base = CTX_ROWS // B_BLOCK

    def q_block(b, n):
        return jnp.where(n < nb, base + b * nb + n, b * nc + n - nb)

    qspec = pl.BlockSpec((B_BLOCK, B_QW), lambda b, n: (q_block(b, n), 0))

    def kv(shift):
        return pl.BlockSpec((B_BLOCK, B_KVW), lambda b, n: (base + b * nb + jnp.clip(n + shift, 0, nb - 1), 0))

    ctx_kv = pl.BlockSpec((CTX_LEN, B_KVW), lambda b, n: (b, 0))
    sink_spec = pl.BlockSpec((B_HEADS, 128), lambda b, n: (0, 0))
    return pl.pallas_call(
        _attn_kernel,
        grid=(BATCH, nb + nc),
        in_specs=[qspec, kv(-1), kv(0), kv(1), kv(-1), kv(0), kv(1), ctx_kv, ctx_kv, sink_spec],
        out_specs=qspec,
        out_shape=jax.ShapeDtypeStruct((m, B_QW), F32),
        compiler_params=_params("parallel", "parallel"),
        name="attention",
    )(q, k, k, k, v, v, v, k, v, sink_rows)


def _window_attention(q, k, v, sink):
    cos_t, sa_t, sb_t = _rope_tables()
    qr, kr = _rope(q, k, cos_t, sa_t, sb_t)
    sink_rows = jnp.broadcast_to(sink.astype(F32)[:, None], (B_HEADS, 128))
    return _attention(qr, kr, v, sink_rows)


def _hgrn2(q, f_fw, f_bw, i, g, lb, norm_g):
    lb = lb.reshape(1, A_WIDTH)
    o_fw = _hgrn_pass(False, q, f_fw, i, lb)
    return _hgrn_pass(True, q, f_bw, i, lb, (o_fw, g, norm_g.reshape(1, A_WIDTH)))


S5_STEPS = 64
S5_ROWS = S5_STEPS * BATCH
S5_TILES = (CTX_LEN + SEQ) // S5_STEPS
S5_CTX_TILES = CTX_LEN // S5_STEPS
S5_BLOCKS = 4
S5_BLOCK_IN = C_WIDTH // S5_BLOCKS
S5_BLOCK_STATE = C_GROUPS * C_STATE // S5_BLOCKS


def _s5_kernel(reverse, finish, *refs):
    if finish:
        u_ref, a_ref, wb_ref, wc_ref, yf_ref, glu_ref, y_ref, x_ref, st_ref = refs
    else:
        u_ref, a_ref, wb_ref, wc_ref, d_ref, y_ref, x_ref, st_ref = refs
    ns = S5_BLOCK_STATE

    @pl.when(pl.program_id(0) == 0)
    def _():
        st_ref[...] = jnp.zeros_like(st_ref)

    for k in range(S5_BLOCKS):
        x_ref[:, 2 * ns * k:2 * ns * (k + 1)] = _bdot(u_ref[:, S5_BLOCK_IN * k:S5_BLOCK_IN * (k + 1)], wb_ref[k])

    def step(tt, carry):
        t = (S5_STEPS - 1 - tt) if reverse else tt
        rows = pl.ds(pl.multiple_of(t * BATCH, BATCH), BATCH)
        for k in range(S5_BLOCKS):
            re = slice(2 * ns * k, 2 * ns * k + ns)
            im = slice(2 * ns * k + ns, 2 * ns * (k + 1))
            ar, ai = a_ref[:, re], a_ref[:, im]
            sr, si = st_ref[:, re], st_ref[:, im]
            nr = ar * sr - ai * si + x_ref[rows, re]
            ni = ar * si + ai * sr + x_ref[rows, im]
            st_ref[:, re] = nr
            st_ref[:, im] = ni
            x_ref[rows, re] = nr
            x_ref[rows, im] = ni
        return carry

    lax.fori_loop(0, S5_STEPS, step, 0)

    for k in range(S5_BLOCKS):
        cols = slice(S5_BLOCK_IN * k, S5_BLOCK_IN * (k + 1))
        y = _bdot(x_ref[:, 2 * ns * k:2 * ns * (k + 1)], wc_ref[k])
        if finish:
            y_ref[:, cols] = y + yf_ref[:, cols]
        else:
            y_ref[:, cols] = y + d_ref[:, cols] * u_ref[:, cols]
    if finish:
        y = jax.nn.gelu(y_ref[...])
        y_ref[...] = y * _sigmoid(_bdot(y, glu_ref[...]))


def _s5_tile_order(i, reverse):
    if not reverse:
        return i
    return jnp.where(i < S5_CTX_TILES, S5_CTX_TILES - 1 - i, S5_TILES + S5_CTX_TILES - 1 - i)


def _s5_pass(reverse, u_tm, acoef, wb, wc, extra):
    m, w = u_tm.shape
    nstate = 2 * C_GROUPS * C_STATE
    row_spec = pl.BlockSpec((S5_ROWS, w), lambda i: (_s5_tile_order(i, reverse), 0))
    full = lambda a: pl.BlockSpec(a.shape, lambda i: (0,) * a.ndim)
    finish = reverse
    if finish:
        yf, glu_w = extra
        ins, specs = [u_tm, acoef, wb, wc, yf, glu_w], [row_spec, full(acoef), full(wb), full(wc), row_spec, full(glu_w)]
    else:
        (dskip,) = extra
        ins, specs = [u_tm, acoef, wb, wc, dskip], [row_spec, full(acoef), full(wb), full(wc), full(dskip)]
    return pl.pallas_call(
        functools.partial(_s5_kernel, reverse, finish),
        grid=(S5_TILES,),
        in_specs=specs,
        out_specs=row_spec,
        out_shape=jax.ShapeDtypeStruct((m, w), F32),
        scratch_shapes=[pltpu.VMEM((S5_ROWS, nstate), F32), pltpu.VMEM((BATCH, nstate), F32)],
        compiler_params=_params("arbitrary"),
        name="s5_bwd" if reverse else "s5_fwd",
    )(*ins)


def _s5_discretize(lam_re, lam_im, log_dt, b_re, b_im):
    lam_re = jnp.minimum(lam_re.astype(F32), -1e-4)
    lam_im = lam_im.astype(F32)
    dt = jnp.exp(log_dt.astype(F32))[:, None]
    mag = jnp.exp(lam_re * dt)
    ab_re, ab_im = mag * jnp.cos(lam_im * dt), mag * jnp.sin(lam_im * dt)
    den = lam_re ** 2 + lam_im ** 2
    nr = ab_re - 1.0
    co_re = (nr * lam_re + ab_im * lam_im) / den
    co_im = (ab_im * lam_re - nr * lam_im) / den
    bb_re = co_re[..., None] * b_re - co_im[..., None] * b_im
    bb_im = co_re[..., None] * b_im + co_im[..., None] * b_re
    return ab_re, ab_im, bb_re, bb_im


def _s5_tables(lam_re, lam_im, log_dt, b_re, b_im, c_re, c_im):
    eye = jnp.eye(C_GROUPS // S5_BLOCKS, dtype=F32)
    gb = C_GROUPS // S5_BLOCKS

    def in_map(bb):
        return jnp.einsum('kgph,gG->kghGp', bb.reshape(S5_BLOCKS, gb, C_STATE, C_GROUP), eye).reshape(
            S5_BLOCKS, S5_BLOCK_IN, S5_BLOCK_STATE)

    def out_map(cc):
        return jnp.einsum('kghp,gG->kgpGh', cc.reshape(S5_BLOCKS, gb, C_GROUP, C_STATE), eye).reshape(
            S5_BLOCKS, S5_BLOCK_STATE, S5_BLOCK_IN)

    wc = jnp.concatenate([out_map(c_re.astype(F32)), -out_map(c_im.astype(F32))], axis=1).astype(_MXU_DTYPE)
    tables = []
    for d in range(2):
        ab_re, ab_im, bb_re, bb_im = _s5_discretize(lam_re[d], lam_im[d], log_dt[d], b_re.astype(F32), b_im.astype(F32))
        a = jnp.concatenate([ab_re.reshape(S5_BLOCKS, S5_BLOCK_STATE), ab_im.reshape(S5_BLOCKS, S5_BLOCK_STATE)], axis=1)
        acoef = jnp.broadcast_to(a.reshape(1, -1), (BATCH, 2 * C_GROUPS * C_STATE))
        wb = jnp.concatenate([in_map(bb_re), in_map(bb_im)], axis=2).astype(_MXU_DTYPE)
        tables.append((acoef, wb))
    return tables, wc


def _to_time_major(y):
    w = y.shape[1]
    c = y[:CTX_ROWS].reshape(BATCH, CTX_LEN, w).transpose(1, 0, 2).reshape(CTX_ROWS, w)
    l = y[CTX_ROWS:].reshape(BATCH, SEQ, w).transpose(1, 0, 2).reshape(BATCH * SEQ, w)
    return jnp.concatenate([c, l], axis=0)


def _from_time_major(y):
    w = y.shape[1]
    c = y[:CTX_ROWS].reshape(CTX_LEN, BATCH, w).transpose(1, 0, 2).reshape(CTX_ROWS, w)
    l = y[CTX_ROWS:].reshape(SEQ, BATCH, w).transpose(1, 0, 2).reshape(BATCH * SEQ, w)
    return jnp.concatenate([c, l], axis=0)


def _s5(u, lam_re, lam_im, log_dt, b_re, b_im, c_re, c_im, d_skip, glu_w):
    (fw, bw), wc = _s5_tables(lam_re, lam_im, log_dt, b_re, b_im, c_re, c_im)
    u_tm = _to_time_major(u)
    y_fw = _s5_pass(False, u_tm, fw[0], fw[1], wc, (d_skip.astype(F32).reshape(1, C_WIDTH),))
    y = _s5_pass(True, u_tm, bw[0], bw[1], wc, (y_fw, glu_w.astype(_MXU_DTYPE)))
    return _from_time_major(y)


RW_LORA_OFF = 3 * R_WIDTH


def _softplus(z):
    return jnp.maximum(z, 0.0) + jnp.log1p(jnp.exp(-jnp.abs(z)))


def _rwkv_prep_kernel(p_ref, hp_ref, hn_ref, mu_ref, w0_ref, w2_ref, a0_ref, a2_ref, g2_ref, kk_ref, ka_ref, rk_ref,
                      ones_ref, r_o, v_o, g_o, bonus_o, kkn_o, lwf_o, kdf_o, bf_o, lwb_o, kdb_o, bb_o):
    i = pl.program_id(1)
    x = p_ref[...]
    rows = x.shape[0]
    rowi = lax.broadcasted_iota(jnp.int32, (rows, 1), 0)
    prev_row = jnp.where(i >= 2, hp_ref[7:8, :], 0.0)
    next_row = jnp.where((i >= 1) & (i < SEQ_TILES - 1), hn_ref[0:1, :], 0.0)
    prev = jnp.where(rowi == 0, prev_row, pltpu.roll(x, 1, 0))
    nxt = jnp.where(rowi == rows - 1, next_row, pltpu.roll(x, rows - 1, 0))
    x = x + mu_ref[...] * (0.5 * (prev + nxt) - x)

    r = x[:, 0:R_WIDTH]
    k = x[:, R_WIDTH:2 * R_WIDTH]
    v = x[:, 2 * R_WIDTH:3 * R_WIDTH]
    wd = x[:, RW_LORA_OFF:RW_LORA_OFF + 128]
    ad = x[:, RW_LORA_OFF + 128:RW_LORA_OFF + 256]
    gd = x[:, RW_LORA_OFF + 256:RW_LORA_OFF + 384]
    ones = ones_ref[...]

    r_o[...] = r
    v_o[...] = v
    g_o[...] = _bdot(_sigmoid(gd), g2_ref[...])
    bonus_o[...] = _dot(r * k * rk_ref[...], ones, precision=HIGHEST) * v
    kk = k * kk_ref[...]
    kkn = kk / jnp.maximum(jnp.sqrt(_dot(kk * kk, ones, precision=HIGHEST)), 1e-12)
    kkn_o[...] = kkn
    tw = jnp.tanh(wd)
    for d, (lw_o, kd_o, b_o) in enumerate(((lwf_o, kdf_o, bf_o), (lwb_o, kdb_o, bb_o))):
        w = -_softplus(-(w0_ref[d:d + 1, :] + _bdot(tw, w2_ref[d]))) - 0.5
        lw_o[...] = -jnp.exp(w)
        a = _sigmoid(a0_ref[d:d + 1, :] + _bdot(ad, a2_ref[d]))
        kd_o[...] = k * (1.0 + (a - 1.0) * ka_ref[...])
        b_o[...] = kkn * a


def _rwkv_prep(p, mu, w0, w2pad, a0, a2pad, g2, k_k, k_a, r_k, ones_blk):
    m, w = p.shape
    hb = ROW_TILE // 8
    seq = lambda width: pl.BlockSpec((ROW_TILE, width), lambda b, i: (_seq_block(b, i), 0))
    halo_prev = pl.BlockSpec((8, w), lambda b, i: (jnp.maximum(_seq_block(b, i) * hb - 1, 0), 0))
    halo_next = pl.BlockSpec((8, w), lambda b, i: (jnp.minimum((_seq_block(b, i) + 1) * hb, m // 8 - 1), 0))
    full = lambda a: pl.BlockSpec(a.shape, lambda b, i: (0,) * a.ndim)
    consts = [mu, w0, w2pad, a0, a2pad, g2, k_k, k_a, r_k, ones_blk]
    return pl.pallas_call(
        _rwkv_prep_kernel,
        grid=(BATCH, SEQ_TILES),
        in_specs=[seq(w), halo_prev, halo_next] + [full(c) for c in consts],
        out_specs=[seq(R_WIDTH)] * 11,
        out_shape=[jax.ShapeDtypeStruct((m, R_WIDTH), F32)] * 11,
        compiler_params=_params("parallel", "parallel"),
        name="rwkv_prep",
    )(p, p, p, *consts)


RW_CHUNK = 64


def _rwkv_scan_kernel(reverse, finish, *refs):
    if finish:
        (r_ref, kd_ref, v_ref, lw_ref, kkn_ref, b_ref, of_ref, g_ref, bonus_ref, lng_ref, lnb_ref,
         o_ref, st_ref) = refs
    else:
        r_ref, kd_ref, v_ref, lw_ref, kkn_ref, b_ref, o_ref, st_ref = refs
    C = RW_CHUNK
    P = 2 * R_HEAD
    n_chunks = ROW_TILE // C

    @pl.when(pl.program_id(1) == 0)
    def _():
        st_ref[...] = jnp.zeros_like(st_ref)

    row = lax.broadcasted_iota(jnp.int32, (C, C), 0)
    col = lax.broadcasted_iota(jnp.int32, (C, C), 1)
    incl = (row <= col) if reverse else (row >= col)
    strict = (row < col) if reverse else (row > col)
    tri = incl.astype(F32)
    row2 = lax.broadcasted_iota(jnp.int32, (C, 2 * C), 0)
    col2 = lax.broadcasted_iota(jnp.int32, (C, 2 * C), 1)
    s2 = jnp.where(col2 < C, col2, col2 - C)
    incl2 = (row2 <= s2) if reverse else (row2 >= s2)
    strict_k = ((row2 < s2) if reverse else (row2 > s2)) & (col2 >= C)
    left = lax.broadcasted_iota(jnp.int32, (C, P), 1) < R_HEAD
    left2 = lax.broadcasted_iota(jnp.int32, (2 * C, P), 1) < R_HEAD
    blockdiag = ((lax.broadcasted_iota(jnp.int32, (P, P), 0) < R_HEAD)
                 == (lax.broadcasted_iota(jnp.int32, (P, P), 1) < R_HEAD))
    steps = int(math.log2(C))

    def chunk(ci, carry):
        c = (n_chunks - 1 - ci) if reverse else ci
        rows = pl.ds(pl.multiple_of(c * C, C), C)
        lw = lw_ref[rows, :]
        kd, bv, v_all = kd_ref[rows, :], b_ref[rows, :], v_ref[rows, :]
        gi = _dot(tri, lw, precision=HIGHEST)
        g_tot = gi[0:1] if reverse else gi[C - 1:C]
        inv = jnp.exp(-gi)
        a_t = -kkn_ref[rows, :] * jnp.exp(gi - lw)
        r_t = r_ref[rows, :] * jnp.exp(gi)
        b_t, k_t = bv * inv, kd * inv
        tail = jnp.exp(g_tot - gi)
        b_tail, k_tail = bv * tail, kd * tail
        decay = jnp.exp(g_tot)
        if finish:
            of_all, g_all, bonus_all = of_ref[rows, :], g_ref[rows, :], bonus_ref[rows, :]
        pairs = range(R_HEADS // 2)
        heads = [(p, half) for p in pairs for half in range(2)]
        sls = [slice(p * P, (p + 1) * P) for p in pairs]
        sts = [st_ref[p] for p in pairs]
        vps = [v_all[:, sl] for sl in sls]
        rhss = [jnp.concatenate([b_t[:, sl], k_t[:, sl], st], axis=0).astype(_MXU_DTYPE) for sl, st in zip(sls, sts)]
        ars = [jnp.concatenate([a_t[:, sl], r_t[:, sl]], axis=0) for sl in sls]
        zvs = [jnp.concatenate([jnp.zeros((C, P), F32), vp], axis=0).astype(_MXU_DTYPE) for vp in vps]
        prods = [_dot(jnp.where(left2 if half == 0 else jnp.logical_not(left2), ars[p], 0.0).astype(_MXU_DTYPE),
                      rhss[p], trans_b=True) for p, half in heads]
        ahs = [prod[:, 2 * C:] for prod in prods]
        us = [ah[:C] + _dot(jnp.where(strict_k, prod[:C, :2 * C], 0.0).astype(_MXU_DTYPE), zvs[p])
              for (p, half), prod, ah in zip(heads, prods, ahs)]
        npows = [jnp.where(strict, prod[:C, :C], 0.0) for prod in prods]
        for it in range(steps):
            us = [u + _bdot(npow, u) for u, npow in zip(us, npows)]
            if it + 1 < steps:
                npows = [_bdot(npow, npow) for npow in npows]
        os_ = [ah[C:] + _bdot(jnp.where(incl2, prod[C:, :2 * C], 0.0), jnp.concatenate([u, vps[p]], axis=0))
               for (p, half), prod, ah, u in zip(heads, prods, ahs, us)]
        outs = []
        for p in pairs:
            sl, st, vp = sls[p], sts[p], vps[p]
            u_pair = jnp.where(left, us[2 * p], us[2 * p + 1])
            o = jnp.where(left, os_[2 * p], os_[2 * p + 1])
            upd = _bdot(jnp.concatenate([u_pair, vp], axis=0),
                        jnp.concatenate([b_tail[:, sl], k_tail[:, sl]], axis=0), trans_a=True)
            st_ref[p] = jnp.where(blockdiag, st * decay[:, sl] + upd, 0.0)
            if finish:
                o = o + of_all[:, sl]

                def head_mean(t):
                    tot = jnp.sum(t, axis=1, keepdims=True)
                    lsum = jnp.sum(jnp.where(left, t, 0.0), axis=1, keepdims=True)
                    return jnp.where(left, lsum, tot - lsum) * (1.0 / R_HEAD)

                oc = o - head_mean(o)
                o = oc * lax.rsqrt(head_mean(oc * oc) + RWKV_GN_EPS) * lng_ref[:, sl] + lnb_ref[:, sl]
                o = (o + bonus_all[:, sl]) * g_all[:, sl]
            outs.append(o)
        for p, o in enumerate(outs):
            o_ref[rows, p * P:(p + 1) * P] = o
        return carry

    lax.fori_loop(0, n_chunks, chunk, 0)


def _rwkv_scan(reverse, r, kd, v, lw, kkn, bvec, finish_args=None):
    m, w = r.shape
    seq_spec = pl.BlockSpec((ROW_TILE, w), lambda b, t: (_seq_block(b, _tile_order(t, reverse)), 0))
    vec_spec = pl.BlockSpec((1, w), lambda b, t: (0, 0))
    ins = [r, kd, v, lw, kkn, bvec]
    specs = [seq_spec] * 6
    if finish_args is not None:
        ins += list(finish_args)
        specs += [seq_spec, seq_spec, seq_spec, vec_spec, vec_spec]
    return pl.pallas_call(
        functools.partial(_rwkv_scan_kernel, reverse, finish_args is not None),
        grid=(BATCH, SEQ_TILES),
        in_specs=specs,
        out_specs=seq_spec,
        out_shape=jax.ShapeDtypeStruct((m, w), F32),
        scratch_shapes=[pltpu.VMEM((R_HEADS // 2, 2 * R_HEAD, 2 * R_HEAD), F32)],
        compiler_params=_params("parallel", "arbitrary"),
        name="rwkv_bwd" if reverse else "rwkv_fwd",
    )(*ins)


def _rwkv7(p, mu, w0, w2, a0, a2, g2, k_k, k_a, r_k, lnx_g, lnx_b):
    zeros_w = jnp.zeros((R_LORA_W, R_WIDTH), F32)
    w2pad = jnp.stack([jnp.concatenate([w2[0], zeros_w], 0), jnp.concatenate([zeros_w, w2[1]], 0)]).astype(_MXU_DTYPE)
    a2pad = jnp.stack([jnp.concatenate([a2[0], zeros_w], 0), jnp.concatenate([zeros_w, a2[1]], 0)]).astype(_MXU_DTYPE)
    head = np.arange(R_WIDTH) // R_HEAD
    ones_blk = jnp.asarray(head[:, None] == head[None, :], F32)
    row = lambda t: t.astype(F32).reshape(1, -1)
    r, v, g, bonus, kkn, lwf, kdf, bf, lwb, kdb, bb = _rwkv_prep(
        p, row(mu), w0.astype(F32), w2pad, a0.astype(F32), a2pad, g2.astype(_MXU_DTYPE),
        row(k_k), row(k_a), row(r_k), ones_blk)
    o_fw = _rwkv_scan(False, r, kdf, v, lwf, kkn, bf)
    return _rwkv_scan(True, r, kdb, v, lwb, kkn, bb, (o_fw, g, bonus, row(lnx_g), row(lnx_b)))


PROJ_TM = 256
FFN_TM = 512
FFN_TF = 256
EVEN_SPLITS = (A_WIDTH,) * 5 + (B_QW, B_KVW, B_KVW)
ODD_SPLITS = (C_WIDTH, R_IN)


def _mod_tiles(vec9, tm):
    idx = np.concatenate([np.full(CTX_ROWS // tm, BATCH), np.repeat(np.arange(BATCH), SEQ // tm)])
    return vec9[idx][:, None, :]


def kernel(x, c, ctx, c_ctx, ada_w, ada_b, ln_g, ln_b, ev_w_in, ev_w_out, hg_lb, hg_norm_g, attn_sink, ffn_w_gate, ffn_w_up, ffn_w_down, od_w_in, od_w_out, s5_lam_re, s5_lam_im, s5_log_dt, s5_b_re, s5_b_im, s5_c_re, s5_c_im, s5_d, s5_glu_w, rwkv_mu, rwkv_w0, rwkv_w2, rwkv_a0, rwkv_a2, rwkv_g2, rwkv_k_k, rwkv_k_a, rwkv_r_k, rwkv_ln_g, rwkv_ln_b, moe_router_w, moe_router_b, moe_w_gate, moe_w_up, moe_w_down):
    d = D_MODEL
    xs = jnp.concatenate([ctx.reshape(CTX_ROWS, d), x.reshape(BATCH * SEQ, d)], axis=0).astype(F32)
    cond = jnp.concatenate([c, c_ctx[None, :], jnp.zeros((16 - BATCH - 1, d), F32)], axis=0)
    ada = _ada_all(cond, ada_w, ada_b)
    lb_soft = jax.nn.softmax(hg_lb.astype(F32), axis=0)
    lb_all = jnp.cumsum(lb_soft, axis=0) - lb_soft[0:1]
    no_route = jnp.zeros((M_ROWS, 128), F32)

    for layer in range(DEPTH):
        j = layer // 2
        sh1, sc1, gt1, sh2, sc2, gt2 = [ada[layer, :BATCH + 1, n * d:(n + 1) * d] for n in range(6)]
        if layer % 2 == 0:
            q, f_fw, f_bw, i_in, g, aq, ak, av = _mod_matmul(
                xs, _mod_tiles(1.0 + sc1, PROJ_TM), _mod_tiles(sh1, PROJ_TM), ev_w_in[j].astype(BF16), EVEN_SPLITS, PROJ_TM)
            y1 = _hgrn2(q, f_fw, f_bw, i_in, g, lb_all[j], hg_norm_g[j])
            y2 = _window_attention(aq, ak, av, attn_sink[j])
            w_out = ev_w_out[j]
        else:
            u, p_rw = _mod_matmul(
                xs, _mod_tiles(1.0 + sc1, PROJ_TM), _mod_tiles(sh1, PROJ_TM), od_w_in[j].astype(BF16), ODD_SPLITS, PROJ_TM)
            y1 = _s5(u, s5_lam_re[j], s5_lam_im[j], s5_log_dt[j], s5_b_re[j], s5_b_im[j], s5_c_re[j], s5_c_im[j],
                     s5_d[j], s5_glu_w[j])
            y2 = _rwkv7(p_rw, rwkv_mu[j], rwkv_w0[j], rwkv_w2[j], rwkv_a0[j], rwkv_a2[j], rwkv_g2[j],
                        rwkv_k_k[j], rwkv_k_a[j], rwkv_r_k[j], rwkv_ln_g[j], rwkv_ln_b[j])
            w_out = od_w_out[j]
        xs = _out_proj_ln(y1, y2, w_out.astype(BF16), xs, _mod_tiles(gt1, PROJ_TM), ln_g[layer, 0], ln_b[layer, 0], PROJ_TM)
        scale2, shift2, gate2 = _mod_tiles(1.0 + sc2, FFN_TM), _mod_tiles(sh2, FFN_TM), _mod_tiles(gt2, FFN_TM)
        if layer % 2 == 0:
            xs = _ffn_ln(xs, scale2, shift2, gate2, no_route, ffn_w_gate[j][None].astype(BF16),
                         ffn_w_up[j][None].astype(BF16), ffn_w_down[j][None].astype(BF16),
                         ln_g[layer, 1], ln_b[layer, 1], FFN_TM, FFN_TF)
        else:
            w_pad = jnp.pad(moe_router_w[j].astype(F32), ((0, 0), (0, 128 - N_EXPERTS)))
            b_pad = jnp.pad(moe_router_b[j].astype(F32), (0, 128 - N_EXPERTS)).reshape(1, 128)
            route, sel, h = _router(xs, scale2, shift2, w_pad, b_pad, FFN_TM)
            pos1, pos2, w1, w2, tile_expert, n_used, pad_lo, pad_hi = _moe_plan(route, sel)
            hs = _moe_dispatch(h, pos1, pos2, pad_lo, pad_hi, n_used)
            y_sorted = _moe_experts(hs, tile_expert, n_used, moe_w_gate[j].astype(BF16),
                                    moe_w_up[j].astype(BF16), moe_w_down[j].astype(BF16))
            xs = _moe_combine_ln(y_sorted, pos1, pos2, w1, w2, xs, _mod_tiles(gt2, MOE_COMBINE_TM),
                                 ln_g[layer, 1], ln_b[layer, 1])
    return xs[CTX_ROWS:].reshape(BATCH, SEQ, d)
```

```python
import functools
import math

import numpy as np
import jax
import jax.numpy as jnp
from jax import lax
from jax.experimental import pallas as pl
from jax.experimental.pallas import tpu as pltpu

F32 = jnp.float32
BF16 = jnp.bfloat16
HIGHEST = lax.Precision.HIGHEST

D_MODEL = 1024
BATCH = 8
SEQ = 2048
CTX_LEN = 256
DEPTH = 4
GRID_W = 64
ROW_TILE = 256
SEQ_TILES = (CTX_LEN + SEQ) // ROW_TILE
CTX_ROWS = BATCH * CTX_LEN
M_ROWS = BATCH * (CTX_LEN + SEQ)

A_HEADS, A_DK, A_WIDTH = 4, 128, 512
B_HEADS, B_KV_HEADS, B_HEAD_DIM = 8, 2, 64
B_QW, B_KVW = 512, 128
B_BLOCK = 128
ROPE_BASE = 10000.0
C_GROUP, C_GROUPS, C_WIDTH, C_STATE = 16, 32, 512, 64
R_HEADS, R_HEAD, R_WIDTH = 8, 64, 512
R_LORA_W, R_LORA_A, R_LORA_G = 64, 64, 128
R_IN = 3 * R_WIDTH + 2 * R_LORA_W + 2 * R_LORA_A + R_LORA_G
D_FF = 2816
N_EXPERTS = 8
DEEPNORM_ALPHA = (2 * DEPTH) ** 0.25
LN_EPS = 1e-5
RWKV_GN_EPS = 64e-5

VMEM_LIMIT_BYTES = 56 * 1024 * 1024


def _params(*sem):
    return pltpu.CompilerParams(dimension_semantics=sem, vmem_limit_bytes=VMEM_LIMIT_BYTES)


def _dot(a, b, *, trans_a=False, trans_b=False, precision=None):
    dn = (((0 if trans_a else 1,), (1 if trans_b else 0,)), ((), ()))
    return lax.dot_general(a, b, dn, preferred_element_type=F32, precision=precision)


_MXU_DTYPE = BF16


def _bdot(a, b, **kw):
    return _dot(a.astype(_MXU_DTYPE), b.astype(_MXU_DTYPE), **kw)


def _split3(x):
    x1 = x.astype(BF16)
    r1 = x - x1.astype(F32)
    x2 = r1.astype(BF16)
    x3 = (r1 - x2.astype(F32)).astype(BF16)
    return x1, x2, x3


def _dot_sel(sel, x):
    s = sel.astype(BF16)
    x1, x2, x3 = _split3(x)
    return _dot(s, x1) + _dot(s, x2) + _dot(s, x3)


def _dot_by_sel(x, sel):
    s = sel.astype(BF16)
    x1, x2, x3 = _split3(x)
    return _dot(x1, s) + _dot(x2, s) + _dot(x3, s)


def _sigmoid(x):
    return 1.0 / (1.0 + jnp.exp(-x))


def _silu(x):
    return x * _sigmoid(x)


def _seq_block(b, i):
    return jnp.where(i == 0, b, BATCH + b * (SEQ // ROW_TILE) + i - 1)


def _tile_order(i, reverse):
    if not reverse:
        return i
    return jnp.where(i == 0, 0, SEQ_TILES - i)


def _ada_kernel(cond_ref, w_ref, b_ref, o_ref):
    o_ref[0] = _dot(_silu(cond_ref[...]), w_ref[0], precision=HIGHEST) + b_ref[0]


def _ada_all(cond, ada_w, ada_b):
    rows = cond.shape[0]
    tn = 1536
    return pl.pallas_call(
        _ada_kernel,
        grid=(DEPTH, 6 * D_MODEL // tn),
        in_specs=[pl.BlockSpec((rows, D_MODEL), lambda l, j: (0, 0)),
                  pl.BlockSpec((1, D_MODEL, tn), lambda l, j: (l, 0, j)),
                  pl.BlockSpec((1, 1, tn), lambda l, j: (l, 0, j))],
        out_specs=pl.BlockSpec((1, rows, tn), lambda l, j: (l, 0, j)),
        out_shape=jax.ShapeDtypeStruct((DEPTH, rows, 6 * D_MODEL), F32),
        compiler_params=_params("parallel", "parallel"),
        name="ada",
    )(cond, ada_w, ada_b.reshape(DEPTH, 1, 6 * D_MODEL))


def _mod_matmul_kernel(splits, x_ref, sc_ref, sh_ref, w_ref, *o_refs):
    h = (x_ref[...] * sc_ref[0] + sh_ref[0]).astype(BF16)
    off = 0
    for o_ref, width in zip(o_refs, splits):
        o_ref[...] = _dot(h, w_ref[:, off:off + width])
        off += width


def _mod_matmul(x, scale_t, shift_t, w_bf16, splits, tm):
    m, d = x.shape
    n = w_bf16.shape[1]
    assert sum(splits) == n and m % tm == 0
    return pl.pallas_call(
        functools.partial(_mod_matmul_kernel, splits),
        grid=(m // tm,),
        in_specs=[pl.BlockSpec((tm, d), lambda i: (i, 0)),
                  pl.BlockSpec((1, 1, d), lambda i: (i, 0, 0)),
                  pl.BlockSpec((1, 1, d), lambda i: (i, 0, 0)),
                  pl.BlockSpec((d, n), lambda i: (0, 0))],
        out_specs=[pl.BlockSpec((tm, w), lambda i: (i, 0)) for w in splits],
        out_shape=[jax.ShapeDtypeStruct((m, w), F32) for w in splits],
        compiler_params=_params("parallel"),
        name="mod_matmul",
    )(x, scale_t, shift_t, w_bf16)


def _layer_norm_rows(z, g, b):
    mu = jnp.mean(z, axis=-1, keepdims=True)
    zc = z - mu
    var = jnp.mean(zc * zc, axis=-1, keepdims=True)
    return zc * lax.rsqrt(var + LN_EPS) * g + b


def _out_proj_kernel(y1_ref, y2_ref, w_ref, x_ref, gt_ref, g_ref, b_ref, o_ref):
    k1 = y1_ref.shape[1]
    proj = _bdot(y1_ref[...], w_ref[:k1, :]) + _bdot(y2_ref[...], w_ref[k1:, :])
    z = DEEPNORM_ALPHA * x_ref[...] + gt_ref[0] * proj
    o_ref[...] = _layer_norm_rows(z, g_ref[...], b_ref[...])


def _out_proj_ln(y1, y2, w_bf16, x, gate_t, ln_g, ln_b, tm, skip_rows=0):
    m, d = x.shape
    k1, k2 = y1.shape[1], y2.shape[1]
    off = skip_rows // tm
    return pl.pallas_call(
        _out_proj_kernel,
        grid=((m - skip_rows) // tm,),
        in_specs=[pl.BlockSpec((tm, k1), lambda i: (i + off, 0)),
                  pl.BlockSpec((tm, k2), lambda i: (i + off, 0)),
                  pl.BlockSpec((k1 + k2, d), lambda i: (0, 0)),
                  pl.BlockSpec((tm, d), lambda i: (i + off, 0)),
                  pl.BlockSpec((1, 1, d), lambda i: (i, 0, 0)),
                  pl.BlockSpec((1, d), lambda i: (0, 0)),
                  pl.BlockSpec((1, d), lambda i: (0, 0))],
        out_specs=pl.BlockSpec((tm, d), lambda i: (i, 0)),
        out_shape=jax.ShapeDtypeStruct((m - skip_rows, d), F32),
        compiler_params=_params("parallel"),
        name="out_proj_ln",
    )(y1, y2, w_bf16, x, gate_t, ln_g.reshape(1, d), ln_b.reshape(1, d))


FF_CHUNK = 256


def _swiglu_rows(h, wg_ref, wu_ref, wd_ref):
    ff = wg_ref.shape[2]
    acc = jnp.zeros((h.shape[0], wd_ref.shape[2]), F32)
    pending = None
    for c in range(ff // FF_CHUNK):
        cols = slice(c * FF_CHUNK, (c + 1) * FF_CHUNK)
        g, u = _dot(h, wg_ref[0, :, cols]), _dot(h, wu_ref[0, :, cols])
        if pending is not None:
            acc = acc + _dot(pending[0], wd_ref[0, pending[1], :])
        pending = ((_silu(g) * u).astype(BF16), cols)
    return acc + _dot(pending[0], wd_ref[0, pending[1], :])


def _ffn_kernel(x_ref, sc_ref, sh_ref, gt_ref, wg_ref, wu_ref, wd_ref, g_ref, b_ref, o_ref):
    x = x_ref[...]
    f = _swiglu_rows((x * sc_ref[0] + sh_ref[0]).astype(BF16), wg_ref, wu_ref, wd_ref)
    o_ref[...] = _layer_norm_rows(DEEPNORM_ALPHA * x + gt_ref[0] * f, g_ref[...], b_ref[...])


def _ffn_ln(x, scale_t, shift_t, gate_t, wg, wu, wd, ln_g, ln_b, tm):
    m, d = x.shape
    ff = wg.shape[2]
    return pl.pallas_call(
        _ffn_kernel,
        grid=(m // tm,),
        in_specs=[pl.BlockSpec((tm, d), lambda i: (i, 0)),
                  pl.BlockSpec((1, 1, d), lambda i: (i, 0, 0)),
                  pl.BlockSpec((1, 1, d), lambda i: (i, 0, 0)),
                  pl.BlockSpec((1, 1, d), lambda i: (i, 0, 0)),
                  pl.BlockSpec((1, d, ff), lambda i: (0, 0, 0)),
                  pl.BlockSpec((1, d, ff), lambda i: (0, 0, 0)),
                  pl.BlockSpec((1, ff, d), lambda i: (0, 0, 0)),
                  pl.BlockSpec((1, d), lambda i: (0, 0)),
                  pl.BlockSpec((1, d), lambda i: (0, 0))],
        out_specs=pl.BlockSpec((tm, d), lambda i: (i, 0)),
        out_shape=jax.ShapeDtypeStruct((m, d), F32),
        compiler_params=_params("parallel"),
        name="ffn_ln",
    )(x, scale_t, shift_t, gate_t, wg, wu, wd, ln_g.reshape(1, d), ln_b.reshape(1, d))


def _router_kernel(x_ref, sc_ref, sh_ref, w_ref, b_ref, o_ref, sel_ref, h_ref):
    h = x_ref[...] * sc_ref[0] + sh_ref[0]
    h_ref[...] = h
    logits = _dot(h, w_ref[...], precision=HIGHEST) + b_ref[...]
    lane = lax.broadcasted_iota(jnp.int32, logits.shape, 1)
    neg = jnp.float32(-jnp.inf)
    logits = jnp.where(lane < N_EXPERTS, logits, neg)
    v1 = jnp.max(logits, axis=1, keepdims=True)
    i1 = jnp.min(jnp.where(logits == v1, lane, 128), axis=1, keepdims=True)
    rest = jnp.where(lane == i1, neg, logits)
    v2 = jnp.max(rest, axis=1, keepdims=True)
    i2 = jnp.min(jnp.where(rest == v2, lane, 128), axis=1, keepdims=True)
    e2 = jnp.exp(v2 - v1)
    p1 = 1.0 / (1.0 + e2)
    p2 = e2 / (1.0 + e2)
    o_ref[...] = jnp.where(lane == i1, p1, 0.0) + jnp.where(lane == i2, p2, 0.0)
    sel_ref[...] = ((lane == i1) | (lane == i2)).astype(F32)


def _router(x, scale_t, shift_t, w_pad, b_pad, tm):
    m, d = x.shape
    lanes = pl.BlockSpec((tm, 128), lambda i: (i, 0))
    return pl.pallas_call(
        _router_kernel,
        grid=(m // tm,),
        in_specs=[pl.BlockSpec((tm, d), lambda i: (i, 0)),
                  pl.BlockSpec((1, 1, d), lambda i: (i, 0, 0)),
                  pl.BlockSpec((1, 1, d), lambda i: (i, 0, 0)),
                  pl.BlockSpec((d, 128), lambda i: (0, 0)),
                  pl.BlockSpec((1, 128), lambda i: (0, 0))],
        out_specs=[lanes, lanes, pl.BlockSpec((tm, d), lambda i: (i, 0))],
        out_shape=[jax.ShapeDtypeStruct((m, 128), F32), jax.ShapeDtypeStruct((m, 128), F32),
                   jax.ShapeDtypeStruct((m, d), F32)],
        compiler_params=_params("parallel"),
        name="router",
    )(x, scale_t, shift_t, w_pad, b_pad)


MOE_TM = 512
MOE_COMBINE_TM = 256
MOE_DISPATCH_TM = 256


def _moe_rows(m):
    return 2 * m + N_EXPERTS * MOE_TM


def _moe_plan(route, sel):
    moe_rows = _moe_rows(route.shape[0])
    sel8 = sel[:, :N_EXPERTS].astype(jnp.int32)
    counts = jnp.sum(sel8, axis=0)
    rank = jnp.cumsum(sel8, axis=0) - sel8
    padded = ((counts + MOE_TM - 1) // MOE_TM) * MOE_TM
    ends = jnp.cumsum(padded)
    starts = ends - padded
    pos = jnp.where(sel8 > 0, starts[None, :] + rank, moe_rows)
    pos1 = jnp.min(pos, axis=1)
    pos2 = jnp.min(jnp.where(pos == pos1[:, None], moe_rows, pos), axis=1)
    route8 = route[:, :N_EXPERTS]
    w1 = jnp.sum(jnp.where(pos == pos1[:, None], route8, 0.0), axis=1, keepdims=True)
    w2 = jnp.sum(jnp.where(pos == pos2[:, None], route8, 0.0), axis=1, keepdims=True)
    tile_start = jnp.arange(moe_rows // MOE_TM, dtype=jnp.int32) * MOE_TM
    tile_expert = jnp.minimum(jnp.sum((tile_start[:, None] >= ends[None, :]).astype(jnp.int32), axis=1),
                              N_EXPERTS - 1).astype(jnp.int32)
    n_used = (ends[-1] // MOE_TM).astype(jnp.int32).reshape(1)
    pad_lo, pad_hi = (starts + counts).astype(jnp.int32), ends.astype(jnp.int32)
    return pos1.astype(jnp.int32), pos2.astype(jnp.int32), w1, w2, tile_expert, n_used, pad_lo, pad_hi


def _moe_dispatch_kernel(p1_ref, p2_ref, lo_ref, hi_ref, nused_ref, h_ref, hs_hbm, zero_ref, sem, zsem):
    i = pl.program_id(0)
    base = i * MOE_DISPATCH_TM

    def row_copy(r, dst_row):
        return pltpu.make_async_copy(h_ref.at[pl.ds(r, 1), :], hs_hbm.at[pl.ds(dst_row, 1), :], sem)

    def start(r, carry):
        row_copy(r, p1_ref[base + r]).start(priority=0)
        row_copy(r, p2_ref[base + r]).start(priority=1)
        return carry

    def wait(r, carry):
        row_copy(0, 0).wait()
        row_copy(0, 0).wait()
        return carry

    lax.fori_loop(0, MOE_DISPATCH_TM, start, 0, unroll=8)
    lax.fori_loop(0, MOE_DISPATCH_TM, wait, 0, unroll=8)

    @pl.when(i == pl.num_programs(0) - 1)
    def _():
        zero_ref[...] = jnp.zeros_like(zero_ref)

        def zero_copy(dst_row):
            return pltpu.make_async_copy(zero_ref.at[pl.ds(0, 1), :], hs_hbm.at[pl.ds(dst_row, 1), :], zsem)

        for e in range(N_EXPERTS):
            lo, hi = lo_ref[e], hi_ref[e]

            def zstart(r, carry):
                zero_copy(r).start()
                return carry

            def zwait(r, carry):
                zero_copy(0).wait()
                return carry

            lax.fori_loop(lo, hi, zstart, 0)
            lax.fori_loop(lo, hi, zwait, 0)

        def tile_copy(t):
            return pltpu.make_async_copy(zero_ref, hs_hbm.at[pl.ds(t * MOE_TM, MOE_TM), :], zsem)

        def tstart(t, carry):
            tile_copy(t).start()
            return carry

        def twait(t, carry):
            tile_copy(0).wait()
            return carry

        n_tiles = hs_hbm.shape[0] // MOE_TM
        lax.fori_loop(nused_ref[0], n_tiles, tstart, 0)
        lax.fori_loop(nused_ref[0], n_tiles, twait, 0)


def _moe_dispatch(h, pos1, pos2, pad_lo, pad_hi, n_used):
    m, d = h.shape
    grid_spec = pltpu.PrefetchScalarGridSpec(
        num_scalar_prefetch=5,
        grid=(m // MOE_DISPATCH_TM,),
        in_specs=[pl.BlockSpec((MOE_DISPATCH_TM, d), lambda i, *_: (i, 0))],
        out_specs=pl.BlockSpec(memory_space=pl.ANY),
        scratch_shapes=[pltpu.VMEM((MOE_TM, d), F32), pltpu.SemaphoreType.DMA(()), pltpu.SemaphoreType.DMA(())],
    )
    return pl.pallas_call(
        _moe_dispatch_kernel,
        grid_spec=grid_spec,
        out_shape=jax.ShapeDtypeStruct((_moe_rows(m), d), F32),
        compiler_params=_params("arbitrary"),
        name="moe_dispatch",
    )(pos1, pos2, pad_lo, pad_hi, n_used, h)


def _moe_expert_kernel(texp_ref, nused_ref, hs_ref, wg_ref, wu_ref, wd_ref, y_ref):
    t = pl.program_id(0)

    @pl.when(t < nused_ref[0])
    def _():
        y_ref[...] = _swiglu_rows(hs_ref[...].astype(BF16), wg_ref, wu_ref, wd_ref)

    @pl.when(t >= nused_ref[0])
    def _():
        y_ref[...] = jnp.zeros_like(y_ref)


def _moe_experts(hs, tile_expert, n_used, wg, wu, wd):
    moe_rows, d = hs.shape
    ff = wg.shape[2]
    grid_spec = pltpu.PrefetchScalarGridSpec(
        num_scalar_prefetch=2,
        grid=(moe_rows // MOE_TM,),
        in_specs=[pl.BlockSpec((MOE_TM, d), lambda t, te, nu: (jnp.minimum(t, nu[0] - 1), 0)),
                  pl.BlockSpec((1, d, ff), lambda t, te, nu: (te[t], 0, 0)),
                  pl.BlockSpec((1, d, ff), lambda t, te, nu: (te[t], 0, 0)),
                  pl.BlockSpec((1, ff, d), lambda t, te, nu: (te[t], 0, 0))],
        out_specs=pl.BlockSpec((MOE_TM, d), lambda t, te, nu: (t, 0)),
    )
    return pl.pallas_call(
        _moe_expert_kernel,
        grid_spec=grid_spec,
        out_shape=jax.ShapeDtypeStruct((moe_rows, d), F32),
        compiler_params=_params("arbitrary"),
        name="moe_experts",
    )(tile_expert, n_used, hs, wg, wu, wd)


def _moe_combine_kernel(p1_ref, p2_ref, y_hbm, x_ref, w1_ref, w2_ref, gt_ref, g_ref, b_ref, o_ref, buf1, buf2, sem):
    base = pl.program_id(0) * MOE_COMBINE_TM

    def row_copy(buf, k, r, src_row):
        return pltpu.make_async_copy(y_hbm.at[pl.ds(src_row, 1), :], buf.at[pl.ds(r, 1), :], sem.at[k])

    def start(r, carry):
        row_copy(buf1, 0, r, p1_ref[base + r]).start(priority=0)
        row_copy(buf2, 1, r, p2_ref[base + r]).start(priority=1)
        return carry

    def wait(r, carry):
        row_copy(buf1, 0, 0, 0).wait()
        row_copy(buf2, 1, 0, 0).wait()
        return carry

    lax.fori_loop(0, MOE_COMBINE_TM, start, 0, unroll=8)
    lax.fori_loop(0, MOE_COMBINE_TM, wait, 0, unroll=8)
    f = w1_ref[...] * buf1[...] + w2_ref[...] * buf2[...]
    z = DEEPNORM_ALPHA * x_ref[...] + gt_ref[0] * f
    o_ref[...] = _layer_norm_rows(z, g_ref[...], b_ref[...])


def _moe_combine_ln(y_sorted, pos1, pos2, w1, w2, x, gate_t, ln_g, ln_b):
    m, d = x.shape
    tm = MOE_COMBINE_TM
    grid_spec = pltpu.PrefetchScalarGridSpec(
        num_scalar_prefetch=2,
        grid=(m // tm,),
        in_specs=[pl.BlockSpec(memory_space=pl.ANY),
                  pl.BlockSpec((tm, d), lambda i, p1, p2: (i, 0)),
                  pl.BlockSpec((tm, 1), lambda i, p1, p2: (i, 0)),
                  pl.BlockSpec((tm, 1), lambda i, p1, p2: (i, 0)),
                  pl.BlockSpec((1, 1, d), lambda i, p1, p2: (i, 0, 0)),
                  pl.BlockSpec((1, d), lambda i, p1, p2: (0, 0)),
                  pl.BlockSpec((1, d), lambda i, p1, p2: (0, 0))],
        out_specs=pl.BlockSpec((tm, d), lambda i, p1, p2: (i, 0)),
        scratch_shapes=[pltpu.VMEM((tm, d), F32), pltpu.VMEM((tm, d), F32), pltpu.SemaphoreType.DMA((2,))],
    )
    return pl.pallas_call(
        _moe_combine_kernel,
        grid_spec=grid_spec,
        out_shape=jax.ShapeDtypeStruct((m, d), F32),
        compiler_params=_params("arbitrary"),
        name="moe_combine_ln",
    )(pos1, pos2, y_sorted, x, w1, w2, gate_t, ln_g.reshape(1, d), ln_b.reshape(1, d))


HGRN_CHUNK = 16
HGRN_SUPER = 64


def _hgrn_kernel(reverse, finish, *refs):
    if finish:
        q_ref, f_ref, i_ref, lb_ref, of_ref, g_ref, ng_ref, o_ref, st_ref = refs
    else:
        q_ref, f_ref, i_ref, lb_ref, o_ref, st_ref = refs
    C, S = HGRN_CHUNK, HGRN_SUPER
    nsub = S // C
    n_super = ROW_TILE // S

    @pl.when(pl.program_id(1) == 0)
    def _():
        st_ref[...] = jnp.zeros_like(st_ref)

    row = lax.broadcasted_iota(jnp.int32, (S, S), 0)
    col = lax.broadcasted_iota(jnp.int32, (S, S), 1)
    blk_r, blk_c = jnp.zeros_like(row), jnp.zeros_like(col)
    for j in range(1, nsub):
        blk_r = blk_r + (row >= j * C).astype(jnp.int32)
        blk_c = blk_c + (col >= j * C).astype(jnp.int32)
    col_in = col - blk_c * C
    keep = (blk_r == blk_c) & ((row <= col) if reverse else (row >= col))
    tri = keep.astype(F32)
    heads = range(A_HEADS)
    sls = [slice(h * A_DK, (h + 1) * A_DK) for h in heads]
    chunk_rows = [slice(c * C, (c + 1) * C) for c in range(nsub)]
    width = A_HEADS * A_DK

    def per_chunk_row(t, offset):
        return jnp.concatenate(
            [jnp.broadcast_to(t[c * C + offset:c * C + offset + 1], (C, width)) for c in range(nsub)], axis=0)

    def superchunk(si, carry):
        sc = (n_super - 1 - si) if reverse else si
        rows = pl.ds(pl.multiple_of(sc * S, S), S)
        lb = lb_ref[...]
        q = _silu(q_ref[rows, :]) * A_DK ** -0.5
        f = lb + (1.0 - lb) * _sigmoid(f_ref[rows, :])
        k = 1.0 - f
        v = i_ref[rows, :]
        b = _dot_sel(tri, jnp.log(f))
        last = 0 if reverse else C - 1
        b_tot = per_chunk_row(b, last)
        qe = q * jnp.exp(b)
        kt = k * jnp.exp(b_tot - b)
        atts = [jnp.zeros((S, S), F32) for _ in heads]
        for s in range(C):
            tmp = q * jnp.exp(b - per_chunk_row(b, s)) * per_chunk_row(k, s)
            atts = [jnp.where(col_in == s, jnp.sum(tmp[:, sl], axis=1, keepdims=True), att)
                    for sl, att in zip(sls, atts)]
        intra = [_bdot(jnp.where(keep, att, 0.0), v[:, sl]) for sl, att in zip(sls, atts)]
        upds = [[_bdot(v[cr, sl], kt[cr, sl], trans_a=True) for cr in chunk_rows] for sl in sls]
        sts = [st_ref[h] for h in heads]
        inter = [[None] * nsub for _ in heads]
        for c in (range(nsub - 1, -1, -1) if reverse else range(nsub)):
            cr = chunk_rows[c]
            decay = jnp.exp(b[c * C + last:c * C + last + 1])
            for h in heads:
                inter[h][c] = _bdot(qe[cr, sls[h]], sts[h], trans_b=True)
                sts[h] = sts[h] * decay[:, sls[h]] + upds[h][c]
        outs = []
        for h in heads:
            st_ref[h] = sts[h]
            o = intra[h] + jnp.concatenate(inter[h], axis=0)
            if finish:
                o = o + of_ref[rows, sls[h]]
                o = o * lax.rsqrt(jnp.mean(o * o, axis=1, keepdims=True) + 1e-6)
                o = o * ng_ref[:, sls[h]] * _silu(g_ref[rows, sls[h]])
            outs.append(o)
        for h in heads:
            o_ref[rows, sls[h]] = outs[h]
        return carry

    lax.fori_loop(0, n_super, superchunk, 0)


def _hgrn_pass(reverse, q, f, i, lb, finish_args=None):
    m, w = q.shape
    seq_spec = pl.BlockSpec((ROW_TILE, w), lambda b, t: (_seq_block(b, _tile_order(t, reverse)), 0))
    vec_spec = pl.BlockSpec((1, w), lambda b, t: (0, 0))
    ins = [q, f, i, lb]
    specs = [seq_spec, seq_spec, seq_spec, vec_spec]
    if finish_args is not None:
        of, g, ng = finish_args
        ins += [of, g, ng]
        specs += [seq_spec, seq_spec, vec_spec]
    return pl.pallas_call(
        functools.partial(_hgrn_kernel, reverse, finish_args is not None),
        grid=(BATCH, SEQ_TILES),
        in_specs=specs,
        out_specs=seq_spec,
        out_shape=jax.ShapeDtypeStruct((m, w), F32),
        scratch_shapes=[pltpu.VMEM((A_HEADS, A_DK, A_DK), F32)],
        compiler_params=_params("parallel", "arbitrary"),
        name="hgrn_bwd" if reverse else "hgrn_fwd",
    )(*ins)


def _rope_kernel(q_ref, k_ref, cos_ref, sa_ref, sb_ref, qo_ref, ko_ref):
    cos, sa, sb = cos_ref[...], sa_ref[...], sb_ref[...]

    def rot(x):
        return x * cos + pltpu.roll(x, 96, 1) * sa + pltpu.roll(x, 32, 1) * sb

    for j in range(B_QW // 128):
        qo_ref[:, j * 128:(j + 1) * 128] = rot(q_ref[:, j * 128:(j + 1) * 128])
    ko_ref[...] = rot(k_ref[...])


def _rope(q, k, cos_t, sa_t, sb_t):
    m = q.shape[0]
    seq = lambda w: pl.BlockSpec((ROW_TILE, w), lambda b, t: (_seq_block(b, t), 0))
    tab = pl.BlockSpec((ROW_TILE, 128), lambda b, t: (t, 0))
    return pl.pallas_call(
        _rope_kernel,
        grid=(BATCH, SEQ_TILES),
        in_specs=[seq(B_QW), seq(B_KVW), tab, tab, tab],
        out_specs=[seq(B_QW), seq(B_KVW)],
        out_shape=[jax.ShapeDtypeStruct((m, B_QW), F32), jax.ShapeDtypeStruct((m, B_KVW), F32)],
        compiler_params=_params("parallel", "parallel"),
        name="rope",
    )(q, k, cos_t, sa_t, sb_t)


def _rope_tables():
    rows = SEQ // GRID_W
    row = jnp.repeat(jnp.arange(rows, dtype=F32), GRID_W)
    colp = jnp.tile(jnp.arange(GRID_W, dtype=F32), rows)
    n_freq = B_HEAD_DIM // 4
    inv = ROPE_BASE ** (-jnp.arange(n_freq, dtype=F32) / n_freq)
    ang = jnp.concatenate([row[:, None] * inv, colp[:, None] * inv], axis=-1)
    cos, sin = jnp.cos(ang), jnp.sin(ang)
    zero = jnp.zeros_like(sin)
    cos_l = jnp.tile(cos, (1, 4))
    sa_l = jnp.tile(jnp.concatenate([-sin, zero], axis=-1), (1, 2))
    sb_l = jnp.tile(jnp.concatenate([zero, sin], axis=-1), (1, 2))
    pad = lambda t, v: jnp.concatenate([jnp.full((CTX_LEN, 128), v, F32), t], axis=0)
    return pad(cos_l, 1.0), pad(sa_l, 0.0), pad(sb_l, 0.0)


def _attend(q_ref, o_ref, sink_ref, k, v, mask):
    tq = q_ref.shape[0]
    left = lax.broadcasted_iota(jnp.int32, (tq, 128), 1) < B_HEAD_DIM
    lk = lax.broadcasted_iota(jnp.int32, k.shape, 1) < B_HEAD_DIM
    neg = jnp.float32(-jnp.inf)
    scale = B_HEAD_DIM ** -0.5
    kr, vr = pltpu.roll(k, 64, 1), pltpu.roll(v, 64, 1)
    kds = [jnp.where(lk, k, kr).astype(_MXU_DTYPE), jnp.where(lk, kr, k).astype(_MXU_DTYPE)]
    vds = [jnp.where(lk, v, vr), jnp.where(lk, vr, v)]
    vsel = [[jnp.where(lk if half == 0 else jnp.logical_not(lk), vd, 0.0).astype(_MXU_DTYPE) for half in range(2)]
            for vd in vds]
    pairs = B_HEADS // 2
    heads = [(p, half) for p in range(pairs) for half in range(2)]
    group = lambda p: p // (pairs // B_KV_HEADS)
    qms = [jnp.where(left if half == 0 else jnp.logical_not(left), q_ref[:, p * 128:(p + 1) * 128] * scale, 0.0)
           .astype(_MXU_DTYPE) for p, half in heads]
    scores = [_dot(qm, kds[group(p)], trans_b=True) for (p, half), qm in zip(heads, qms)]
    if mask is not None:
        scores = [jnp.where(mask, s, neg) for s in scores]
    sinks = [sink_ref[2 * p + half:2 * p + half + 1, 0:1] for p, half in heads]
    mxs = [jnp.maximum(sink, jnp.max(s, axis=1, keepdims=True)) for s, sink in zip(scores, sinks)]
    es = [jnp.exp(s - mx) for s, mx in zip(scores, mxs)]
    denoms = [jnp.exp(sink - mx) + jnp.sum(e, axis=1, keepdims=True) for e, mx, sink in zip(es, mxs, sinks)]
    accs = [_dot(e.astype(_MXU_DTYPE), vsel[group(p)][half]) for (p, half), e in zip(heads, es)]
    for p in range(pairs):
        o_ref[:, p * 128:(p + 1) * 128] = accs[2 * p] / denoms[2 * p] + accs[2 * p + 1] / denoms[2 * p + 1]


ATTN_LATENT_BLOCKS = SEQ // B_BLOCK
ATTN_CTX_BLOCKS = CTX_LEN // B_BLOCK


def _attn_kernel(q_ref, kp_ref, kc_ref, kn_ref, vp_ref, vc_ref, vn_ref, kx_ref, vx_ref, sink_ref, o_ref):
    n = pl.program_id(1)
    nb = ATTN_LATENT_BLOCKS

    @pl.when(n < nb)
    def _():
        row = lax.broadcasted_iota(jnp.int32, (B_BLOCK, B_BLOCK), 0)
        col = lax.broadcasted_iota(jnp.int32, (B_BLOCK, B_BLOCK), 1)
        mask = jnp.concatenate([(col >= row) & (n > 0), jnp.ones((B_BLOCK, B_BLOCK), jnp.bool_),
                                (col <= row) & (n < nb - 1), jnp.ones((B_BLOCK, CTX_LEN), jnp.bool_)], axis=1)
        k = jnp.concatenate([kp_ref[...], kc_ref[...], kn_ref[...], kx_ref[...]], axis=0)
        v = jnp.concatenate([vp_ref[...], vc_ref[...], vn_ref[...], vx_ref[...]], axis=0)
        _attend(q_ref, o_ref, sink_ref, k, v, mask)

    @pl.when(n >= nb)
    def _():
        _attend(q_ref, o_ref, sink_ref, kx_ref[...], vx_ref[...], None)


def _attention(q, k, v, sink_rows):
    m = q.shape[0]
    nb, nc = ATTN_LATENT_BLOCKS, ATTN_CTX_BLOCKS
    base = CTX_ROWS // B_BLOCK

    def q_block(b, n):
        return jnp.where(n < nb, base + b * nb + n, b * nc + n - nb)

    qspec = pl.BlockSpec((B_BLOCK, B_QW), lambda b, n: (q_block(b, n), 0))

    def kv(shift):
        return pl.BlockSpec((B_BLOCK, B_KVW), lambda b, n: (base + b * nb + jnp.clip(n + shift, 0, nb - 1), 0))

    ctx_kv = pl.BlockSpec((CTX_LEN, B_KVW), lambda b, n: (b, 0))
    sink_spec = pl.BlockSpec((B_HEADS, 128), lambda b, n: (0, 0))
    return pl.pallas_call(
        _attn_kernel,
        grid=(BATCH, nb + nc),
        in_specs=[qspec, kv(-1), kv(0), kv(1), kv(-1), kv(0), kv(1), ctx_kv, ctx_kv, sink_spec],
        out_specs=qspec,
        out_shape=jax.ShapeDtypeStruct((m, B_QW), F32),
        compiler_params=_params("parallel", "parallel"),
        name="attention",
    )(q, k, k, k, v, v, v, k, v, sink_rows)


def _window_attention(q, k, v, sink):
    cos_t, sa_t, sb_t = _rope_tables()
    qr, kr = _rope(q, k, cos_t, sa_t, sb_t)
    sink_rows = jnp.broadcast_to(sink.astype(F32)[:, None], (B_HEADS, 128))
    return _attention(qr, kr, v, sink_rows)


def _hgrn2(q, f_fw, f_bw, i, g, lb, norm_g):
    lb = lb.reshape(1, A_WIDTH)
    o_fw = _hgrn_pass(False, q, f_fw, i, lb)
    return _hgrn_pass(True, q, f_bw, i, lb, (o_fw, g, norm_g.reshape(1, A_WIDTH)))


S5_STEPS = 64
S5_ROWS = S5_STEPS * BATCH
S5_TILES = (CTX_LEN + SEQ) // S5_STEPS
S5_CTX_TILES = CTX_LEN // S5_STEPS
S5_BLOCKS = 4
S5_BLOCK_IN = C_WIDTH // S5_BLOCKS
S5_BLOCK_STATE = C_GROUPS * C_STATE // S5_BLOCKS


def _s5_kernel(reverse, finish, *refs):
    if finish:
        u_ref, a_ref, wb_ref, wc_ref, yf_ref, glu_ref, y_ref, x_ref, st_ref = refs
    else:
        u_ref, a_ref, wb_ref, wc_ref, d_ref, y_ref, x_ref, st_ref = refs
    ns = S5_BLOCK_STATE

    @pl.when(pl.program_id(0) == 0)
    def _():
        st_ref[...] = jnp.zeros_like(st_ref)

    for k in range(S5_BLOCKS):
        x_ref[:, 2 * ns * k:2 * ns * (k + 1)] = _bdot(u_ref[:, S5_BLOCK_IN * k:S5_BLOCK_IN * (k + 1)], wb_ref[k])

    def step(tt, carry):
        t = (S5_STEPS - 1 - tt) if reverse else tt
        rows = pl.ds(pl.multiple_of(t * BATCH, BATCH), BATCH)
        for k in range(S5_BLOCKS):
            re = slice(2 * ns * k, 2 * ns * k + ns)
            im = slice(2 * ns * k + ns, 2 * ns * (k + 1))
            ar, ai = a_ref[:, re], a_ref[:, im]
            sr, si = st_ref[:, re], st_ref[:, im]
            nr = ar * sr - ai * si + x_ref[rows, re]
            ni = ar * si + ai * sr + x_ref[rows, im]
            st_ref[:, re] = nr
            st_ref[:, im] = ni
            x_ref[rows, re] = nr
            x_ref[rows, im] = ni
        return carry

    lax.fori_loop(0, S5_STEPS, step, 0)

    for k in range(S5_BLOCKS):
        cols = slice(S5_BLOCK_IN * k, S5_BLOCK_IN * (k + 1))
        y = _bdot(x_ref[:, 2 * ns * k:2 * ns * (k + 1)], wc_ref[k])
        if finish:
            y_ref[:, cols] = y + yf_ref[:, cols]
        else:
            y_ref[:, cols] = y + d_ref[:, cols] * u_ref[:, cols]
    if finish:
        y = jax.nn.gelu(y_ref[...])
        y_ref[...] = y * _sigmoid(_bdot(y, glu_ref[...]))


def _s5_tile_order(i, reverse):
    if not reverse:
        return i
    return jnp.where(i < S5_CTX_TILES, S5_CTX_TILES - 1 - i, S5_TILES + S5_CTX_TILES - 1 - i)


def _s5_pass(reverse, u_tm, acoef, wb, wc, extra):
    m, w = u_tm.shape
    nstate = 2 * C_GROUPS * C_STATE
    row_spec = pl.BlockSpec((S5_ROWS, w), lambda i: (_s5_tile_order(i, reverse), 0))
    full = lambda a: pl.BlockSpec(a.shape, lambda i: (0,) * a.ndim)
    finish = reverse
    if finish:
        yf, glu_w = extra
        ins, specs = [u_tm, acoef, wb, wc, yf, glu_w], [row_spec, full(acoef), full(wb), full(wc), row_spec, full(glu_w)]
    else:
        (dskip,) = extra
        ins, specs = [u_tm, acoef, wb, wc, dskip], [row_spec, full(acoef), full(wb), full(wc), full(dskip)]
    return pl.pallas_call(
        functools.partial(_s5_kernel, reverse, finish),
        grid=(S5_TILES,),
        in_specs=specs,
        out_specs=row_spec,
        out_shape=jax.ShapeDtypeStruct((m, w), F32),
        scratch_shapes=[pltpu.VMEM((S5_ROWS, nstate), F32), pltpu.VMEM((BATCH, nstate), F32)],
        compiler_params=_params("arbitrary"),
        name="s5_bwd" if reverse else "s5_fwd",
    )(*ins)


def _s5_discretize(lam_re, lam_im, log_dt, b_re, b_im):
    lam_re = jnp.minimum(lam_re.astype(F32), -1e-4)
    lam_im = lam_im.astype(F32)
    dt = jnp.exp(log_dt.astype(F32))[:, None]
    mag = jnp.exp(lam_re * dt)
    ab_re, ab_im = mag * jnp.cos(lam_im * dt), mag * jnp.sin(lam_im * dt)
    den = lam_re ** 2 + lam_im ** 2
    nr = ab_re - 1.0
    co_re = (nr * lam_re + ab_im * lam_im) / den
    co_im = (ab_im * lam_re - nr * lam_im) / den
    bb_re = co_re[..., None] * b_re - co_im[..., None] * b_im
    bb_im = co_re[..., None] * b_im + co_im[..., None] * b_re
    return ab_re, ab_im, bb_re, bb_im


def _s5_tables(lam_re, lam_im, log_dt, b_re, b_im, c_re, c_im):
    eye = jnp.eye(C_GROUPS // S5_BLOCKS, dtype=F32)
    gb = C_GROUPS // S5_BLOCKS

    def in_map(bb):
        return jnp.einsum('kgph,gG->kghGp', bb.reshape(S5_BLOCKS, gb, C_STATE, C_GROUP), eye).reshape(
            S5_BLOCKS, S5_BLOCK_IN, S5_BLOCK_STATE)

    def out_map(cc):
        return jnp.einsum('kghp,gG->kgpGh', cc.reshape(S5_BLOCKS, gb, C_GROUP, C_STATE), eye).reshape(
            S5_BLOCKS, S5_BLOCK_STATE, S5_BLOCK_IN)

    wc = jnp.concatenate([out_map(c_re.astype(F32)), -out_map(c_im.astype(F32))], axis=1).astype(_MXU_DTYPE)
    tables = []
    for d in range(2):
        ab_re, ab_im, bb_re, bb_im = _s5_discretize(lam_re[d], lam_im[d], log_dt[d], b_re.astype(F32), b_im.astype(F32))
        a = jnp.concatenate([ab_re.reshape(S5_BLOCKS, S5_BLOCK_STATE), ab_im.reshape(S5_BLOCKS, S5_BLOCK_STATE)], axis=1)
        acoef = jnp.broadcast_to(a.reshape(1, -1), (BATCH, 2 * C_GROUPS * C_STATE))
        wb = jnp.concatenate([in_map(bb_re), in_map(bb_im)], axis=2).astype(_MXU_DTYPE)
        tables.append((acoef, wb))
    return tables, wc


def _to_time_major(y):
    w = y.shape[1]
    c = y[:CTX_ROWS].reshape(BATCH, CTX_LEN, w).transpose(1, 0, 2).reshape(CTX_ROWS, w)
    l = y[CTX_ROWS:].reshape(BATCH, SEQ, w).transpose(1, 0, 2).reshape(BATCH * SEQ, w)
    return jnp.concatenate([c, l], axis=0)


def _from_time_major(y):
    w = y.shape[1]
    c = y[:CTX_ROWS].reshape(CTX_LEN, BATCH, w).transpose(1, 0, 2).reshape(CTX_ROWS, w)
    l = y[CTX_ROWS:].reshape(SEQ, BATCH, w).transpose(1, 0, 2).reshape(BATCH * SEQ, w)
    return jnp.concatenate([c, l], axis=0)


def _s5(u, lam_re, lam_im, log_dt, b_re, b_im, c_re, c_im, d_skip, glu_w):
    (fw, bw), wc = _s5_tables(lam_re, lam_im, log_dt, b_re, b_im, c_re, c_im)
    u_tm = _to_time_major(u)
    y_fw = _s5_pass(False, u_tm, fw[0], fw[1], wc, (d_skip.astype(F32).reshape(1, C_WIDTH),))
    y = _s5_pass(True, u_tm, bw[0], bw[1], wc, (y_fw, glu_w.astype(_MXU_DTYPE)))
    return _from_time_major(y)


RW_LORA_OFF = 3 * R_WIDTH


def _softplus(z):
    return jnp.maximum(z, 0.0) + jnp.log1p(jnp.exp(-jnp.abs(z)))


def _rwkv_prep_kernel(p_ref, hp_ref, hn_ref, mu_ref, w0_ref, w2_ref, a0_ref, a2_ref, g2_ref, kk_ref, ka_ref, rk_ref,
                      ones_ref, r_o, v_o, g_o, bonus_o, kkn_o, lwf_o, kdf_o, bf_o, lwb_o, kdb_o, bb_o):
    i = pl.program_id(1)
    x = p_ref[...]
    rows = x.shape[0]
    rowi = lax.broadcasted_iota(jnp.int32, (rows, 1), 0)
    prev_row = jnp.where(i >= 2, hp_ref[7:8, :], 0.0)
    next_row = jnp.where((i >= 1) & (i < SEQ_TILES - 1), hn_ref[0:1, :], 0.0)
    prev = jnp.where(rowi == 0, prev_row, pltpu.roll(x, 1, 0))
    nxt = jnp.where(rowi == rows - 1, next_row, pltpu.roll(x, rows - 1, 0))
    x = x + mu_ref[...] * (0.5 * (prev + nxt) - x)

    r = x[:, 0:R_WIDTH]
    k = x[:, R_WIDTH:2 * R_WIDTH]
    v = x[:, 2 * R_WIDTH:3 * R_WIDTH]
    wd = x[:, RW_LORA_OFF:RW_LORA_OFF + 128]
    ad = x[:, RW_LORA_OFF + 128:RW_LORA_OFF + 256]
    gd = x[:, RW_LORA_OFF + 256:RW_LORA_OFF + 384]
    ones = ones_ref[...]

    r_o[...] = r
    v_o[...] = v
    g_o[...] = _bdot(_sigmoid(gd), g2_ref[...])
    bonus_o[...] = _dot_by_sel(r * k * rk_ref[...], ones) * v
    kk = k * kk_ref[...]
    kkn = kk / jnp.maximum(jnp.sqrt(_dot_by_sel(kk * kk, ones)), 1e-12)
    kkn_o[...] = kkn
    tw = jnp.tanh(wd)
    for d, (lw_o, kd_o, b_o) in enumerate(((lwf_o, kdf_o, bf_o), (lwb_o, kdb_o, bb_o))):
        w = -_softplus(-(w0_ref[d:d + 1, :] + _bdot(tw, w2_ref[d]))) - 0.5
        lw_o[...] = -jnp.exp(w)
        a = _sigmoid(a0_ref[d:d + 1, :] + _bdot(ad, a2_ref[d]))
        kd_o[...] = k * (1.0 + (a - 1.0) * ka_ref[...])
        b_o[...] = kkn * a


def _rwkv_prep(p, mu, w0, w2pad, a0, a2pad, g2, k_k, k_a, r_k, ones_blk):
    m, w = p.shape
    hb = ROW_TILE // 8
    seq = lambda width: pl.BlockSpec((ROW_TILE, width), lambda b, i: (_seq_block(b, i), 0))
    halo_prev = pl.BlockSpec((8, w), lambda b, i: (jnp.maximum(_seq_block(b, i) * hb - 1, 0), 0))
    halo_next = pl.BlockSpec((8, w), lambda b, i: (jnp.minimum((_seq_block(b, i) + 1) * hb, m // 8 - 1), 0))
    full = lambda a: pl.BlockSpec(a.shape, lambda b, i: (0,) * a.ndim)
    consts = [mu, w0, w2pad, a0, a2pad, g2, k_k, k_a, r_k, ones_blk]
    return pl.pallas_call(
        _rwkv_prep_kernel,
        grid=(BATCH, SEQ_TILES),
        in_specs=[seq(w), halo_prev, halo_next] + [full(c) for c in consts],
        out_specs=[seq(R_WIDTH)] * 11,
        out_shape=[jax.ShapeDtypeStruct((m, R_WIDTH), F32)] * 11,
        compiler_params=_params("parallel", "parallel"),
        name="rwkv_prep",
    )(p, p, p, *consts)


RW_CHUNK = 64
RW_SEQS = 4


def _rwkv_scan_kernel(reverse, finish, *refs):
    per_seq = 9 if finish else 6
    seq_refs = [refs[q * per_seq:(q + 1) * per_seq] for q in range(RW_SEQS)]
    rest = refs[RW_SEQS * per_seq:]
    if finish:
        lng_ref, lnb_ref = rest[:2]
        rest = rest[2:]
    o_refs, st_ref = rest[:RW_SEQS], rest[RW_SEQS]
    C = RW_CHUNK
    P = 2 * R_HEAD
    n_chunks = ROW_TILE // C

    @pl.when(pl.program_id(1) == 0)
    def _():
        st_ref[...] = jnp.zeros_like(st_ref)

    row = lax.broadcasted_iota(jnp.int32, (C, C), 0)
    col = lax.broadcasted_iota(jnp.int32, (C, C), 1)
    incl = (row <= col) if reverse else (row >= col)
    strict = (row < col) if reverse else (row > col)
    tri = incl.astype(F32)
    row2 = lax.broadcasted_iota(jnp.int32, (C, 2 * C), 0)
    col2 = lax.broadcasted_iota(jnp.int32, (C, 2 * C), 1)
    s2 = jnp.where(col2 < C, col2, col2 - C)
    incl2 = (row2 <= s2) if reverse else (row2 >= s2)
    strict_k = ((row2 < s2) if reverse else (row2 > s2)) & (col2 >= C)
    left = lax.broadcasted_iota(jnp.int32, (C, P), 1) < R_HEAD
    left2 = lax.broadcasted_iota(jnp.int32, (2 * C, P), 1) < R_HEAD
    blockdiag = ((lax.broadcasted_iota(jnp.int32, (P, P), 0) < R_HEAD)
                 == (lax.broadcasted_iota(jnp.int32, (P, P), 1) < R_HEAD))
    steps = int(math.log2(C))

    def chunk(ci, carry):
        c = (n_chunks - 1 - ci) if reverse else ci
        rows = pl.ds(pl.multiple_of(c * C, C), C)
        npair = R_HEADS // 2
        pairs = range(RW_SEQS * npair)
        heads = [(p, half) for p in pairs for half in range(2)]
        lane_sl = [slice((p % npair) * P, (p % npair + 1) * P) for p in pairs]
        sts = [st_ref[p] for p in pairs]
        vps, rhss, ars, tails, decays, fin = [], [], [], [], [], []
        for q in range(RW_SEQS):
            r_ref, kd_ref, v_ref, lw_ref, kkn_ref, b_ref = seq_refs[q][:6]
            lw = lw_ref[rows, :]
            kd, bv, v_all = kd_ref[rows, :], b_ref[rows, :], v_ref[rows, :]
            gi = _dot_sel(tri, lw)
            g_tot = gi[0:1] if reverse else gi[C - 1:C]
            inv = jnp.exp(-gi)
            a_t = -kkn_ref[rows, :] * jnp.exp(gi - lw)
            r_t = r_ref[rows, :] * jnp.exp(gi)
            b_t, k_t = bv * inv, kd * inv
            tail = jnp.exp(g_tot - gi)
            b_tail, k_tail = bv * tail, kd * tail
            decay = jnp.exp(g_tot)
            for j in range(npair):
                sl = lane_sl[j]
                vps.append(v_all[:, sl])
                rhss.append(jnp.concatenate([b_t[:, sl], k_t[:, sl], sts[q * npair + j]], axis=0).astype(_MXU_DTYPE))
                ars.append(jnp.concatenate([a_t[:, sl], r_t[:, sl]], axis=0))
                tails.append(jnp.concatenate([b_tail[:, sl], k_tail[:, sl]], axis=0))
                decays.append(decay[:, sl])
                if finish:
                    of_ref, g_ref, bonus_ref = seq_refs[q][6:9]
                    fin.append((of_ref[rows, sl], g_ref[rows, sl], bonus_ref[rows, sl]))
        zvs = [jnp.concatenate([jnp.zeros((C, P), F32), vp], axis=0).astype(_MXU_DTYPE) for vp in vps]
        prods = [_dot(jnp.where(left2 if half == 0 else jnp.logical_not(left2), ars[p], 0.0).astype(_MXU_DTYPE),
                      rhss[p], trans_b=True) for p, half in heads]
        ahs = [prod[:, 2 * C:] for prod in prods]
        us = [ah[:C] + _dot(jnp.where(strict_k, prod[:C, :2 * C], 0.0).astype(_MXU_DTYPE), zvs[p])
              for (p, half), prod, ah in zip(heads, prods, ahs)]
        npows = [jnp.where(strict, prod[:C, :C], 0.0) for prod in prods]
        for it in range(steps):
            us = [u + _bdot(npow, u) for u, npow in zip(us, npows)]
            if it + 1 < steps:
                npows = [_bdot(npow, npow) for npow in npows]
        os_ = [ah[C:] + _bdot(jnp.where(incl2, prod[C:, :2 * C], 0.0), jnp.concatenate([u, vps[p]], axis=0))
               for (p, half), prod, ah, u in zip(heads, prods, ahs, us)]
        outs = []
        for p in pairs:
            sl, st, vp = lane_sl[p], sts[p], vps[p]
            u_pair = jnp.where(left, us[2 * p], us[2 * p + 1])
            o = jnp.where(left, os_[2 * p], os_[2 * p + 1])
            upd = _bdot(jnp.concatenate([u_pair, vp], axis=0), tails[p], trans_a=True)
            st_ref[p] = jnp.where(blockdiag, st * decays[p] + upd, 0.0)
            if finish:
                of_p, g_p, bonus_p = fin[p]
                o = o + of_p

                def head_mean(t):
                    tot = jnp.sum(t, axis=1, keepdims=True)
                    lsum = jnp.sum(jnp.where(left, t, 0.0), axis=1, keepdims=True)
                    return jnp.where(left, lsum, tot - lsum) * (1.0 / R_HEAD)

                oc = o - head_mean(o)
                o = oc * lax.rsqrt(head_mean(oc * oc) + RWKV_GN_EPS) * lng_ref[:, sl] + lnb_ref[:, sl]
                o = (o + bonus_p) * g_p
            outs.append(o)
        for p, o in enumerate(outs):
            o_refs[p // npair][rows, lane_sl[p]] = o
        return carry

    lax.fori_loop(0, n_chunks, chunk, 0)


def _rwkv_scan(reverse, r, kd, v, lw, kkn, bvec, finish_args=None):
    m, w = r.shape
    nb = BATCH // RW_SEQS

    def seq_spec(q):
        return pl.BlockSpec((ROW_TILE, w), lambda b, t: (_seq_block(b + q * nb, _tile_order(t, reverse)), 0))

    def part_block(b, i):
        return jnp.where(i == 0, b, nb + b * (SEQ // ROW_TILE) + i - 1)

    part_spec = pl.BlockSpec((ROW_TILE, w), lambda b, t: (part_block(b, _tile_order(t, reverse)), 0))
    vec_spec = pl.BlockSpec((1, w), lambda b, t: (0, 0))
    ins, specs = [], []
    for q in range(RW_SEQS):
        ins += [r, kd, v, lw, kkn, bvec]
        specs += [seq_spec(q)] * 6
        if finish_args is not None:
            o_parts, g, bonus = finish_args[:3]
            ins += [o_parts[q], g, bonus]
            specs += [part_spec, seq_spec(q), seq_spec(q)]
    if finish_args is not None:
        ins += list(finish_args[3:])
        specs += [vec_spec, vec_spec]
    return pl.pallas_call(
        functools.partial(_rwkv_scan_kernel, reverse, finish_args is not None),
        grid=(nb, SEQ_TILES),
        in_specs=specs,
        out_specs=[part_spec] * RW_SEQS,
        out_shape=[jax.ShapeDtypeStruct((m // RW_SEQS, w), F32)] * RW_SEQS,
        scratch_shapes=[pltpu.VMEM((RW_SEQS * R_HEADS // 2, 2 * R_HEAD, 2 * R_HEAD), F32)],
        compiler_params=_params("parallel", "arbitrary"),
        name="rwkv_bwd" if reverse else "rwkv_fwd",
    )(*ins)


def _merge_seq_parts(parts):
    c = CTX_ROWS // RW_SEQS
    return jnp.concatenate([p[:c] for p in parts] + [p[c:] for p in parts], axis=0)


def _rwkv7(p, mu, w0, w2, a0, a2, g2, k_k, k_a, r_k, lnx_g, lnx_b):
    zeros_w = jnp.zeros((R_LORA_W, R_WIDTH), F32)
    w2pad = jnp.stack([jnp.concatenate([w2[0], zeros_w], 0), jnp.concatenate([zeros_w, w2[1]], 0)]).astype(_MXU_DTYPE)
    a2pad = jnp.stack([jnp.concatenate([a2[0], zeros_w], 0), jnp.concatenate([zeros_w, a2[1]], 0)]).astype(_MXU_DTYPE)
    head = np.arange(R_WIDTH) // R_HEAD
    ones_blk = jnp.asarray(head[:, None] == head[None, :], F32)
    row = lambda t: t.astype(F32).reshape(1, -1)
    r, v, g, bonus, kkn, lwf, kdf, bf, lwb, kdb, bb = _rwkv_prep(
        p, row(mu), w0.astype(F32), w2pad, a0.astype(F32), a2pad, g2.astype(_MXU_DTYPE),
        row(k_k), row(k_a), row(r_k), ones_blk)
    o_fw = _rwkv_scan(False, r, kdf, v, lwf, kkn, bf)
    return _merge_seq_parts(_rwkv_scan(True, r, kdb, v, lwb, kkn, bb, (o_fw, g, bonus, row(lnx_g), row(lnx_b))))


PROJ_TM = 256
FFN_TM = 512
EVEN_SPLITS = (A_WIDTH,) * 5 + (B_QW, B_KVW, B_KVW)
ODD_SPLITS = (C_WIDTH, R_IN)


def _mod_tiles(vec9, tm):
    idx = np.concatenate([np.full(CTX_ROWS // tm, BATCH), np.repeat(np.arange(BATCH), SEQ // tm)])
    return vec9[idx][:, None, :]


def kernel(x, c, ctx, c_ctx, ada_w, ada_b, ln_g, ln_b, ev_w_in, ev_w_out, hg_lb, hg_norm_g, attn_sink, ffn_w_gate, ffn_w_up, ffn_w_down, od_w_in, od_w_out, s5_lam_re, s5_lam_im, s5_log_dt, s5_b_re, s5_b_im, s5_c_re, s5_c_im, s5_d, s5_glu_w, rwkv_mu, rwkv_w0, rwkv_w2, rwkv_a0, rwkv_a2, rwkv_g2, rwkv_k_k, rwkv_k_a, rwkv_r_k, rwkv_ln_g, rwkv_ln_b, moe_router_w, moe_router_b, moe_w_gate, moe_w_up, moe_w_down):
    d = D_MODEL
    xs = jnp.concatenate([ctx.reshape(CTX_ROWS, d), x.reshape(BATCH * SEQ, d)], axis=0).astype(F32)
    cond = jnp.concatenate([c, c_ctx[None, :], jnp.zeros((16 - BATCH - 1, d), F32)], axis=0)
    ada = _ada_all(cond, ada_w, ada_b)
    lb_soft = jax.nn.softmax(hg_lb.astype(F32), axis=0)
    lb_all = jnp.cumsum(lb_soft, axis=0) - lb_soft[0:1]

    for layer in range(DEPTH):
        j = layer // 2
        sh1, sc1, gt1, sh2, sc2, gt2 = [ada[layer, :BATCH + 1, n * d:(n + 1) * d] for n in range(6)]
        if layer % 2 == 0:
            q, f_fw, f_bw, i_in, g, aq, ak, av = _mod_matmul(
                xs, _mod_tiles(1.0 + sc1, PROJ_TM), _mod_tiles(sh1, PROJ_TM), ev_w_in[j].astype(BF16), EVEN_SPLITS, PROJ_TM)
            y1 = _hgrn2(q, f_fw, f_bw, i_in, g, lb_all[j], hg_norm_g[j])
            y2 = _window_attention(aq, ak, av, attn_sink[j])
            w_out = ev_w_out[j]
        else:
            u, p_rw = _mod_matmul(
                xs, _mod_tiles(1.0 + sc1, PROJ_TM), _mod_tiles(sh1, PROJ_TM), od_w_in[j].astype(BF16), ODD_SPLITS, PROJ_TM)
            y1 = _s5(u, s5_lam_re[j], s5_lam_im[j], s5_log_dt[j], s5_b_re[j], s5_b_im[j], s5_c_re[j], s5_c_im[j],
                     s5_d[j], s5_glu_w[j])
            y2 = _rwkv7(p_rw, rwkv_mu[j], rwkv_w0[j], rwkv_w2[j], rwkv_a0[j], rwkv_a2[j], rwkv_g2[j],
                        rwkv_k_k[j], rwkv_k_a[j], rwkv_r_k[j], rwkv_ln_g[j], rwkv_ln_b[j])
            w_out = od_w_out[j]
        skip = CTX_ROWS if layer == DEPTH - 1 else 0
        tiles = lambda vec9, tm: _mod_tiles(vec9, tm)[skip // tm:]
        xs = _out_proj_ln(y1, y2, w_out.astype(BF16), xs, tiles(gt1, PROJ_TM), ln_g[layer, 0], ln_b[layer, 0],
                          PROJ_TM, skip_rows=skip)
        scale2, shift2, gate2 = tiles(1.0 + sc2, FFN_TM), tiles(sh2, FFN_TM), tiles(gt2, FFN_TM)
        if layer % 2 == 0:
            xs = _ffn_ln(xs, scale2, shift2, gate2, ffn_w_gate[j][None].astype(BF16),
                         ffn_w_up[j][None].astype(BF16), ffn_w_down[j][None].astype(BF16),
                         ln_g[layer, 1], ln_b[layer, 1], FFN_TM)
        else:
            w_pad = jnp.pad(moe_router_w[j].astype(F32), ((0, 0), (0, 128 - N_EXPERTS)))
            b_pad = jnp.pad(moe_router_b[j].astype(F32), (0, 128 - N_EXPERTS)).reshape(1, 128)
            route, sel, h = _router(xs, scale2, shift2, w_pad, b_pad, FFN_TM)
            pos1, pos2, w1, w2, tile_expert, n_used, pad_lo, pad_hi = _moe_plan(route, sel)
            hs = _moe_dispatch(h, pos1, pos2, pad_lo, pad_hi, n_used)
            y_sorted = _moe_experts(hs, tile_expert, n_used, moe_w_gate[j].astype(BF16),
                                    moe_w_up[j].astype(BF16), moe_w_down[j].astype(BF16))
            xs = _moe_combine_ln(y_sorted, pos1, pos2, w1, w2, xs, tiles(gt2, MOE_COMBINE_TM),
                                 ln_g[layer, 1], ln_b[layer, 1])
    return xs.reshape(BATCH, SEQ, d)
```

```python
import functools
import math

import numpy as np
import jax
import jax.numpy as jnp
from jax import lax
from jax.experimental import pallas as pl
from jax.experimental.pallas import tpu as pltpu

F32 = jnp.float32
BF16 = jnp.bfloat16
HIGHEST = lax.Precision.HIGHEST

D_MODEL = 1024
BATCH = 8
SEQ = 2048
CTX_LEN = 256
DEPTH = 4
GRID_W = 64
ROW_TILE = 256
SEQ_TILES = (CTX_LEN + SEQ) // ROW_TILE
CTX_ROWS = BATCH * CTX_LEN
M_ROWS = BATCH * (CTX_LEN + SEQ)

A_HEADS, A_DK, A_WIDTH = 4, 128, 512
B_HEADS, B_KV_HEADS, B_HEAD_DIM = 8, 2, 64
B_QW, B_KVW = 512, 128
B_BLOCK = 128
ROPE_BASE = 10000.0
C_GROUP, C_GROUPS, C_WIDTH, C_STATE = 16, 32, 512, 64
R_HEADS, R_HEAD, R_WIDTH = 8, 64, 512
R_LORA_W, R_LORA_A, R_LORA_G = 64, 64, 128
R_IN = 3 * R_WIDTH + 2 * R_LORA_W + 2 * R_LORA_A + R_LORA_G
D_FF = 2816
N_EXPERTS = 8
DEEPNORM_ALPHA = (2 * DEPTH) ** 0.25
LN_EPS = 1e-5
RWKV_GN_EPS = 64e-5

VMEM_LIMIT_BYTES = 56 * 1024 * 1024


def _params(*sem):
    return pltpu.CompilerParams(dimension_semantics=sem, vmem_limit_bytes=VMEM_LIMIT_BYTES)


def _dot(a, b, *, trans_a=False, trans_b=False, precision=None):
    dn = (((0 if trans_a else 1,), (1 if trans_b else 0,)), ((), ()))
    return lax.dot_general(a, b, dn, preferred_element_type=F32, precision=precision)


_MXU_DTYPE = BF16


def _bdot(a, b, **kw):
    return _dot(a.astype(_MXU_DTYPE), b.astype(_MXU_DTYPE), **kw)


def _split3(x):
    x1 = x.astype(BF16)
    r1 = x - x1.astype(F32)
    x2 = r1.astype(BF16)
    x3 = (r1 - x2.astype(F32)).astype(BF16)
    return x1, x2, x3


def _dot_sel(sel, x):
    s = sel.astype(BF16)
    x1, x2, x3 = _split3(x)
    return _dot(s, x1) + _dot(s, x2) + _dot(s, x3)


def _dot_by_sel(x, sel):
    s = sel.astype(BF16)
    x1, x2, x3 = _split3(x)
    return _dot(x1, s) + _dot(x2, s) + _dot(x3, s)


def _sigmoid(x):
    return jax.nn.sigmoid(x)


def _silu(x):
    return x * _sigmoid(x)


def _seq_block(b, i):
    return jnp.where(i == 0, b, BATCH + b * (SEQ // ROW_TILE) + i - 1)


def _tile_order(i, reverse):
    if not reverse:
        return i
    return jnp.where(i == 0, 0, SEQ_TILES - i)


def _ada_kernel(cond_ref, w_ref, b_ref, o_ref):
    o_ref[0] = _dot(_silu(cond_ref[...]), w_ref[0], precision=HIGHEST) + b_ref[0]


def _ada_all(cond, ada_w, ada_b):
    rows = cond.shape[0]
    tn = 1536
    return pl.pallas_call(
        _ada_kernel,
        grid=(DEPTH, 6 * D_MODEL // tn),
        in_specs=[pl.BlockSpec((rows, D_MODEL), lambda l, j: (0, 0)),
                  pl.BlockSpec((1, D_MODEL, tn), lambda l, j: (l, 0, j)),
                  pl.BlockSpec((1, 1, tn), lambda l, j: (l, 0, j))],
        out_specs=pl.BlockSpec((1, rows, tn), lambda l, j: (l, 0, j)),
        out_shape=jax.ShapeDtypeStruct((DEPTH, rows, 6 * D_MODEL), F32),
        compiler_params=_params("parallel", "parallel"),
        name="ada",
    )(cond, ada_w, ada_b.reshape(DEPTH, 1, 6 * D_MODEL))


def _mod_matmul_kernel(splits, rotary, x_ref, sc_ref, sh_ref, w_ref, *refs):
    if rotary:
        cos, sa, sb = refs[0][...], refs[1][...], refs[2][...]
        refs = refs[3:]
    h = (x_ref[...] * sc_ref[0] + sh_ref[0]).astype(BF16)
    off = 0
    for n, (o_ref, width) in enumerate(zip(refs, splits)):
        y = _dot(h, w_ref[:, off:off + width])
        if n in rotary:
            for j in range(width // 128):
                x = y[:, j * 128:(j + 1) * 128]
                o_ref[:, j * 128:(j + 1) * 128] = x * cos + pltpu.roll(x, 96, 1) * sa + pltpu.roll(x, 32, 1) * sb
        else:
            o_ref[...] = y
        off += width


def _mod_matmul(x, scale_t, shift_t, w_bf16, splits, tm, rotary=(), rope_tables=None):
    m, d = x.shape
    n = w_bf16.shape[1]
    assert sum(splits) == n and m % tm == 0
    ins = [x, scale_t, shift_t, w_bf16]
    specs = [pl.BlockSpec((tm, d), lambda i: (i, 0)),
             pl.BlockSpec((1, 1, d), lambda i: (i, 0, 0)),
             pl.BlockSpec((1, 1, d), lambda i: (i, 0, 0)),
             pl.BlockSpec((d, n), lambda i: (0, 0))]
    if rotary:
        assert tm == ROW_TILE
        ctx_tiles, lat_tiles = CTX_ROWS // tm, SEQ // tm
        tab = pl.BlockSpec((tm, 128), lambda i: (jnp.where(i < ctx_tiles, 0, (i - ctx_tiles) % lat_tiles + 1), 0))
        ins += list(rope_tables)
        specs += [tab, tab, tab]
    return pl.pallas_call(
        functools.partial(_mod_matmul_kernel, splits, tuple(rotary)),
        grid=(m // tm,),
        in_specs=specs,
        out_specs=[pl.BlockSpec((tm, w), lambda i: (i, 0)) for w in splits],
        out_shape=[jax.ShapeDtypeStruct((m, w), F32) for w in splits],
        compiler_params=_params("parallel"),
        name="mod_matmul",
    )(*ins)


def _layer_norm_rows(z, g, b):
    mu = jnp.mean(z, axis=-1, keepdims=True)
    zc = z - mu
    var = jnp.mean(zc * zc, axis=-1, keepdims=True)
    return zc * lax.rsqrt(var + LN_EPS) * g + b


def _out_proj_kernel(y1_ref, y2_ref, w_ref, x_ref, gt_ref, g_ref, b_ref, o_ref):
    k1 = y1_ref.shape[1]
    proj = _bdot(y1_ref[...], w_ref[:k1, :]) + _bdot(y2_ref[...], w_ref[k1:, :])
    z = DEEPNORM_ALPHA * x_ref[...] + gt_ref[0] * proj
    o_ref[...] = _layer_norm_rows(z, g_ref[...], b_ref[...])


def _out_proj_ln(y1, y2, w_bf16, x, gate_t, ln_g, ln_b, tm, skip_rows=0):
    m, d = x.shape
    k1, k2 = y1.shape[1], y2.shape[1]
    off = skip_rows // tm
    return pl.pallas_call(
        _out_proj_kernel,
        grid=((m - skip_rows) // tm,),
        in_specs=[pl.BlockSpec((tm, k1), lambda i: (i + off, 0)),
                  pl.BlockSpec((tm, k2), lambda i: (i + off, 0)),
                  pl.BlockSpec((k1 + k2, d), lambda i: (0, 0)),
                  pl.BlockSpec((tm, d), lambda i: (i + off, 0)),
                  pl.BlockSpec((1, 1, d), lambda i: (i, 0, 0)),
                  pl.BlockSpec((1, d), lambda i: (0, 0)),
                  pl.BlockSpec((1, d), lambda i: (0, 0))],
        out_specs=pl.BlockSpec((tm, d), lambda i: (i, 0)),
        out_shape=jax.ShapeDtypeStruct((m - skip_rows, d), F32),
        compiler_params=_params("parallel"),
        name="out_proj_ln",
    )(y1, y2, w_bf16, x, gate_t, ln_g.reshape(1, d), ln_b.reshape(1, d))


FF_CHUNK = 256


def _swiglu_rows(h, wg_ref, wu_ref, wd_ref):
    ff = wg_ref.shape[2]
    acc = jnp.zeros((h.shape[0], wd_ref.shape[2]), F32)
    pending = None
    for c in range(ff // FF_CHUNK):
        cols = slice(c * FF_CHUNK, (c + 1) * FF_CHUNK)
        g, u = _dot(h, wg_ref[0, :, cols]), _dot(h, wu_ref[0, :, cols])
        if pending is not None:
            acc = acc + _dot(pending[0], wd_ref[0, pending[1], :])
        pending = ((_silu(g) * u).astype(BF16), cols)
    return acc + _dot(pending[0], wd_ref[0, pending[1], :])


def _ffn_kernel(x_ref, sc_ref, sh_ref, gt_ref, wg_ref, wu_ref, wd_ref, g_ref, b_ref, o_ref):
    x = x_ref[...]
    f = _swiglu_rows((x * sc_ref[0] + sh_ref[0]).astype(BF16), wg_ref, wu_ref, wd_ref)
    o_ref[...] = _layer_norm_rows(DEEPNORM_ALPHA * x + gt_ref[0] * f, g_ref[...], b_ref[...])


def _ffn_ln(x, scale_t, shift_t, gate_t, wg, wu, wd, ln_g, ln_b, tm):
    m, d = x.shape
    ff = wg.shape[2]
    return pl.pallas_call(
        _ffn_kernel,
        grid=(m // tm,),
        in_specs=[pl.BlockSpec((tm, d), lambda i: (i, 0)),
                  pl.BlockSpec((1, 1, d), lambda i: (i, 0, 0)),
                  pl.BlockSpec((1, 1, d), lambda i: (i, 0, 0)),
                  pl.BlockSpec((1, 1, d), lambda i: (i, 0, 0)),
                  pl.BlockSpec((1, d, ff), lambda i: (0, 0, 0)),
                  pl.BlockSpec((1, d, ff), lambda i: (0, 0, 0)),
                  pl.BlockSpec((1, ff, d), lambda i: (0, 0, 0)),
                  pl.BlockSpec((1, d), lambda i: (0, 0)),
                  pl.BlockSpec((1, d), lambda i: (0, 0))],
        out_specs=pl.BlockSpec((tm, d), lambda i: (i, 0)),
        out_shape=jax.ShapeDtypeStruct((m, d), F32),
        compiler_params=_params("parallel"),
        name="ffn_ln",
    )(x, scale_t, shift_t, gate_t, wg, wu, wd, ln_g.reshape(1, d), ln_b.reshape(1, d))


def _router_kernel(x_ref, sc_ref, sh_ref, w_ref, b_ref, o_ref, sel_ref, h_ref):
    h = x_ref[...] * sc_ref[0] + sh_ref[0]
    h_ref[...] = h
    logits = _dot(h, w_ref[...], precision=HIGHEST) + b_ref[...]
    lane = lax.broadcasted_iota(jnp.int32, logits.shape, 1)
    neg = jnp.float32(-jnp.inf)
    logits = jnp.where(lane < N_EXPERTS, logits, neg)
    v1 = jnp.max(logits, axis=1, keepdims=True)
    i1 = jnp.min(jnp.where(logits == v1, lane, 128), axis=1, keepdims=True)
    rest = jnp.where(lane == i1, neg, logits)
    v2 = jnp.max(rest, axis=1, keepdims=True)
    i2 = jnp.min(jnp.where(rest == v2, lane, 128), axis=1, keepdims=True)
    e2 = jnp.exp(v2 - v1)
    p1 = 1.0 / (1.0 + e2)
    p2 = e2 / (1.0 + e2)
    o_ref[...] = jnp.where(lane == i1, p1, 0.0) + jnp.where(lane == i2, p2, 0.0)
    sel_ref[...] = ((lane == i1) | (lane == i2)).astype(F32)


def _router(x, scale_t, shift_t, w_pad, b_pad, tm):
    m, d = x.shape
    lanes = pl.BlockSpec((tm, 128), lambda i: (i, 0))
    return pl.pallas_call(
        _router_kernel,
        grid=(m // tm,),
        in_specs=[pl.BlockSpec((tm, d), lambda i: (i, 0)),
                  pl.BlockSpec((1, 1, d), lambda i: (i, 0, 0)),
                  pl.BlockSpec((1, 1, d), lambda i: (i, 0, 0)),
                  pl.BlockSpec((d, 128), lambda i: (0, 0)),
                  pl.BlockSpec((1, 128), lambda i: (0, 0))],
        out_specs=[lanes, lanes, pl.BlockSpec((tm, d), lambda i: (i, 0))],
        out_shape=[jax.ShapeDtypeStruct((m, 128), F32), jax.ShapeDtypeStruct((m, 128), F32),
                   jax.ShapeDtypeStruct((m, d), F32)],
        compiler_params=_params("parallel"),
        name="router",
    )(x, scale_t, shift_t, w_pad, b_pad)


MOE_TM = 512
MOE_COMBINE_TM = 256
MOE_DISPATCH_TM = 256


def _moe_rows(m):
    return 2 * m + N_EXPERTS * MOE_TM


def _moe_plan(route, sel):
    moe_rows = _moe_rows(route.shape[0])
    sel8 = sel[:, :N_EXPERTS].astype(jnp.int32)
    counts = jnp.sum(sel8, axis=0)
    rank = jnp.cumsum(sel8, axis=0) - sel8
    padded = ((counts + MOE_TM - 1) // MOE_TM) * MOE_TM
    ends = jnp.cumsum(padded)
    starts = ends - padded
    pos = jnp.where(sel8 > 0, starts[None, :] + rank, moe_rows)
    pos1 = jnp.min(pos, axis=1)
    pos2 = jnp.min(jnp.where(pos == pos1[:, None], moe_rows, pos), axis=1)
    route8 = route[:, :N_EXPERTS]
    w1 = jnp.sum(jnp.where(pos == pos1[:, None], route8, 0.0), axis=1, keepdims=True)
    w2 = jnp.sum(jnp.where(pos == pos2[:, None], route8, 0.0), axis=1, keepdims=True)
    tile_start = jnp.arange(moe_rows // MOE_TM, dtype=jnp.int32) * MOE_TM
    tile_expert = jnp.minimum(jnp.sum((tile_start[:, None] >= ends[None, :]).astype(jnp.int32), axis=1),
                              N_EXPERTS - 1).astype(jnp.int32)
    n_used = (ends[-1] // MOE_TM).astype(jnp.int32).reshape(1)
    pad_lo, pad_hi = (starts + counts).astype(jnp.int32), ends.astype(jnp.int32)
    return pos1.astype(jnp.int32), pos2.astype(jnp.int32), w1, w2, tile_expert, n_used, pad_lo, pad_hi


def _moe_dispatch_kernel(p1_ref, p2_ref, lo_ref, hi_ref, nused_ref, h_ref, hs_hbm, zero_ref, sem, zsem):
    i = pl.program_id(0)
    base = i * MOE_DISPATCH_TM

    def row_copy(r, dst_row):
        return pltpu.make_async_copy(h_ref.at[pl.ds(r, 1), :], hs_hbm.at[pl.ds(dst_row, 1), :], sem)

    def start(r, carry):
        row_copy(r, p1_ref[base + r]).start(priority=0)
        row_copy(r, p2_ref[base + r]).start(priority=1)
        return carry

    def wait(r, carry):
        row_copy(0, 0).wait()
        row_copy(0, 0).wait()
        return carry

    lax.fori_loop(0, MOE_DISPATCH_TM, start, 0, unroll=8)
    lax.fori_loop(0, MOE_DISPATCH_TM, wait, 0, unroll=8)

    @pl.when(i == pl.num_programs(0) - 1)
    def _():
        zero_ref[...] = jnp.zeros_like(zero_ref)

        def zero_copy(dst_row):
            return pltpu.make_async_copy(zero_ref.at[pl.ds(0, 1), :], hs_hbm.at[pl.ds(dst_row, 1), :], zsem)

        for e in range(N_EXPERTS):
            lo, hi = lo_ref[e], hi_ref[e]

            def zstart(r, carry):
                zero_copy(r).start()
                return carry

            def zwait(r, carry):
                zero_copy(0).wait()
                return carry

            lax.fori_loop(lo, hi, zstart, 0)
            lax.fori_loop(lo, hi, zwait, 0)

        def tile_copy(t):
            return pltpu.make_async_copy(zero_ref, hs_hbm.at[pl.ds(t * MOE_TM, MOE_TM), :], zsem)

        def tstart(t, carry):
            tile_copy(t).start()
            return carry

        def twait(t, carry):
            tile_copy(0).wait()
            return carry

        n_tiles = hs_hbm.shape[0] // MOE_TM
        lax.fori_loop(nused_ref[0], n_tiles, tstart, 0)
        lax.fori_loop(nused_ref[0], n_tiles, twait, 0)


def _moe_dispatch(h, pos1, pos2, pad_lo, pad_hi, n_used):
    m, d = h.shape
    grid_spec = pltpu.PrefetchScalarGridSpec(
        num_scalar_prefetch=5,
        grid=(m // MOE_DISPATCH_TM,),
        in_specs=[pl.BlockSpec((MOE_DISPATCH_TM, d), lambda i, *_: (i, 0))],
        out_specs=pl.BlockSpec(memory_space=pl.ANY),
        scratch_shapes=[pltpu.VMEM((MOE_TM, d), F32), pltpu.SemaphoreType.DMA(()), pltpu.SemaphoreType.DMA(())],
    )
    return pl.pallas_call(
        _moe_dispatch_kernel,
        grid_spec=grid_spec,
        out_shape=jax.ShapeDtypeStruct((_moe_rows(m), d), F32),
        compiler_params=_params("arbitrary"),
        name="moe_dispatch",
    )(pos1, pos2, pad_lo, pad_hi, n_used, h)


def _moe_expert_kernel(texp_ref, nused_ref, hs_ref, wg_ref, wu_ref, wd_ref, y_ref):
    t = pl.program_id(0)

    @pl.when(t < nused_ref[0])
    def _():
        y_ref[...] = _swiglu_rows(hs_ref[...].astype(BF16), wg_ref, wu_ref, wd_ref)

    @pl.when(t >= nused_ref[0])
    def _():
        y_ref[...] = jnp.zeros_like(y_ref)


def _moe_experts(hs, tile_expert, n_used, wg, wu, wd):
    moe_rows, d = hs.shape
    ff = wg.shape[2]
    grid_spec = pltpu.PrefetchScalarGridSpec(
        num_scalar_prefetch=2,
        grid=(moe_rows // MOE_TM,),
        in_specs=[pl.BlockSpec((MOE_TM, d), lambda t, te, nu: (jnp.minimum(t, nu[0] - 1), 0)),
                  pl.BlockSpec((1, d, ff), lambda t, te, nu: (te[t], 0, 0)),
                  pl.BlockSpec((1, d, ff), lambda t, te, nu: (te[t], 0, 0)),
                  pl.BlockSpec((1, ff, d), lambda t, te, nu: (te[t], 0, 0))],
        out_specs=pl.BlockSpec((MOE_TM, d), lambda t, te, nu: (t, 0)),
    )
    return pl.pallas_call(
        _moe_expert_kernel,
        grid_spec=grid_spec,
        out_shape=jax.ShapeDtypeStruct((moe_rows, d), F32),
        compiler_params=_params("arbitrary"),
        name="moe_experts",
    )(tile_expert, n_used, hs, wg, wu, wd)


def _moe_combine_kernel(p1_ref, p2_ref, y_hbm, x_ref, w1_ref, w2_ref, gt_ref, g_ref, b_ref, o_ref, buf1, buf2, sem):
    base = pl.program_id(0) * MOE_COMBINE_TM

    def row_copy(buf, k, r, src_row):
        return pltpu.make_async_copy(y_hbm.at[pl.ds(src_row, 1), :], buf.at[pl.ds(r, 1), :], sem.at[k])

    def start(r, carry):
        row_copy(buf1, 0, r, p1_ref[base + r]).start(priority=0)
        row_copy(buf2, 1, r, p2_ref[base + r]).start(priority=1)
        return carry

    def wait(r, carry):
        row_copy(buf1, 0, 0, 0).wait()
        row_copy(buf2, 1, 0, 0).wait()
        return carry

    lax.fori_loop(0, MOE_COMBINE_TM, start, 0, unroll=8)
    lax.fori_loop(0, MOE_COMBINE_TM, wait, 0, unroll=8)
    f = w1_ref[...] * buf1[...] + w2_ref[...] * buf2[...]
    z = DEEPNORM_ALPHA * x_ref[...] + gt_ref[0] * f
    o_ref[...] = _layer_norm_rows(z, g_ref[...], b_ref[...])


def _moe_combine_ln(y_sorted, pos1, pos2, w1, w2, x, gate_t, ln_g, ln_b):
    m, d = x.shape
    tm = MOE_COMBINE_TM
    grid_spec = pltpu.PrefetchScalarGridSpec(
        num_scalar_prefetch=2,
        grid=(m // tm,),
        in_specs=[pl.BlockSpec(memory_space=pl.ANY),
                  pl.BlockSpec((tm, d), lambda i, p1, p2: (i, 0)),
                  pl.BlockSpec((tm, 1), lambda i, p1, p2: (i, 0)),
                  pl.BlockSpec((tm, 1), lambda i, p1, p2: (i, 0)),
                  pl.BlockSpec((1, 1, d), lambda i, p1, p2: (i, 0, 0)),
                  pl.BlockSpec((1, d), lambda i, p1, p2: (0, 0)),
                  pl.BlockSpec((1, d), lambda i, p1, p2: (0, 0))],
        out_specs=pl.BlockSpec((tm, d), lambda i, p1, p2: (i, 0)),
        scratch_shapes=[pltpu.VMEM((tm, d), F32), pltpu.VMEM((tm, d), F32), pltpu.SemaphoreType.DMA((2,))],
    )
    return pl.pallas_call(
        _moe_combine_kernel,
        grid_spec=grid_spec,
        out_shape=jax.ShapeDtypeStruct((m, d), F32),
        compiler_params=_params("arbitrary"),
        name="moe_combine_ln",
    )(pos1, pos2, y_sorted, x, w1, w2, gate_t, ln_g.reshape(1, d), ln_b.reshape(1, d))


HGRN_CHUNK = 16
HGRN_SUPER = 64


def _hgrn_kernel(reverse, finish, *refs):
    if finish:
        q_ref, f_ref, i_ref, lb_ref, of_ref, g_ref, ng_ref, o_ref, st_ref = refs
    else:
        q_ref, f_ref, i_ref, lb_ref, o_ref, st_ref = refs
    C, S = HGRN_CHUNK, HGRN_SUPER
    nsub = S // C
    n_super = ROW_TILE // S

    @pl.when(pl.program_id(1) == 0)
    def _():
        st_ref[...] = jnp.zeros_like(st_ref)

    row = lax.broadcasted_iota(jnp.int32, (S, S), 0)
    col = lax.broadcasted_iota(jnp.int32, (S, S), 1)
    blk_r, blk_c = jnp.zeros_like(row), jnp.zeros_like(col)
    for j in range(1, nsub):
        blk_r = blk_r + (row >= j * C).astype(jnp.int32)
        blk_c = blk_c + (col >= j * C).astype(jnp.int32)
    H = C // 2
    hi_r, hi_c = (row - blk_r * C) >= H, (col - blk_c * C) >= H
    col_in = col - blk_c * C - jnp.where(hi_c, H, 0)
    ordered = (row <= col) if reverse else (row >= col)
    same_chunk = blk_r == blk_c
    tri = (same_chunk & ordered).astype(F32)
    keep = same_chunk & (hi_r == hi_c) & ordered
    cross = same_chunk & ((jnp.logical_not(hi_r) & hi_c) if reverse else (hi_r & jnp.logical_not(hi_c)))
    heads = range(A_HEADS)
    sls = [slice(h * A_DK, (h + 1) * A_DK) for h in heads]
    chunk_rows = [slice(c * C, (c + 1) * C) for c in range(nsub)]
    width = A_DK

    def per_chunk_row(t, offset):
        return jnp.concatenate(
            [jnp.broadcast_to(t[c * C + offset:c * C + offset + 1], (C, width)) for c in range(nsub)], axis=0)

    def per_half_row(t, offset):
        return jnp.concatenate(
            [jnp.broadcast_to(t[g * H + offset:g * H + offset + 1], (H, width)) for g in range(S // H)], axis=0)

    def superchunk(si, carry):
        sc = (n_super - 1 - si) if reverse else si
        rows = pl.ds(pl.multiple_of(sc * S, S), S)
        last = 0 if reverse else C - 1
        vs, qes, kts, qxs, kxs, atts, bs = [], [], [], [], [], [], []
        for sl in sls:
            lb = lb_ref[:, sl]
            q = _silu(q_ref[rows, sl]) * A_DK ** -0.5
            f = lb + (1.0 - lb) * _sigmoid(f_ref[rows, sl])
            k = 1.0 - f
            b = _dot_sel(tri, jnp.log(f))
            b_tot = per_chunk_row(b, last)
            b_mid = per_chunk_row(b, H if reverse else H - 1)
            att = jnp.zeros((S, S), F32)
            for s in range(H):
                tmp = q * jnp.exp(b - per_half_row(b, s)) * per_half_row(k, s)
                att = jnp.where(col_in == s, jnp.sum(tmp, axis=1, keepdims=True), att)
            vs.append(i_ref[rows, sl])
            qes.append(q * jnp.exp(b))
            kts.append(k * jnp.exp(b_tot - b))
            qxs.append(q * jnp.exp(b - b_mid))
            kxs.append(k * jnp.exp(b_mid - b))
            atts.append(att)
            bs.append(b)
        xatts = [_bdot(qx, kx, trans_b=True) for qx, kx in zip(qxs, kxs)]
        intra = [_bdot(jnp.where(keep, att, 0.0) + jnp.where(cross, xatt, 0.0), v)
                 for v, att, xatt in zip(vs, atts, xatts)]
        upds = [[_bdot(v[cr], kt[cr], trans_a=True) for cr in chunk_rows] for v, kt in zip(vs, kts)]
        sts = [st_ref[h] for h in heads]
        inter = [[None] * nsub for _ in heads]
        for c in (range(nsub - 1, -1, -1) if reverse else range(nsub)):
            cr = chunk_rows[c]
            for h in heads:
                inter[h][c] = _bdot(qes[h][cr], sts[h], trans_b=True)
                sts[h] = sts[h] * jnp.exp(bs[h][c * C + last:c * C + last + 1]) + upds[h][c]
        outs = []
        for h in heads:
            st_ref[h] = sts[h]
            o = intra[h] + jnp.concatenate(inter[h], axis=0)
            if finish:
                o = o + of_ref[rows, sls[h]]
                o = o * lax.rsqrt(jnp.mean(o * o, axis=1, keepdims=True) + 1e-6)
                o = o * ng_ref[:, sls[h]] * _silu(g_ref[rows, sls[h]])
            outs.append(o)
        for h in heads:
            o_ref[rows, sls[h]] = outs[h]
        return carry

    lax.fori_loop(0, n_super, superchunk, 0)


def _hgrn_pass(reverse, q, f, i, lb, finish_args=None):
    m, w = q.shape
    seq_spec = pl.BlockSpec((ROW_TILE, w), lambda b, t: (_seq_block(b, _tile_order(t, reverse)), 0))
    vec_spec = pl.BlockSpec((1, w), lambda b, t: (0, 0))
    ins = [q, f, i, lb]
    specs = [seq_spec, seq_spec, seq_spec, vec_spec]
    if finish_args is not None:
        of, g, ng = finish_args
        ins += [of, g, ng]
        specs += [seq_spec, seq_spec, vec_spec]
    return pl.pallas_call(
        functools.partial(_hgrn_kernel, reverse, finish_args is not None),
        grid=(BATCH, SEQ_TILES),
        in_specs=specs,
        out_specs=seq_spec,
        out_shape=jax.ShapeDtypeStruct((m, w), F32),
        scratch_shapes=[pltpu.VMEM((A_HEADS, A_DK, A_DK), F32)],
        compiler_params=_params("parallel", "arbitrary"),
        name="hgrn_bwd" if reverse else "hgrn_fwd",
    )(*ins)


def _rope_tables():
    rows = SEQ // GRID_W
    row = jnp.repeat(jnp.arange(rows, dtype=F32), GRID_W)
    colp = jnp.tile(jnp.arange(GRID_W, dtype=F32), rows)
    n_freq = B_HEAD_DIM // 4
    inv = ROPE_BASE ** (-jnp.arange(n_freq, dtype=F32) / n_freq)
    ang = jnp.concatenate([row[:, None] * inv, colp[:, None] * inv], axis=-1)
    cos, sin = jnp.cos(ang), jnp.sin(ang)
    zero = jnp.zeros_like(sin)
    cos_l = jnp.tile(cos, (1, 4))
    sa_l = jnp.tile(jnp.concatenate([-sin, zero], axis=-1), (1, 2))
    sb_l = jnp.tile(jnp.concatenate([zero, sin], axis=-1), (1, 2))
    pad = lambda t, v: jnp.concatenate([jnp.full((CTX_LEN, 128), v, F32), t], axis=0)
    return pad(cos_l, 1.0), pad(sa_l, 0.0), pad(sb_l, 0.0)


def _attend(q_ref, o_ref, sink_ref, k, v, mask):
    tq = q_ref.shape[0]
    left = lax.broadcasted_iota(jnp.int32, (tq, 128), 1) < B_HEAD_DIM
    lk = lax.broadcasted_iota(jnp.int32, k.shape, 1) < B_HEAD_DIM
    neg = jnp.float32(-jnp.inf)
    scale = B_HEAD_DIM ** -0.5
    kr, vr = pltpu.roll(k, 64, 1), pltpu.roll(v, 64, 1)
    kds = [jnp.where(lk, k, kr).astype(_MXU_DTYPE), jnp.where(lk, kr, k).astype(_MXU_DTYPE)]
    vds = [jnp.where(lk, v, vr), jnp.where(lk, vr, v)]
    vsel = [[jnp.where(lk if half == 0 else jnp.logical_not(lk), vd, 0.0).astype(_MXU_DTYPE) for half in range(2)]
            for vd in vds]
    pairs = B_HEADS // 2
    heads = [(p, half) for p in range(pairs) for half in range(2)]
    group = lambda p: p // (pairs // B_KV_HEADS)
    qms = [jnp.where(left if half == 0 else jnp.logical_not(left), q_ref[:, p * 128:(p + 1) * 128] * scale, 0.0)
           .astype(_MXU_DTYPE) for p, half in heads]
    scores = [_dot(qm, kds[group(p)], trans_b=True) for (p, half), qm in zip(heads, qms)]
    if mask is not None:
        scores = [jnp.where(mask, s, neg) for s in scores]
    sinks = [sink_ref[2 * p + half:2 * p + half + 1, 0:1] for p, half in heads]
    mxs = [jnp.maximum(sink, jnp.max(s, axis=1, keepdims=True)) for s, sink in zip(scores, sinks)]
    es = [jnp.exp(s - mx) for s, mx in zip(scores, mxs)]
    denoms = [jnp.exp(sink - mx) + jnp.sum(e, axis=1, keepdims=True) for e, mx, sink in zip(es, mxs, sinks)]
    accs = [_dot(e.astype(_MXU_DTYPE), vsel[group(p)][half]) for (p, half), e in zip(heads, es)]
    for p in range(pairs):
        o_ref[:, p * 128:(p + 1) * 128] = accs[2 * p] / denoms[2 * p] + accs[2 * p + 1] / denoms[2 * p + 1]


ATTN_LATENT_BLOCKS = SEQ // B_BLOCK
ATTN_CTX_BLOCKS = CTX_LEN // B_BLOCK


def _attn_kernel(q_ref, kp_ref, kc_ref, kn_ref, vp_ref, vc_ref, vn_ref, kx_ref, vx_ref, sink_ref, o_ref):
    n = pl.program_id(1)
    nb = ATTN_LATENT_BLOCKS

    @pl.when(n < nb)
    def _():
        row = lax.broadcasted_iota(jnp.int32, (B_BLOCK, B_BLOCK), 0)
        col = lax.broadcasted_iota(jnp.int32, (B_BLOCK, B_BLOCK), 1)
        mask = jnp.concatenate([(col >= row) & (n > 0), jnp.ones((B_BLOCK, B_BLOCK), jnp.bool_),
                                (col <= row) & (n < nb - 1), jnp.ones((B_BLOCK, CTX_LEN), jnp.bool_)], axis=1)
        k = jnp.concatenate([kp_ref[...], kc_ref[...], kn_ref[...], kx_ref[...]], axis=0)
        v = jnp.concatenate([vp_ref[...], vc_ref[...], vn_ref[...], vx_ref[...]], axis=0)
        _attend(q_ref, o_ref, sink_ref, k, v, mask)

    @pl.when(n >= nb)
    def _():
        _attend(q_ref, o_ref, sink_ref, kx_ref[...], vx_ref[...], None)


def _attention(q, k, v, sink_rows):
    m = q.shape[0]
    nb, nc = ATTN_LATENT_BLOCKS, ATTN_CTX_BLOCKS
    base = CTX_ROWS // B_BLOCK

    def q_block(b, n):
        return jnp.where(n < nb, base + b * nb + n, b * nc + n - nb)

    qspec = pl.BlockSpec((B_BLOCK, B_QW), lambda b, n: (q_block(b, n), 0))

    def kv(shift):
        return pl.BlockSpec((B_BLOCK, B_KVW), lambda b, n: (base + b * nb + jnp.clip(n + shift, 0, nb - 1), 0))

    ctx_kv = pl.BlockSpec((CTX_LEN, B_KVW), lambda b, n: (b, 0))
    sink_spec = pl.BlockSpec((B_HEADS, 128), lambda b, n: (0, 0))
    return pl.pallas_call(
        _attn_kernel,
        grid=(BATCH, nb + nc),
        in_specs=[qspec, kv(-1), kv(0), kv(1), kv(-1), kv(0), kv(1), ctx_kv, ctx_kv, sink_spec],
        out_specs=qspec,
        out_shape=jax.ShapeDtypeStruct((m, B_QW), F32),
        compiler_params=_params("parallel", "parallel"),
        name="attention",
    )(q, k, k, k, v, v, v, k, v, sink_rows)


def _window_attention(q, k, v, sink):
    sink_rows = jnp.broadcast_to(sink.astype(F32)[:, None], (B_HEADS, 128))
    return _attention(q, k, v, sink_rows)


def _hgrn2(q, f_fw, f_bw, i, g, lb, norm_g):
    lb = lb.reshape(1, A_WIDTH)
    o_fw = _hgrn_pass(False, q, f_fw, i, lb)
    return _hgrn_pass(True, q, f_bw, i, lb, (o_fw, g, norm_g.reshape(1, A_WIDTH)))


S5_STEPS = 64
S5_ROWS = S5_STEPS * BATCH
S5_TILES = (CTX_LEN + SEQ) // S5_STEPS
S5_CTX_TILES = CTX_LEN // S5_STEPS
S5_BLOCKS = 4
S5_BLOCK_IN = C_WIDTH // S5_BLOCKS
S5_BLOCK_STATE = C_GROUPS * C_STATE // S5_BLOCKS


def _s5_kernel(reverse, finish, *refs):
    if finish:
        u_ref, a_ref, wb_ref, wc_ref, yf_ref, glu_ref, y_ref, x_ref, st_ref = refs
    else:
        u_ref, a_ref, wb_ref, wc_ref, d_ref, y_ref, x_ref, st_ref = refs
    ns = S5_BLOCK_STATE

    @pl.when(pl.program_id(0) == 0)
    def _():
        st_ref[...] = jnp.zeros_like(st_ref)

    for k in range(S5_BLOCKS):
        x_ref[:, 2 * ns * k:2 * ns * (k + 1)] = _bdot(u_ref[:, S5_BLOCK_IN * k:S5_BLOCK_IN * (k + 1)], wb_ref[k])

    def step(tt, carry):
        t = (S5_STEPS - 1 - tt) if reverse else tt
        rows = pl.ds(pl.multiple_of(t * BATCH, BATCH), BATCH)
        for k in range(S5_BLOCKS):
            re = slice(2 * ns * k, 2 * ns * k + ns)
            im = slice(2 * ns * k + ns, 2 * ns * (k + 1))
            ar, ai = a_ref[:, re], a_ref[:, im]
            sr, si = st_ref[:, re], st_ref[:, im]
            nr = ar * sr - ai * si + x_ref[rows, re]
            ni = ar * si + ai * sr + x_ref[rows, im]
            st_ref[:, re] = nr
            st_ref[:, im] = ni
            x_ref[rows, re] = nr
            x_ref[rows, im] = ni
        return carry

    lax.fori_loop(0, S5_STEPS, step, 0)

    for k in range(S5_BLOCKS):
        cols = slice(S5_BLOCK_IN * k, S5_BLOCK_IN * (k + 1))
        y = _bdot(x_ref[:, 2 * ns * k:2 * ns * (k + 1)], wc_ref[k])
        if finish:
            y_ref[:, cols] = y + yf_ref[:, cols]
        else:
            y_ref[:, cols] = y + d_ref[:, cols] * u_ref[:, cols]
    if finish:
        y = jax.nn.gelu(y_ref[...])
        y_ref[...] = y * _sigmoid(_bdot(y, glu_ref[...]))


def _s5_tile_order(i, reverse):
    if not reverse:
        return i
    return jnp.where(i < S5_CTX_TILES, S5_CTX_TILES - 1 - i, S5_TILES + S5_CTX_TILES - 1 - i)


def _s5_pass(reverse, u_tm, acoef, wb, wc, extra):
    m, w = u_tm.shape
    nstate = 2 * C_GROUPS * C_STATE
    row_spec = pl.BlockSpec((S5_ROWS, w), lambda i: (_s5_tile_order(i, reverse), 0))
    full = lambda a: pl.BlockSpec(a.shape, lambda i: (0,) * a.ndim)
    finish = reverse
    if finish:
        yf, glu_w = extra
        ins, specs = [u_tm, acoef, wb, wc, yf, glu_w], [row_spec, full(acoef), full(wb), full(wc), row_spec, full(glu_w)]
    else:
        (dskip,) = extra
        ins, specs = [u_tm, acoef, wb, wc, dskip], [row_spec, full(acoef), full(wb), full(wc), full(dskip)]
    return pl.pallas_call(
        functools.partial(_s5_kernel, reverse, finish),
        grid=(S5_TILES,),
        in_specs=specs,
        out_specs=row_spec,
        out_shape=jax.ShapeDtypeStruct((m, w), F32),
        scratch_shapes=[pltpu.VMEM((S5_ROWS, nstate), F32), pltpu.VMEM((BATCH, nstate), F32)],
        compiler_params=_params("arbitrary"),
        name="s5_bwd" if reverse else "s5_fwd",
    )(*ins)


def _s5_discretize(lam_re, lam_im, log_dt, b_re, b_im):
    lam_re = jnp.minimum(lam_re.astype(F32), -1e-4)
    lam_im = lam_im.astype(F32)
    dt = jnp.exp(log_dt.astype(F32))[:, None]
    mag = jnp.exp(lam_re * dt)
    ab_re, ab_im = mag * jnp.cos(lam_im * dt), mag * jnp.sin(lam_im * dt)
    den = lam_re ** 2 + lam_im ** 2
    nr = ab_re - 1.0
    co_re = (nr * lam_re + ab_im * lam_im) / den
    co_im = (ab_im * lam_re - nr * lam_im) / den
    bb_re = co_re[..., None] * b_re - co_im[..., None] * b_im
    bb_im = co_re[..., None] * b_im + co_im[..., None] * b_re
    return ab_re, ab_im, bb_re, bb_im


def _s5_tables(lam_re, lam_im, log_dt, b_re, b_im, c_re, c_im):
    eye = jnp.eye(C_GROUPS // S5_BLOCKS, dtype=F32)
    gb = C_GROUPS // S5_BLOCKS

    def in_map(bb):
        return jnp.einsum('kgph,gG->kghGp', bb.reshape(S5_BLOCKS, gb, C_STATE, C_GROUP), eye).reshape(
            S5_BLOCKS, S5_BLOCK_IN, S5_BLOCK_STATE)

    def out_map(cc):
        return jnp.einsum('kghp,gG->kgpGh', cc.reshape(S5_BLOCKS, gb, C_GROUP, C_STATE), eye).reshape(
            S5_BLOCKS, S5_BLOCK_STATE, S5_BLOCK_IN)

    wc = jnp.concatenate([out_map(c_re.astype(F32)), -out_map(c_im.astype(F32))], axis=1).astype(_MXU_DTYPE)
    tables = []
    for d in range(2):
        ab_re, ab_im, bb_re, bb_im = _s5_discretize(lam_re[d], lam_im[d], log_dt[d], b_re.astype(F32), b_im.astype(F32))
        a = jnp.concatenate([ab_re.reshape(S5_BLOCKS, S5_BLOCK_STATE), ab_im.reshape(S5_BLOCKS, S5_BLOCK_STATE)], axis=1)
        acoef = jnp.broadcast_to(a.reshape(1, -1), (BATCH, 2 * C_GROUPS * C_STATE))
        wb = jnp.concatenate([in_map(bb_re), in_map(bb_im)], axis=2).astype(_MXU_DTYPE)
        tables.append((acoef, wb))
    return tables, wc


def _to_time_major(y):
    w = y.shape[1]
    c = y[:CTX_ROWS].reshape(BATCH, CTX_LEN, w).transpose(1, 0, 2).reshape(CTX_ROWS, w)
    l = y[CTX_ROWS:].reshape(BATCH, SEQ, w).transpose(1, 0, 2).reshape(BATCH * SEQ, w)
    return jnp.concatenate([c, l], axis=0)


def _from_time_major(y):
    w = y.shape[1]
    c = y[:CTX_ROWS].reshape(CTX_LEN, BATCH, w).transpose(1, 0, 2).reshape(CTX_ROWS, w)
    l = y[CTX_ROWS:].reshape(SEQ, BATCH, w).transpose(1, 0, 2).reshape(BATCH * SEQ, w)
    return jnp.concatenate([c, l], axis=0)


def _s5(u, lam_re, lam_im, log_dt, b_re, b_im, c_re, c_im, d_skip, glu_w):
    (fw, bw), wc = _s5_tables(lam_re, lam_im, log_dt, b_re, b_im, c_re, c_im)
    u_tm = _to_time_major(u)
    y_fw = _s5_pass(False, u_tm, fw[0], fw[1], wc, (d_skip.astype(F32).reshape(1, C_WIDTH),))
    y = _s5_pass(True, u_tm, bw[0], bw[1], wc, (y_fw, glu_w.astype(_MXU_DTYPE)))
    return _from_time_major(y)


RW_LORA_OFF = 3 * R_WIDTH


def _softplus(z):
    return jnp.maximum(z, 0.0) + jnp.log1p(jnp.exp(-jnp.abs(z)))


def _rwkv_prep_kernel(p_ref, hp_ref, hn_ref, mu_ref, w0_ref, w2_ref, a0_ref, a2_ref, g2_ref, kk_ref, ka_ref, rk_ref,
                      ones_ref, r_o, v_o, g_o, bonus_o, kkn_o, lwf_o, kdf_o, bf_o, lwb_o, kdb_o, bb_o):
    i = pl.program_id(1)
    x = p_ref[...]
    rows = x.shape[0]
    rowi = lax.broadcasted_iota(jnp.int32, (rows, 1), 0)
    prev_row = jnp.where(i >= 2, hp_ref[7:8, :], 0.0)
    next_row = jnp.where((i >= 1) & (i < SEQ_TILES - 1), hn_ref[0:1, :], 0.0)
    prev = jnp.where(rowi == 0, prev_row, pltpu.roll(x, 1, 0))
    nxt = jnp.where(rowi == rows - 1, next_row, pltpu.roll(x, rows - 1, 0))
    x = x + mu_ref[...] * (0.5 * (prev + nxt) - x)

    r = x[:, 0:R_WIDTH]
    k = x[:, R_WIDTH:2 * R_WIDTH]
    v = x[:, 2 * R_WIDTH:3 * R_WIDTH]
    wd = x[:, RW_LORA_OFF:RW_LORA_OFF + 128]
    ad = x[:, RW_LORA_OFF + 128:RW_LORA_OFF + 256]
    gd = x[:, RW_LORA_OFF + 256:RW_LORA_OFF + 384]
    ones = ones_ref[...]

    r_o[...] = r
    v_o[...] = v
    g_o[...] = _bdot(_sigmoid(gd), g2_ref[...])
    bonus_o[...] = _dot_by_sel(r * k * rk_ref[...], ones) * v
    kk = k * kk_ref[...]
    kkn = kk * jnp.minimum(lax.rsqrt(_dot_by_sel(kk * kk, ones)), 1e12)
    kkn_o[...] = kkn
    tw = jnp.tanh(wd)
    for d, (lw_o, kd_o, b_o) in enumerate(((lwf_o, kdf_o, bf_o), (lwb_o, kdb_o, bb_o))):
        w = -_softplus(-(w0_ref[d:d + 1, :] + _bdot(tw, w2_ref[d]))) - 0.5
        lw_o[...] = -jnp.exp(w)
        a = _sigmoid(a0_ref[d:d + 1, :] + _bdot(ad, a2_ref[d]))
        kd_o[...] = k * (1.0 + (a - 1.0) * ka_ref[...])
        b_o[...] = kkn * a


def _rwkv_prep(p, mu, w0, w2pad, a0, a2pad, g2, k_k, k_a, r_k, ones_blk):
    m, w = p.shape
    hb = ROW_TILE // 8
    seq = lambda width: pl.BlockSpec((ROW_TILE, width), lambda b, i: (_seq_block(b, i), 0))
    halo_prev = pl.BlockSpec((8, w), lambda b, i: (jnp.maximum(_seq_block(b, i) * hb - 1, 0), 0))
    halo_next = pl.BlockSpec((8, w), lambda b, i: (jnp.minimum((_seq_block(b, i) + 1) * hb, m // 8 - 1), 0))
    full = lambda a: pl.BlockSpec(a.shape, lambda b, i: (0,) * a.ndim)
    consts = [mu, w0, w2pad, a0, a2pad, g2, k_k, k_a, r_k, ones_blk]
    return pl.pallas_call(
        _rwkv_prep_kernel,
        grid=(BATCH, SEQ_TILES),
        in_specs=[seq(w), halo_prev, halo_next] + [full(c) for c in consts],
        out_specs=[seq(R_WIDTH)] * 11,
        out_shape=[jax.ShapeDtypeStruct((m, R_WIDTH), F32)] * 11,
        compiler_params=_params("parallel", "parallel"),
        name="rwkv_prep",
    )(p, p, p, *consts)


RW_CHUNK = 64
RW_SEQS = 4


def _rwkv_scan_kernel(reverse, finish, *refs):
    per_seq = 9 if finish else 6
    seq_refs = [refs[q * per_seq:(q + 1) * per_seq] for q in range(RW_SEQS)]
    rest = refs[RW_SEQS * per_seq:]
    if finish:
        lng_ref, lnb_ref = rest[:2]
        rest = rest[2:]
    o_refs, st_ref = rest[:RW_SEQS], rest[RW_SEQS]
    C = RW_CHUNK
    P = 2 * R_HEAD
    n_chunks = ROW_TILE // C

    @pl.when(pl.program_id(1) == 0)
    def _():
        st_ref[...] = jnp.zeros_like(st_ref)

    row = lax.broadcasted_iota(jnp.int32, (C, C), 0)
    col = lax.broadcasted_iota(jnp.int32, (C, C), 1)
    incl = (row <= col) if reverse else (row >= col)
    strict = (row < col) if reverse else (row > col)
    tri = incl.astype(F32)
    row2 = lax.broadcasted_iota(jnp.int32, (C, 2 * C), 0)
    col2 = lax.broadcasted_iota(jnp.int32, (C, 2 * C), 1)
    s2 = jnp.where(col2 < C, col2, col2 - C)
    incl2 = (row2 <= s2) if reverse else (row2 >= s2)
    strict_k = ((row2 < s2) if reverse else (row2 > s2)) & (col2 >= C)
    left = lax.broadcasted_iota(jnp.int32, (C, P), 1) < R_HEAD
    left2 = lax.broadcasted_iota(jnp.int32, (2 * C, P), 1) < R_HEAD
    blockdiag = ((lax.broadcasted_iota(jnp.int32, (P, P), 0) < R_HEAD)
                 == (lax.broadcasted_iota(jnp.int32, (P, P), 1) < R_HEAD))
    steps = int(math.log2(C))

    def chunk(ci, carry):
        c = (n_chunks - 1 - ci) if reverse else ci
        rows = pl.ds(pl.multiple_of(c * C, C), C)
        npair = R_HEADS // 2
        pairs = range(RW_SEQS * npair)
        heads = [(p, half) for p in pairs for half in range(2)]
        lane_sl = [slice((p % npair) * P, (p % npair + 1) * P) for p in pairs]
        sts = [st_ref[p] for p in pairs]
        vps, rhss, ars, tails, decays, fin = [], [], [], [], [], []
        for q in range(RW_SEQS):
            r_ref, kd_ref, v_ref, lw_ref, kkn_ref, b_ref = seq_refs[q][:6]
            lw = lw_ref[rows, :]
            kd, bv, v_all = kd_ref[rows, :], b_ref[rows, :], v_ref[rows, :]
            gi = _dot_sel(tri, lw)
            g_tot = gi[0:1] if reverse else gi[C - 1:C]
            inv = jnp.exp(-gi)
            a_t = -kkn_ref[rows, :] * jnp.exp(gi - lw)
            r_t = r_ref[rows, :] * jnp.exp(gi)
            b_t, k_t = bv * inv, kd * inv
            tail = jnp.exp(g_tot - gi)
            b_tail, k_tail = bv * tail, kd * tail
            decay = jnp.exp(g_tot)
            for j in range(npair):
                sl = lane_sl[j]
                vps.append(v_all[:, sl])
                rhss.append(jnp.concatenate([b_t[:, sl], k_t[:, sl], sts[q * npair + j]], axis=0).astype(_MXU_DTYPE))
                ars.append(jnp.concatenate([a_t[:, sl], r_t[:, sl]], axis=0))
                tails.append(jnp.concatenate([b_tail[:, sl], k_tail[:, sl]], axis=0))
                decays.append(decay[:, sl])
                if finish:
                    of_ref, g_ref, bonus_ref = seq_refs[q][6:9]
                    fin.append((of_ref[rows, sl], g_ref[rows, sl], bonus_ref[rows, sl]))
        zvs = [jnp.concatenate([jnp.zeros((C, P), F32), vp], axis=0).astype(_MXU_DTYPE) for vp in vps]
        prods = [_dot(jnp.where(left2 if half == 0 else jnp.logical_not(left2), ars[p], 0.0).astype(_MXU_DTYPE),
                      rhss[p], trans_b=True) for p, half in heads]
        ahs = [prod[:, 2 * C:] for prod in prods]
        us = [ah[:C] + _dot(jnp.where(strict_k, prod[:C, :2 * C], 0.0).astype(_MXU_DTYPE), zvs[p])
              for (p, half), prod, ah in zip(heads, prods, ahs)]
        npows = [jnp.where(strict, prod[:C, :C], 0.0) for prod in prods]
        for it in range(steps):
            us = [u + _bdot(npow, u) for u, npow in zip(us, npows)]
            if it + 1 < steps:
                npows = [_bdot(npow, npow) for npow in npows]
        os_ = [ah[C:] + _bdot(jnp.where(incl2, prod[C:, :2 * C], 0.0), jnp.concatenate([u, vps[p]], axis=0))
               for (p, half), prod, ah, u in zip(heads, prods, ahs, us)]
        outs = []
        for p in pairs:
            sl, st, vp = lane_sl[p], sts[p], vps[p]
            u_pair = jnp.where(left, us[2 * p], us[2 * p + 1])
            o = jnp.where(left, os_[2 * p], os_[2 * p + 1])
            upd = _bdot(jnp.concatenate([u_pair, vp], axis=0), tails[p], trans_a=True)
            st_ref[p] = jnp.where(blockdiag, st * decays[p] + upd, 0.0)
            if finish:
                of_p, g_p, bonus_p = fin[p]
                o = o + of_p

                def head_mean(t):
                    tot = jnp.sum(t, axis=1, keepdims=True)
                    lsum = jnp.sum(jnp.where(left, t, 0.0), axis=1, keepdims=True)
                    return jnp.where(left, lsum, tot - lsum) * (1.0 / R_HEAD)

                oc = o - head_mean(o)
                o = oc * lax.rsqrt(head_mean(oc * oc) + RWKV_GN_EPS) * lng_ref[:, sl] + lnb_ref[:, sl]
                o = (o + bonus_p) * g_p
            outs.append(o)
        for p, o in enumerate(outs):
            o_refs[p // npair][rows, lane_sl[p]] = o
        return carry

    lax.fori_loop(0, n_chunks, chunk, 0)


def _rwkv_scan(reverse, r, kd, v, lw, kkn, bvec, finish_args=None):
    m, w = r.shape
    nb = BATCH // RW_SEQS

    def seq_spec(q):
        return pl.BlockSpec((ROW_TILE, w), lambda b, t: (_seq_block(b + q * nb, _tile_order(t, reverse)), 0))

    def part_block(b, i):
        return jnp.where(i == 0, b, nb + b * (SEQ // ROW_TILE) + i - 1)

    part_spec = pl.BlockSpec((ROW_TILE, w), lambda b, t: (part_block(b, _tile_order(t, reverse)), 0))
    vec_spec = pl.BlockSpec((1, w), lambda b, t: (0, 0))
    ins, specs = [], []
    for q in range(RW_SEQS):
        ins += [r, kd, v, lw, kkn, bvec]
        specs += [seq_spec(q)] * 6
        if finish_args is not None:
            o_parts, g, bonus = finish_args[:3]
            ins += [o_parts[q], g, bonus]
            specs += [part_spec, seq_spec(q), seq_spec(q)]
    if finish_args is not None:
        ins += list(finish_args[3:])
        specs += [vec_spec, vec_spec]
    return pl.pallas_call(
        functools.partial(_rwkv_scan_kernel, reverse, finish_args is not None),
        grid=(nb, SEQ_TILES),
        in_specs=specs,
        out_specs=[part_spec] * RW_SEQS,
        out_shape=[jax.ShapeDtypeStruct((m // RW_SEQS, w), F32)] * RW_SEQS,
        scratch_shapes=[pltpu.VMEM((RW_SEQS * R_HEADS // 2, 2 * R_HEAD, 2 * R_HEAD), F32)],
        compiler_params=_params("parallel", "arbitrary"),
        name="rwkv_bwd" if reverse else "rwkv_fwd",
    )(*ins)


def _merge_seq_parts(parts):
    c = CTX_ROWS // RW_SEQS
    return jnp.concatenate([p[:c] for p in parts] + [p[c:] for p in parts], axis=0)


def _rwkv7(p, mu, w0, w2, a0, a2, g2, k_k, k_a, r_k, lnx_g, lnx_b):
    zeros_w = jnp.zeros((R_LORA_W, R_WIDTH), F32)
    w2pad = jnp.stack([jnp.concatenate([w2[0], zeros_w], 0), jnp.concatenate([zeros_w, w2[1]], 0)]).astype(_MXU_DTYPE)
    a2pad = jnp.stack([jnp.concatenate([a2[0], zeros_w], 0), jnp.concatenate([zeros_w, a2[1]], 0)]).astype(_MXU_DTYPE)
    head = np.arange(R_WIDTH) // R_HEAD
    ones_blk = jnp.asarray(head[:, None] == head[None, :], F32)
    row = lambda t: t.astype(F32).reshape(1, -1)
    r, v, g, bonus, kkn, lwf, kdf, bf, lwb, kdb, bb = _rwkv_prep(
        p, row(mu), w0.astype(F32), w2pad, a0.astype(F32), a2pad, g2.astype(_MXU_DTYPE),
        row(k_k), row(k_a), row(r_k), ones_blk)
    o_fw = _rwkv_scan(False, r, kdf, v, lwf, kkn, bf)
    return _merge_seq_parts(_rwkv_scan(True, r, kdb, v, lwb, kkn, bb, (o_fw, g, bonus, row(lnx_g), row(lnx_b))))


PROJ_TM = 256
FFN_TM = 512
EVEN_SPLITS = (A_WIDTH,) * 5 + (B_QW, B_KVW, B_KVW)
EVEN_ROTARY = (5, 6)
ODD_SPLITS = (C_WIDTH, R_IN)


def _mod_tiles(vec9, tm):
    idx = np.concatenate([np.full(CTX_ROWS // tm, BATCH), np.repeat(np.arange(BATCH), SEQ // tm)])
    return vec9[idx][:, None, :]


def kernel(x, c, ctx, c_ctx, ada_w, ada_b, ln_g, ln_b, ev_w_in, ev_w_out, hg_lb, hg_norm_g, attn_sink, ffn_w_gate, ffn_w_up, ffn_w_down, od_w_in, od_w_out, s5_lam_re, s5_lam_im, s5_log_dt, s5_b_re, s5_b_im, s5_c_re, s5_c_im, s5_d, s5_glu_w, rwkv_mu, rwkv_w0, rwkv_w2, rwkv_a0, rwkv_a2, rwkv_g2, rwkv_k_k, rwkv_k_a, rwkv_r_k, rwkv_ln_g, rwkv_ln_b, moe_router_w, moe_router_b, moe_w_gate, moe_w_up, moe_w_down):
    d = D_MODEL
    xs = jnp.concatenate([ctx.reshape(CTX_ROWS, d), x.reshape(BATCH * SEQ, d)], axis=0).astype(F32)
    cond = jnp.concatenate([c, c_ctx[None, :], jnp.zeros((16 - BATCH - 1, d), F32)], axis=0)
    ada = _ada_all(cond, ada_w, ada_b)
    lb_soft = jax.nn.softmax(hg_lb.astype(F32), axis=0)
    lb_all = jnp.cumsum(lb_soft, axis=0) - lb_soft[0:1]
    rope_tables = _rope_tables()

    for layer in range(DEPTH):
        j = layer // 2
        sh1, sc1, gt1, sh2, sc2, gt2 = [ada[layer, :BATCH + 1, n * d:(n + 1) * d] for n in range(6)]
        if layer % 2 == 0:
            q, f_fw, f_bw, i_in, g, aq, ak, av = _mod_matmul(
                xs, _mod_tiles(1.0 + sc1, PROJ_TM), _mod_tiles(sh1, PROJ_TM), ev_w_in[j].astype(BF16), EVEN_SPLITS, PROJ_TM,
                rotary=EVEN_ROTARY, rope_tables=rope_tables)
            y1 = _hgrn2(q, f_fw, f_bw, i_in, g, lb_all[j], hg_norm_g[j])
            y2 = _window_attention(aq, ak, av, attn_sink[j])
            w_out = ev_w_out[j]
        else:
            u, p_rw = _mod_matmul(
                xs, _mod_tiles(1.0 + sc1, PROJ_TM), _mod_tiles(sh1, PROJ_TM), od_w_in[j].astype(BF16), ODD_SPLITS, PROJ_TM)
            y1 = _s5(u, s5_lam_re[j], s5_lam_im[j], s5_log_dt[j], s5_b_re[j], s5_b_im[j], s5_c_re[j], s5_c_im[j],
                     s5_d[j], s5_glu_w[j])
            y2 = _rwkv7(p_rw, rwkv_mu[j], rwkv_w0[j], rwkv_w2[j], rwkv_a0[j], rwkv_a2[j], rwkv_g2[j],
                        rwkv_k_k[j], rwkv_k_a[j], rwkv_r_k[j], rwkv_ln_g[j], rwkv_ln_b[j])
            w_out = od_w_out[j]
        skip = CTX_ROWS if layer == DEPTH - 1 else 0
        tiles = lambda vec9, tm: _mod_tiles(vec9, tm)[skip // tm:]
        xs = _out_proj_ln(y1, y2, w_out.astype(BF16), xs, tiles(gt1, PROJ_TM), ln_g[layer, 0], ln_b[layer, 0],
                          PROJ_TM, skip_rows=skip)
        scale2, shift2, gate2 = tiles(1.0 + sc2, FFN_TM), tiles(sh2, FFN_TM), tiles(gt2, FFN_TM)
        if layer % 2 == 0:
            xs = _ffn_ln(xs, scale2, shift2, gate2, ffn_w_gate[j][None].astype(BF16),
                         ffn_w_up[j][None].astype(BF16), ffn_w_down[j][None].astype(BF16),
                         ln_g[layer, 1], ln_b[layer, 1], FFN_TM)
        else:
            w_pad = jnp.pad(moe_router_w[j].astype(F32), ((0, 0), (0, 128 - N_EXPERTS)))
            b_pad = jnp.pad(moe_router_b[j].astype(F32), (0, 128 - N_EXPERTS)).reshape(1, 128)
            route, sel, h = _router(xs, scale2, shift2, w_pad, b_pad, FFN_TM)
            pos1, pos2, w1, w2, tile_expert, n_used, pad_lo, pad_hi = _moe_plan(route, sel)
            hs = _moe_dispatch(h, pos1, pos2, pad_lo, pad_hi, n_used)
            y_sorted = _moe_experts(hs, tile_expert, n_used, moe_w_gate[j].astype(BF16),
                                    moe_w_up[j].astype(BF16), moe_w_down[j].astype(BF16))
            xs = _moe_combine_ln(y_sorted, pos1, pos2, w1, w2, xs, tiles(gt2, MOE_COMBINE_TM),
                                 ln_g[layer, 1], ln_b[layer, 1])
    return xs.reshape(BATCH, SEQ, d)
```

```python
import functools
import math

import numpy as np
import jax
import jax.numpy as jnp
from jax import lax
from jax.experimental import pallas as pl
from jax.experimental.pallas import tpu as pltpu

F32 = jnp.float32
BF16 = jnp.bfloat16
HIGHEST = lax.Precision.HIGHEST

D_MODEL = 1024
BATCH = 8
SEQ = 2048
CTX_LEN = 256
DEPTH = 4
GRID_W = 64
ROW_TILE = 256
SEQ_TILES = (CTX_LEN + SEQ) // ROW_TILE
CTX_ROWS = BATCH * CTX_LEN
M_ROWS = BATCH * (CTX_LEN + SEQ)

A_HEADS, A_DK, A_WIDTH = 4, 128, 512
B_HEADS, B_KV_HEADS, B_HEAD_DIM = 8, 2, 64
B_QW, B_KVW = 512, 128
B_BLOCK = 128
ROPE_BASE = 10000.0
C_GROUP, C_GROUPS, C_WIDTH, C_STATE = 16, 32, 512, 64
R_HEADS, R_HEAD, R_WIDTH = 8, 64, 512
R_LORA_W, R_LORA_A, R_LORA_G = 64, 64, 128
R_IN = 3 * R_WIDTH + 2 * R_LORA_W + 2 * R_LORA_A + R_LORA_G
D_FF = 2816
N_EXPERTS = 8
DEEPNORM_ALPHA = (2 * DEPTH) ** 0.25
LN_EPS = 1e-5
RWKV_GN_EPS = 64e-5

VMEM_LIMIT_BYTES = 56 * 1024 * 1024


def _params(*sem):
    return pltpu.CompilerParams(dimension_semantics=sem, vmem_limit_bytes=VMEM_LIMIT_BYTES)


def _dot(a, b, *, trans_a=False, trans_b=False, precision=None):
    dn = (((0 if trans_a else 1,), (1 if trans_b else 0,)), ((), ()))
    return lax.dot_general(a, b, dn, preferred_element_type=F32, precision=precision)


_MXU_DTYPE = BF16


def _bdot(a, b, **kw):
    return _dot(a.astype(_MXU_DTYPE), b.astype(_MXU_DTYPE), **kw)


def _split3(x):
    x1 = x.astype(BF16)
    r1 = x - x1.astype(F32)
    x2 = r1.astype(BF16)
    x3 = (r1 - x2.astype(F32)).astype(BF16)
    return x1, x2, x3


def _dot_sel(sel, x):
    s = sel.astype(BF16)
    x1, x2, x3 = _split3(x)
    return _dot(s, x1) + _dot(s, x2) + _dot(s, x3)


def _dot_by_sel(x, sel):
    s = sel.astype(BF16)
    x1, x2, x3 = _split3(x)
    return _dot(x1, s) + _dot(x2, s) + _dot(x3, s)


def _sigmoid(x):
    return jax.nn.sigmoid(x)


def _silu(x):
    return x * _sigmoid(x)


def _seq_block(b, i):
    return jnp.where(i == 0, b, BATCH + b * (SEQ // ROW_TILE) + i - 1)


def _tile_order(i, reverse):
    if not reverse:
        return i
    return jnp.where(i == 0, 0, SEQ_TILES - i)


def _ada_kernel(cond_ref, w_ref, b_ref, o_ref):
    o_ref[0] = _dot(_silu(cond_ref[...]), w_ref[0], precision=HIGHEST) + b_ref[0]


def _ada_all(cond, ada_w, ada_b):
    rows = cond.shape[0]
    tn = 1536
    return pl.pallas_call(
        _ada_kernel,
        grid=(DEPTH, 6 * D_MODEL // tn),
        in_specs=[pl.BlockSpec((rows, D_MODEL), lambda l, j: (0, 0)),
                  pl.BlockSpec((1, D_MODEL, tn), lambda l, j: (l, 0, j)),
                  pl.BlockSpec((1, 1, tn), lambda l, j: (l, 0, j))],
        out_specs=pl.BlockSpec((1, rows, tn), lambda l, j: (l, 0, j)),
        out_shape=jax.ShapeDtypeStruct((DEPTH, rows, 6 * D_MODEL), F32),
        compiler_params=_params("parallel", "parallel"),
        name="ada",
    )(cond, ada_w, ada_b.reshape(DEPTH, 1, 6 * D_MODEL))


def _mod_matmul_kernel(splits, rotary, x_ref, sc_ref, sh_ref, w_ref, *refs):
    if rotary:
        cos, sa, sb = refs[0][...], refs[1][...], refs[2][...]
        refs = refs[3:]
    h = (x_ref[...] * sc_ref[0] + sh_ref[0]).astype(BF16)
    off = 0
    for n, (o_ref, width) in enumerate(zip(refs, splits)):
        y = _dot(h, w_ref[:, off:off + width])
        if n in rotary:
            for j in range(width // 128):
                x = y[:, j * 128:(j + 1) * 128]
                o_ref[:, j * 128:(j + 1) * 128] = x * cos + pltpu.roll(x, 96, 1) * sa + pltpu.roll(x, 32, 1) * sb
        else:
            o_ref[...] = y
        off += width


def _mod_matmul(x, scale_t, shift_t, w_bf16, splits, tm, rotary=(), rope_tables=None):
    m, d = x.shape
    n = w_bf16.shape[1]
    assert sum(splits) == n and m % tm == 0
    ins = [x, scale_t, shift_t, w_bf16]
    specs = [pl.BlockSpec((tm, d), lambda i: (i, 0)),
             pl.BlockSpec((1, 1, d), lambda i: (i, 0, 0)),
             pl.BlockSpec((1, 1, d), lambda i: (i, 0, 0)),
             pl.BlockSpec((d, n), lambda i: (0, 0))]
    if rotary:
        assert tm == ROW_TILE
        ctx_tiles, lat_tiles = CTX_ROWS // tm, SEQ // tm
        tab = pl.BlockSpec((tm, 128), lambda i: (jnp.where(i < ctx_tiles, 0, (i - ctx_tiles) % lat_tiles + 1), 0))
        ins += list(rope_tables)
        specs += [tab, tab, tab]
    return pl.pallas_call(
        functools.partial(_mod_matmul_kernel, splits, tuple(rotary)),
        grid=(m // tm,),
        in_specs=specs,
        out_specs=[pl.BlockSpec((tm, w), lambda i: (i, 0)) for w in splits],
        out_shape=[jax.ShapeDtypeStruct((m, w), F32) for w in splits],
        compiler_params=_params("parallel"),
        name="mod_matmul",
    )(*ins)


def _layer_norm_rows(z, g, b):
    mu = jnp.mean(z, axis=-1, keepdims=True)
    zc = z - mu
    var = jnp.mean(zc * zc, axis=-1, keepdims=True)
    return zc * lax.rsqrt(var + LN_EPS) * g + b


def _out_proj_kernel(y1_ref, y2_ref, w_ref, x_ref, gt_ref, g_ref, b_ref, o_ref):
    k1 = y1_ref.shape[1]
    proj = _bdot(y1_ref[...], w_ref[:k1, :]) + _bdot(y2_ref[...], w_ref[k1:, :])
    z = DEEPNORM_ALPHA * x_ref[...] + gt_ref[0] * proj
    o_ref[...] = _layer_norm_rows(z, g_ref[...], b_ref[...])


def _out_proj_ln(y1, y2, w_bf16, x, gate_t, ln_g, ln_b, tm, skip_rows=0):
    m, d = x.shape
    k1, k2 = y1.shape[1], y2.shape[1]
    off = skip_rows // tm
    return pl.pallas_call(
        _out_proj_kernel,
        grid=((m - skip_rows) // tm,),
        in_specs=[pl.BlockSpec((tm, k1), lambda i: (i + off, 0)),
                  pl.BlockSpec((tm, k2), lambda i: (i + off, 0)),
                  pl.BlockSpec((k1 + k2, d), lambda i: (0, 0)),
                  pl.BlockSpec((tm, d), lambda i: (i + off, 0)),
                  pl.BlockSpec((1, 1, d), lambda i: (i, 0, 0)),
                  pl.BlockSpec((1, d), lambda i: (0, 0)),
                  pl.BlockSpec((1, d), lambda i: (0, 0))],
        out_specs=pl.BlockSpec((tm, d), lambda i: (i, 0)),
        out_shape=jax.ShapeDtypeStruct((m - skip_rows, d), F32),
        compiler_params=_params("parallel"),
        name="out_proj_ln",
    )(y1, y2, w_bf16, x, gate_t, ln_g.reshape(1, d), ln_b.reshape(1, d))


FF_CHUNK = 256


def _swiglu_rows(h, wg_ref, wu_ref, wd_ref):
    ff = wg_ref.shape[1]
    acc = jnp.zeros((h.shape[0], wd_ref.shape[1]), F32)
    pending = None
    for c in range(ff // FF_CHUNK):
        cols = slice(c * FF_CHUNK, (c + 1) * FF_CHUNK)
        g, u = _dot(h, wg_ref[:, cols]), _dot(h, wu_ref[:, cols])
        if pending is not None:
            acc = acc + _dot(pending[0], wd_ref[pending[1], :])
        pending = ((_silu(g) * u).astype(BF16), cols)
    return acc + _dot(pending[0], wd_ref[pending[1], :])


def _ffn_kernel(x_ref, sc_ref, sh_ref, gt_ref, wg_ref, wu_ref, wd_ref, g_ref, b_ref, o_ref):
    x = x_ref[...]
    f = _swiglu_rows((x * sc_ref[0] + sh_ref[0]).astype(BF16), wg_ref, wu_ref, wd_ref)
    o_ref[...] = _layer_norm_rows(DEEPNORM_ALPHA * x + gt_ref[0] * f, g_ref[...], b_ref[...])


def _ffn_ln(x, scale_t, shift_t, gate_t, wg, wu, wd, layer, ln_g, ln_b, tm):
    m, d = x.shape
    ff = wg.shape[2]
    return pl.pallas_call(
        _ffn_kernel,
        grid=(m // tm,),
        in_specs=[pl.BlockSpec((tm, d), lambda i: (i, 0)),
                  pl.BlockSpec((1, 1, d), lambda i: (i, 0, 0)),
                  pl.BlockSpec((1, 1, d), lambda i: (i, 0, 0)),
                  pl.BlockSpec((1, 1, d), lambda i: (i, 0, 0)),
                  pl.BlockSpec((None, d, ff), lambda i: (layer, 0, 0)),
                  pl.BlockSpec((None, d, ff), lambda i: (layer, 0, 0)),
                  pl.BlockSpec((None, ff, d), lambda i: (layer, 0, 0)),
                  pl.BlockSpec((1, d), lambda i: (0, 0)),
                  pl.BlockSpec((1, d), lambda i: (0, 0))],
        out_specs=pl.BlockSpec((tm, d), lambda i: (i, 0)),
        out_shape=jax.ShapeDtypeStruct((m, d), F32),
        compiler_params=_params("parallel"),
        name="ffn_ln",
    )(x, scale_t, shift_t, gate_t, wg, wu, wd, ln_g.reshape(1, d), ln_b.reshape(1, d))


def _router_kernel(x_ref, sc_ref, sh_ref, w_ref, b_ref, o_ref, sel_ref, h_ref):
    h = x_ref[...] * sc_ref[0] + sh_ref[0]
    h_ref[...] = h
    logits = _dot(h, w_ref[...], precision=HIGHEST) + b_ref[...]
    lane = lax.broadcasted_iota(jnp.int32, logits.shape, 1)
    neg = jnp.float32(-jnp.inf)
    logits = jnp.where(lane < N_EXPERTS, logits, neg)
    v1 = jnp.max(logits, axis=1, keepdims=True)
    i1 = jnp.min(jnp.where(logits == v1, lane, 128), axis=1, keepdims=True)
    rest = jnp.where(lane == i1, neg, logits)
    v2 = jnp.max(rest, axis=1, keepdims=True)
    i2 = jnp.min(jnp.where(rest == v2, lane, 128), axis=1, keepdims=True)
    e2 = jnp.exp(v2 - v1)
    p1 = 1.0 / (1.0 + e2)
    p2 = e2 / (1.0 + e2)
    o_ref[...] = jnp.where(lane == i1, p1, 0.0) + jnp.where(lane == i2, p2, 0.0)
    sel_ref[...] = ((lane == i1) | (lane == i2)).astype(F32)


def _router(x, scale_t, shift_t, w_pad, b_pad, tm):
    m, d = x.shape
    lanes = pl.BlockSpec((tm, 128), lambda i: (i, 0))
    return pl.pallas_call(
        _router_kernel,
        grid=(m // tm,),
        in_specs=[pl.BlockSpec((tm, d), lambda i: (i, 0)),
                  pl.BlockSpec((1, 1, d), lambda i: (i, 0, 0)),
                  pl.BlockSpec((1, 1, d), lambda i: (i, 0, 0)),
                  pl.BlockSpec((d, 128), lambda i: (0, 0)),
                  pl.BlockSpec((1, 128), lambda i: (0, 0))],
        out_specs=[lanes, lanes, pl.BlockSpec((tm, d), lambda i: (i, 0))],
        out_shape=[jax.ShapeDtypeStruct((m, 128), F32), jax.ShapeDtypeStruct((m, 128), F32),
                   jax.ShapeDtypeStruct((m, d), F32)],
        compiler_params=_params("parallel"),
        name="router",
    )(x, scale_t, shift_t, w_pad, b_pad)


MOE_TM = 512
MOE_COMBINE_TM = 1024
MOE_DISPATCH_TM = 1024


def _moe_rows(m):
    return 2 * m + N_EXPERTS * MOE_TM


def _moe_plan(route, sel):
    moe_rows = _moe_rows(route.shape[0])
    sel8 = sel[:, :N_EXPERTS].astype(jnp.int32)
    counts = jnp.sum(sel8, axis=0)
    rank = jnp.cumsum(sel8, axis=0) - sel8
    padded = ((counts + MOE_TM - 1) // MOE_TM) * MOE_TM
    ends = jnp.cumsum(padded)
    starts = ends - padded
    pos = jnp.where(sel8 > 0, starts[None, :] + rank, moe_rows)
    pos1 = jnp.min(pos, axis=1)
    pos2 = jnp.min(jnp.where(pos == pos1[:, None], moe_rows, pos), axis=1)
    route8 = route[:, :N_EXPERTS]
    w1 = jnp.sum(jnp.where(pos == pos1[:, None], route8, 0.0), axis=1, keepdims=True)
    w2 = jnp.sum(jnp.where(pos == pos2[:, None], route8, 0.0), axis=1, keepdims=True)
    tile_start = jnp.arange(moe_rows // MOE_TM, dtype=jnp.int32) * MOE_TM
    tile_expert = jnp.minimum(jnp.sum((tile_start[:, None] >= ends[None, :]).astype(jnp.int32), axis=1),
                              N_EXPERTS - 1).astype(jnp.int32)
    n_used = (ends[-1] // MOE_TM).astype(jnp.int32).reshape(1)
    pad_lo, pad_hi = (starts + counts).astype(jnp.int32), ends.astype(jnp.int32)
    return pos1.astype(jnp.int32), pos2.astype(jnp.int32), w1, w2, tile_expert, n_used, pad_lo, pad_hi


def _moe_dispatch_kernel(p1_ref, p2_ref, lo_ref, hi_ref, nused_ref, h_ref, hs_hbm, zero_ref, sem, zsem):
    i = pl.program_id(0)
    base = i * MOE_DISPATCH_TM

    def row_copy(r, dst_row):
        return pltpu.make_async_copy(h_ref.at[pl.ds(r, 1), :], hs_hbm.at[pl.ds(dst_row, 1), :], sem)

    def start(r, carry):
        row_copy(r, p1_ref[base + r]).start(priority=0)
        row_copy(r, p2_ref[base + r]).start(priority=1)
        return carry

    def wait(r, carry):
        row_copy(0, 0).wait()
        row_copy(0, 0).wait()
        return carry

    lax.fori_loop(0, MOE_DISPATCH_TM, start, 0, unroll=8)
    lax.fori_loop(0, MOE_DISPATCH_TM, wait, 0, unroll=8)

    @pl.when(i == pl.num_programs(0) - 1)
    def _():
        zero_ref[...] = jnp.zeros_like(zero_ref)

        def zero_copy(dst_row):
            return pltpu.make_async_copy(zero_ref.at[pl.ds(0, 1), :], hs_hbm.at[pl.ds(dst_row, 1), :], zsem)

        for e in range(N_EXPERTS):
            lo, hi = lo_ref[e], hi_ref[e]

            def zstart(r, carry):
                zero_copy(r).start()
                return carry

            def zwait(r, carry):
                zero_copy(0).wait()
                return carry

            lax.fori_loop(lo, hi, zstart, 0)
            lax.fori_loop(lo, hi, zwait, 0)

        def tile_copy(t):
            return pltpu.make_async_copy(zero_ref, hs_hbm.at[pl.ds(t * MOE_TM, MOE_TM), :], zsem)

        def tstart(t, carry):
            tile_copy(t).start()
            return carry

        def twait(t, carry):
            tile_copy(0).wait()
            return carry

        n_tiles = hs_hbm.shape[0] // MOE_TM
        lax.fori_loop(nused_ref[0], n_tiles, tstart, 0)
        lax.fori_loop(nused_ref[0], n_tiles, twait, 0)


def _moe_dispatch(h, pos1, pos2, pad_lo, pad_hi, n_used):
    m, d = h.shape
    grid_spec = pltpu.PrefetchScalarGridSpec(
        num_scalar_prefetch=5,
        grid=(m // MOE_DISPATCH_TM,),
        in_specs=[pl.BlockSpec((MOE_DISPATCH_TM, d), lambda i, *_: (i, 0))],
        out_specs=pl.BlockSpec(memory_space=pl.ANY),
        scratch_shapes=[pltpu.VMEM((MOE_TM, d), F32), pltpu.SemaphoreType.DMA(()), pltpu.SemaphoreType.DMA(())],
    )
    return pl.pallas_call(
        _moe_dispatch_kernel,
        grid_spec=grid_spec,
        out_shape=jax.ShapeDtypeStruct((_moe_rows(m), d), F32),
        compiler_params=_params("arbitrary"),
        name="moe_dispatch",
    )(pos1, pos2, pad_lo, pad_hi, n_used, h)


def _moe_expert_kernel(texp_ref, nused_ref, hs_ref, wg_ref, wu_ref, wd_ref, y_ref):
    t = pl.program_id(0)

    @pl.when(t < nused_ref[0])
    def _():
        y_ref[...] = _swiglu_rows(hs_ref[...].astype(BF16), wg_ref, wu_ref, wd_ref)

    @pl.when(t >= nused_ref[0])
    def _():
        y_ref[...] = jnp.zeros_like(y_ref)


def _moe_experts(hs, tile_expert, n_used, wg, wu, wd, layer):
    moe_rows, d = hs.shape
    ff = wg.shape[3]
    grid_spec = pltpu.PrefetchScalarGridSpec(
        num_scalar_prefetch=2,
        grid=(moe_rows // MOE_TM,),
        in_specs=[pl.BlockSpec((MOE_TM, d), lambda t, te, nu: (jnp.minimum(t, nu[0] - 1), 0)),
                  pl.BlockSpec((None, None, d, ff), lambda t, te, nu: (layer, te[t], 0, 0)),
                  pl.BlockSpec((None, None, d, ff), lambda t, te, nu: (layer, te[t], 0, 0)),
                  pl.BlockSpec((None, None, ff, d), lambda t, te, nu: (layer, te[t], 0, 0))],
        out_specs=pl.BlockSpec((MOE_TM, d), lambda t, te, nu: (t, 0)),
    )
    return pl.pallas_call(
        _moe_expert_kernel,
        grid_spec=grid_spec,
        out_shape=jax.ShapeDtypeStruct((moe_rows, d), F32),
        compiler_params=_params("arbitrary"),
        name="moe_experts",
    )(tile_expert, n_used, hs, wg, wu, wd)


def _moe_combine_kernel(p1_ref, p2_ref, y_hbm, x_ref, w1_ref, w2_ref, gt_ref, g_ref, b_ref, o_ref, buf1, buf2, sem):
    base = pl.program_id(0) * MOE_COMBINE_TM

    def row_copy(buf, k, r, src_row):
        return pltpu.make_async_copy(y_hbm.at[pl.ds(src_row, 1), :], buf.at[pl.ds(r, 1), :], sem.at[k])

    def start(r, carry):
        row_copy(buf1, 0, r, p1_ref[base + r]).start(priority=0)
        row_copy(buf2, 1, r, p2_ref[base + r]).start(priority=1)
        return carry

    def wait(r, carry):
        row_copy(buf1, 0, 0, 0).wait()
        row_copy(buf2, 1, 0, 0).wait()
        return carry

    lax.fori_loop(0, MOE_COMBINE_TM, start, 0, unroll=8)
    lax.fori_loop(0, MOE_COMBINE_TM, wait, 0, unroll=8)
    f = w1_ref[...] * buf1[...] + w2_ref[...] * buf2[...]
    z = DEEPNORM_ALPHA * x_ref[...] + gt_ref[0] * f
    o_ref[...] = _layer_norm_rows(z, g_ref[...], b_ref[...])


def _moe_combine_ln(y_sorted, pos1, pos2, w1, w2, x, gate_t, ln_g, ln_b):
    m, d = x.shape
    tm = MOE_COMBINE_TM
    grid_spec = pltpu.PrefetchScalarGridSpec(
        num_scalar_prefetch=2,
        grid=(m // tm,),
        in_specs=[pl.BlockSpec(memory_space=pl.ANY),
                  pl.BlockSpec((tm, d), lambda i, p1, p2: (i, 0)),
                  pl.BlockSpec((tm, 1), lambda i, p1, p2: (i, 0)),
                  pl.BlockSpec((tm, 1), lambda i, p1, p2: (i, 0)),
                  pl.BlockSpec((1, 1, d), lambda i, p1, p2: (i, 0, 0)),
                  pl.BlockSpec((1, d), lambda i, p1, p2: (0, 0)),
                  pl.BlockSpec((1, d), lambda i, p1, p2: (0, 0))],
        out_specs=pl.BlockSpec((tm, d), lambda i, p1, p2: (i, 0)),
        scratch_shapes=[pltpu.VMEM((tm, d), F32), pltpu.VMEM((tm, d), F32), pltpu.SemaphoreType.DMA((2,))],
    )
    return pl.pallas_call(
        _moe_combine_kernel,
        grid_spec=grid_spec,
        out_shape=jax.ShapeDtypeStruct((m, d), F32),
        compiler_params=_params("arbitrary"),
        name="moe_combine_ln",
    )(pos1, pos2, y_sorted, x, w1, w2, gate_t, ln_g.reshape(1, d), ln_b.reshape(1, d))


HGRN_CHUNK = 16
HGRN_SUPER = 64


def _hgrn_kernel(reverse, finish, *refs):
    if finish:
        q_ref, f_ref, i_ref, lb_ref, of_ref, g_ref, ng_ref, o_ref, st_ref = refs
    else:
        q_ref, f_ref, i_ref, lb_ref, o_ref, st_ref = refs
    C, S = HGRN_CHUNK, HGRN_SUPER
    nsub = S // C
    n_super = ROW_TILE // S

    @pl.when(pl.program_id(1) == 0)
    def _():
        st_ref[...] = jnp.zeros_like(st_ref)

    row = lax.broadcasted_iota(jnp.int32, (S, S), 0)
    col = lax.broadcasted_iota(jnp.int32, (S, S), 1)
    blk_r, blk_c = jnp.zeros_like(row), jnp.zeros_like(col)
    for j in range(1, nsub):
        blk_r = blk_r + (row >= j * C).astype(jnp.int32)
        blk_c = blk_c + (col >= j * C).astype(jnp.int32)
    H = C // 2
    hi_r, hi_c = (row - blk_r * C) >= H, (col - blk_c * C) >= H
    col_in = col - blk_c * C - jnp.where(hi_c, H, 0)
    ordered = (row <= col) if reverse else (row >= col)
    same_chunk = blk_r == blk_c
    tri = (same_chunk & ordered).astype(F32)
    keep = same_chunk & (hi_r == hi_c) & ordered
    cross = same_chunk & ((jnp.logical_not(hi_r) & hi_c) if reverse else (hi_r & jnp.logical_not(hi_c)))
    heads = range(A_HEADS)
    sls = [slice(h * A_DK, (h + 1) * A_DK) for h in heads]
    chunk_rows = [slice(c * C, (c + 1) * C) for c in range(nsub)]
    width = A_DK

    def per_chunk_row(t, offset):
        return jnp.concatenate(
            [jnp.broadcast_to(t[c * C + offset:c * C + offset + 1], (C, width)) for c in range(nsub)], axis=0)

    def per_half_row(t, offset):
        return jnp.concatenate(
            [jnp.broadcast_to(t[g * H + offset:g * H + offset + 1], (H, width)) for g in range(S // H)], axis=0)

    def superchunk(si, carry):
        sc = (n_super - 1 - si) if reverse else si
        rows = pl.ds(pl.multiple_of(sc * S, S), S)
        last = 0 if reverse else C - 1
        vs, qes, kts, qxs, kxs, atts, bs = [], [], [], [], [], [], []
        for sl in sls:
            lb = lb_ref[:, sl]
            q = _silu(q_ref[rows, sl]) * A_DK ** -0.5
            f = lb + (1.0 - lb) * _sigmoid(f_ref[rows, sl])
            k = 1.0 - f
            b = _dot_sel(tri, jnp.log(f))
            b_tot = per_chunk_row(b, last)
            b_mid = per_chunk_row(b, H if reverse else H - 1)
            att = jnp.zeros((S, S), F32)
            for s in range(H):
                tmp = q * jnp.exp(b - per_half_row(b, s)) * per_half_row(k, s)
                att = jnp.where(col_in == s, jnp.sum(tmp, axis=1, keepdims=True), att)
            vs.append(i_ref[rows, sl])
            qes.append(q * jnp.exp(b))
            kts.append(k * jnp.exp(b_tot - b))
            qxs.append(q * jnp.exp(b - b_mid))
            kxs.append(k * jnp.exp(b_mid - b))
            atts.append(att)
            bs.append(b)
        xatts = [_bdot(qx, kx, trans_b=True) for qx, kx in zip(qxs, kxs)]
        intra = [_bdot(jnp.where(keep, att, 0.0) + jnp.where(cross, xatt, 0.0), v)
                 for v, att, xatt in zip(vs, atts, xatts)]
        upds = [[_bdot(v[cr], kt[cr], trans_a=True) for cr in chunk_rows] for v, kt in zip(vs, kts)]
        sts = [st_ref[h] for h in heads]
        inter = [[None] * nsub for _ in heads]
        for c in (range(nsub - 1, -1, -1) if reverse else range(nsub)):
            cr = chunk_rows[c]
            for h in heads:
                inter[h][c] = _bdot(qes[h][cr], sts[h], trans_b=True)
                sts[h] = sts[h] * jnp.exp(bs[h][c * C + last:c * C + last + 1]) + upds[h][c]
        outs = []
        for h in heads:
            st_ref[h] = sts[h]
            o = intra[h] + jnp.concatenate(inter[h], axis=0)
            if finish:
                o = o + of_ref[rows, sls[h]]
                o = o * lax.rsqrt(jnp.mean(o * o, axis=1, keepdims=True) + 1e-6)
                o = o * ng_ref[:, sls[h]] * _silu(g_ref[rows, sls[h]])
            outs.append(o)
        for h in heads:
            o_ref[rows, sls[h]] = outs[h]
        return carry

    lax.fori_loop(0, n_super, superchunk, 0)


def _hgrn_pass(reverse, q, f, i, lb, finish_args=None):
    m, w = q.shape
    seq_spec = pl.BlockSpec((ROW_TILE, w), lambda b, t: (_seq_block(b, _tile_order(t, reverse)), 0))
    vec_spec = pl.BlockSpec((1, w), lambda b, t: (0, 0))
    ins = [q, f, i, lb]
    specs = [seq_spec, seq_spec, seq_spec, vec_spec]
    if finish_args is not None:
        of, g, ng = finish_args
        ins += [of, g, ng]
        specs += [seq_spec, seq_spec, vec_spec]
    return pl.pallas_call(
        functools.partial(_hgrn_kernel, reverse, finish_args is not None),
        grid=(BATCH, SEQ_TILES),
        in_specs=specs,
        out_specs=seq_spec,
        out_shape=jax.ShapeDtypeStruct((m, w), F32),
        scratch_shapes=[pltpu.VMEM((A_HEADS, A_DK, A_DK), F32)],
        compiler_params=_params("parallel", "arbitrary"),
        name="hgrn_bwd" if reverse else "hgrn_fwd",
    )(*ins)


def _rope_tables():
    rows = SEQ // GRID_W
    row = jnp.repeat(jnp.arange(rows, dtype=F32), GRID_W)
    colp = jnp.tile(jnp.arange(GRID_W, dtype=F32), rows)
    n_freq = B_HEAD_DIM // 4
    inv = ROPE_BASE ** (-jnp.arange(n_freq, dtype=F32) / n_freq)
    ang = jnp.concatenate([row[:, None] * inv, colp[:, None] * inv], axis=-1)
    cos, sin = jnp.cos(ang), jnp.sin(ang)
    zero = jnp.zeros_like(sin)
    cos_l = jnp.tile(cos, (1, 4))
    sa_l = jnp.tile(jnp.concatenate([-sin, zero], axis=-1), (1, 2))
    sb_l = jnp.tile(jnp.concatenate([zero, sin], axis=-1), (1, 2))
    pad = lambda t, v: jnp.concatenate([jnp.full((CTX_LEN, 128), v, F32), t], axis=0)
    return pad(cos_l, 1.0), pad(sa_l, 0.0), pad(sb_l, 0.0)


def _attend(q_ref, o_ref, sink_ref, k, v, mask):
    tq = q_ref.shape[0]
    left = lax.broadcasted_iota(jnp.int32, (tq, 128), 1) < B_HEAD_DIM
    lk = lax.broadcasted_iota(jnp.int32, k.shape, 1) < B_HEAD_DIM
    neg = jnp.float32(-jnp.inf)
    scale = B_HEAD_DIM ** -0.5
    kr, vr = pltpu.roll(k, 64, 1), pltpu.roll(v, 64, 1)
    kds = [jnp.where(lk, k, kr).astype(_MXU_DTYPE), jnp.where(lk, kr, k).astype(_MXU_DTYPE)]
    vds = [jnp.where(lk, v, vr), jnp.where(lk, vr, v)]
    vsel = [[jnp.where(lk if half == 0 else jnp.logical_not(lk), vd, 0.0).astype(_MXU_DTYPE) for half in range(2)]
            for vd in vds]
    pairs = B_HEADS // 2
    heads = [(p, half) for p in range(pairs) for half in range(2)]
    group = lambda p: p // (pairs // B_KV_HEADS)
    qms = [jnp.where(left if half == 0 else jnp.logical_not(left), q_ref[:, p * 128:(p + 1) * 128] * scale, 0.0)
           .astype(_MXU_DTYPE) for p, half in heads]
    scores = [_dot(qm, kds[group(p)], trans_b=True) for (p, half), qm in zip(heads, qms)]
    if mask is not None:
        scores = [jnp.where(mask, s, neg) for s in scores]
    sinks = [sink_ref[2 * p + half:2 * p + half + 1, 0:1] for p, half in heads]
    mxs = [jnp.maximum(sink, jnp.max(s, axis=1, keepdims=True)) for s, sink in zip(scores, sinks)]
    es = [jnp.exp(s - mx) for s, mx in zip(scores, mxs)]
    denoms = [jnp.exp(sink - mx) + jnp.sum(e, axis=1, keepdims=True) for e, mx, sink in zip(es, mxs, sinks)]
    accs = [_dot(e.astype(_MXU_DTYPE), vsel[group(p)][half]) for (p, half), e in zip(heads, es)]
    for p in range(pairs):
        o_ref[:, p * 128:(p + 1) * 128] = accs[2 * p] / denoms[2 * p] + accs[2 * p + 1] / denoms[2 * p + 1]


ATTN_LATENT_BLOCKS = SEQ // B_BLOCK
ATTN_CTX_BLOCKS = CTX_LEN // B_BLOCK


def _attn_kernel(q_ref, kp_ref, kc_ref, kn_ref, vp_ref, vc_ref, vn_ref, kx_ref, vx_ref, sink_ref, o_ref):
    n = pl.program_id(1)
    nb = ATTN_LATENT_BLOCKS

    @pl.when(n < nb)
    def _():
        row = lax.broadcasted_iota(jnp.int32, (B_BLOCK, B_BLOCK), 0)
        col = lax.broadcasted_iota(jnp.int32, (B_BLOCK, B_BLOCK), 1)
        mask = jnp.concatenate([(col >= row) & (n > 0), jnp.ones((B_BLOCK, B_BLOCK), jnp.bool_),
                                (col <= row) & (n < nb - 1), jnp.ones((B_BLOCK, CTX_LEN), jnp.bool_)], axis=1)
        k = jnp.concatenate([kp_ref[...], kc_ref[...], kn_ref[...], kx_ref[...]], axis=0)
        v = jnp.concatenate([vp_ref[...], vc_ref[...], vn_ref[...], vx_ref[...]], axis=0)
        _attend(q_ref, o_ref, sink_ref, k, v, mask)

    @pl.when(n >= nb)
    def _():
        _attend(q_ref, o_ref, sink_ref, kx_ref[...], vx_ref[...], None)


def _attention(q, k, v, sink_rows):
    m = q.shape[0]
    nb, nc = ATTN_LATENT_BLOCKS, ATTN_CTX_BLOCKS
    base = CTX_ROWS // B_BLOCK

    def q_block(b, n):
        return jnp.where(n < nb, base + b * nb + n, b * nc + n - nb)

    qspec = pl.BlockSpec((B_BLOCK, B_QW), lambda b, n: (q_block(b, n), 0))

    def kv(shift):
        return pl.BlockSpec((B_BLOCK, B_KVW), lambda b, n: (base + b * nb + jnp.clip(n + shift, 0, nb - 1), 0))

    ctx_kv = pl.BlockSpec((CTX_LEN, B_KVW), lambda b, n: (b, 0))
    sink_spec = pl.BlockSpec((B_HEADS, 128), lambda b, n: (0, 0))
    return pl.pallas_call(
        _attn_kernel,
        grid=(BATCH, nb + nc),
        in_specs=[qspec, kv(-1), kv(0), kv(1), kv(-1), kv(0), kv(1), ctx_kv, ctx_kv, sink_spec],
        out_specs=qspec,
        out_shape=jax.ShapeDtypeStruct((m, B_QW), F32),
        compiler_params=_params("parallel", "parallel"),
        name="attention",
    )(q, k, k, k, v, v, v, k, v, sink_rows)


def _window_attention(q, k, v, sink):
    sink_rows = jnp.broadcast_to(sink.astype(F32)[:, None], (B_HEADS, 128))
    return _attention(q, k, v, sink_rows)


def _hgrn2(q, f_fw, f_bw, i, g, lb, norm_g):
    lb = lb.reshape(1, A_WIDTH)
    o_fw = _hgrn_pass(False, q, f_fw, i, lb)
    return _hgrn_pass(True, q, f_bw, i, lb, (o_fw, g, norm_g.reshape(1, A_WIDTH)))


S5_STEPS = 64
S5_ROWS = S5_STEPS * BATCH
S5_TILES = (CTX_LEN + SEQ) // S5_STEPS
S5_CTX_TILES = CTX_LEN // S5_STEPS
S5_BLOCKS = 4
S5_BLOCK_IN = C_WIDTH // S5_BLOCKS
S5_BLOCK_STATE = C_GROUPS * C_STATE // S5_BLOCKS


def _s5_kernel(reverse, finish, *refs):
    if finish:
        u_ref, a_ref, wb_ref, wc_ref, yf_ref, glu_ref, y_ref, x_ref, st_ref = refs
    else:
        u_ref, a_ref, wb_ref, wc_ref, d_ref, y_ref, x_ref, st_ref = refs
    ns = S5_BLOCK_STATE

    @pl.when(pl.program_id(0) == 0)
    def _():
        st_ref[...] = jnp.zeros_like(st_ref)

    for k in range(S5_BLOCKS):
        x_ref[:, 2 * ns * k:2 * ns * (k + 1)] = _bdot(u_ref[:, S5_BLOCK_IN * k:S5_BLOCK_IN * (k + 1)], wb_ref[k])

    def step(tt, carry):
        t = (S5_STEPS - 1 - tt) if reverse else tt
        rows = pl.ds(pl.multiple_of(t * BATCH, BATCH), BATCH)
        for k in range(S5_BLOCKS):
            re = slice(2 * ns * k, 2 * ns * k + ns)
            im = slice(2 * ns * k + ns, 2 * ns * (k + 1))
            ar, ai = a_ref[:, re], a_ref[:, im]
            sr, si = st_ref[:, re], st_ref[:, im]
            nr = ar * sr - ai * si + x_ref[rows, re]
            ni = ar * si + ai * sr + x_ref[rows, im]
            st_ref[:, re] = nr
            st_ref[:, im] = ni
            x_ref[rows, re] = nr
            x_ref[rows, im] = ni
        return carry

    lax.fori_loop(0, S5_STEPS, step, 0)

    for k in range(S5_BLOCKS):
        cols = slice(S5_BLOCK_IN * k, S5_BLOCK_IN * (k + 1))
        y = _bdot(x_ref[:, 2 * ns * k:2 * ns * (k + 1)], wc_ref[k])
        if finish:
            y_ref[:, cols] = y + yf_ref[:, cols]
        else:
            y_ref[:, cols] = y + d_ref[:, cols] * u_ref[:, cols]
    if finish:
        y = jax.nn.gelu(y_ref[...])
        y_ref[...] = y * _sigmoid(_bdot(y, glu_ref[...]))


def _s5_tile_order(i, reverse):
    if not reverse:
        return i
    return jnp.where(i < S5_CTX_TILES, S5_CTX_TILES - 1 - i, S5_TILES + S5_CTX_TILES - 1 - i)


def _s5_pass(reverse, u_tm, acoef, wb, wc, extra):
    m, w = u_tm.shape
    nstate = 2 * C_GROUPS * C_STATE
    row_spec = pl.BlockSpec((S5_ROWS, w), lambda i: (_s5_tile_order(i, reverse), 0))
    full = lambda a: pl.BlockSpec(a.shape, lambda i: (0,) * a.ndim)
    finish = reverse
    if finish:
        yf, glu_w = extra
        ins, specs = [u_tm, acoef, wb, wc, yf, glu_w], [row_spec, full(acoef), full(wb), full(wc), row_spec, full(glu_w)]
    else:
        (dskip,) = extra
        ins, specs = [u_tm, acoef, wb, wc, dskip], [row_spec, full(acoef), full(wb), full(wc), full(dskip)]
    return pl.pallas_call(
        functools.partial(_s5_kernel, reverse, finish),
        grid=(S5_TILES,),
        in_specs=specs,
        out_specs=row_spec,
        out_shape=jax.ShapeDtypeStruct((m, w), F32),
        scratch_shapes=[pltpu.VMEM((S5_ROWS, nstate), F32), pltpu.VMEM((BATCH, nstate), F32)],
        compiler_params=_params("arbitrary"),
        name="s5_bwd" if reverse else "s5_fwd",
    )(*ins)


def _s5_discretize(lam_re, lam_im, log_dt, b_re, b_im):
    lam_re = jnp.minimum(lam_re.astype(F32), -1e-4)
    lam_im = lam_im.astype(F32)
    dt = jnp.exp(log_dt.astype(F32))[:, None]
    mag = jnp.exp(lam_re * dt)
    ab_re, ab_im = mag * jnp.cos(lam_im * dt), mag * jnp.sin(lam_im * dt)
    den = lam_re ** 2 + lam_im ** 2
    nr = ab_re - 1.0
    co_re = (nr * lam_re + ab_im * lam_im) / den
    co_im = (ab_im * lam_re - nr * lam_im) / den
    bb_re = co_re[..., None] * b_re - co_im[..., None] * b_im
    bb_im = co_re[..., None] * b_im + co_im[..., None] * b_re
    return ab_re, ab_im, bb_re, bb_im


def _s5_tables(lam_re, lam_im, log_dt, b_re, b_im, c_re, c_im):
    eye = jnp.eye(C_GROUPS // S5_BLOCKS, dtype=F32)
    gb = C_GROUPS // S5_BLOCKS

    def in_map(bb):
        return jnp.einsum('kgph,gG->kghGp', bb.reshape(S5_BLOCKS, gb, C_STATE, C_GROUP), eye).reshape(
            S5_BLOCKS, S5_BLOCK_IN, S5_BLOCK_STATE)

    def out_map(cc):
        return jnp.einsum('kghp,gG->kgpGh', cc.reshape(S5_BLOCKS, gb, C_GROUP, C_STATE), eye).reshape(
            S5_BLOCKS, S5_BLOCK_STATE, S5_BLOCK_IN)

    wc = jnp.concatenate([out_map(c_re.astype(F32)), -out_map(c_im.astype(F32))], axis=1).astype(_MXU_DTYPE)
    tables = []
    for d in range(2):
        ab_re, ab_im, bb_re, bb_im = _s5_discretize(lam_re[d], lam_im[d], log_dt[d], b_re.astype(F32), b_im.astype(F32))
        a = jnp.concatenate([ab_re.reshape(S5_BLOCKS, S5_BLOCK_STATE), ab_im.reshape(S5_BLOCKS, S5_BLOCK_STATE)], axis=1)
        acoef = jnp.broadcast_to(a.reshape(1, -1), (BATCH, 2 * C_GROUPS * C_STATE))
        wb = jnp.concatenate([in_map(bb_re), in_map(bb_im)], axis=2).astype(_MXU_DTYPE)
        tables.append((acoef, wb))
    return tables, wc


def _to_time_major(y):
    w = y.shape[1]
    c = y[:CTX_ROWS].reshape(BATCH, CTX_LEN, w).transpose(1, 0, 2).reshape(CTX_ROWS, w)
    l = y[CTX_ROWS:].reshape(BATCH, SEQ, w).transpose(1, 0, 2).reshape(BATCH * SEQ, w)
    return jnp.concatenate([c, l], axis=0)


def _from_time_major(y):
    w = y.shape[1]
    c = y[:CTX_ROWS].reshape(CTX_LEN, BATCH, w).transpose(1, 0, 2).reshape(CTX_ROWS, w)
    l = y[CTX_ROWS:].reshape(SEQ, BATCH, w).transpose(1, 0, 2).reshape(BATCH * SEQ, w)
    return jnp.concatenate([c, l], axis=0)


def _s5(u, lam_re, lam_im, log_dt, b_re, b_im, c_re, c_im, d_skip, glu_w):
    (fw, bw), wc = _s5_tables(lam_re, lam_im, log_dt, b_re, b_im, c_re, c_im)
    u_tm = _to_time_major(u)
    y_fw = _s5_pass(False, u_tm, fw[0], fw[1], wc, (d_skip.astype(F32).reshape(1, C_WIDTH),))
    y = _s5_pass(True, u_tm, bw[0], bw[1], wc, (y_fw, glu_w.astype(_MXU_DTYPE)))
    return _from_time_major(y)


RW_LORA_OFF = 3 * R_WIDTH


def _softplus(z):
    return jnp.maximum(z, 0.0) + jnp.log1p(jnp.exp(-jnp.abs(z)))


def _rwkv_prep_kernel(p_ref, hp_ref, hn_ref, mu_ref, w0_ref, w2_ref, a0_ref, a2_ref, g2_ref, kk_ref, ka_ref, rk_ref,
                      ones_ref, r_o, v_o, g_o, bonus_o, kkn_o, lwf_o, kdf_o, bf_o, lwb_o, kdb_o, bb_o):
    i = pl.program_id(1)
    x = p_ref[...]
    rows = x.shape[0]
    rowi = lax.broadcasted_iota(jnp.int32, (rows, 1), 0)
    prev_row = jnp.where(i >= 2, hp_ref[7:8, :], 0.0)
    next_row = jnp.where((i >= 1) & (i < SEQ_TILES - 1), hn_ref[0:1, :], 0.0)
    prev = jnp.where(rowi == 0, prev_row, pltpu.roll(x, 1, 0))
    nxt = jnp.where(rowi == rows - 1, next_row, pltpu.roll(x, rows - 1, 0))
    x = x + mu_ref[...] * (0.5 * (prev + nxt) - x)

    r = x[:, 0:R_WIDTH]
    k = x[:, R_WIDTH:2 * R_WIDTH]
    v = x[:, 2 * R_WIDTH:3 * R_WIDTH]
    wd = x[:, RW_LORA_OFF:RW_LORA_OFF + 128]
    ad = x[:, RW_LORA_OFF + 128:RW_LORA_OFF + 256]
    gd = x[:, RW_LORA_OFF + 256:RW_LORA_OFF + 384]
    ones = ones_ref[...]

    r_o[...] = r
    v_o[...] = v
    g_o[...] = _bdot(_sigmoid(gd), g2_ref[...])
    bonus_o[...] = _dot_by_sel(r * k * rk_ref[...], ones) * v
    kk = k * kk_ref[...]
    kkn = kk * jnp.minimum(lax.rsqrt(_dot_by_sel(kk * kk, ones)), 1e12)
    kkn_o[...] = kkn
    tw = jnp.tanh(wd)
    for d, (lw_o, kd_o, b_o) in enumerate(((lwf_o, kdf_o, bf_o), (lwb_o, kdb_o, bb_o))):
        w = -_softplus(-(w0_ref[d:d + 1, :] + _bdot(tw, w2_ref[d]))) - 0.5
        lw_o[...] = -jnp.exp(w)
        a = _sigmoid(a0_ref[d:d + 1, :] + _bdot(ad, a2_ref[d]))
        kd_o[...] = k * (1.0 + (a - 1.0) * ka_ref[...])
        b_o[...] = kkn * a


def _rwkv_prep(p, mu, w0, w2pad, a0, a2pad, g2, k_k, k_a, r_k, ones_blk):
    m, w = p.shape
    hb = ROW_TILE // 8
    seq = lambda width: pl.BlockSpec((ROW_TILE, width), lambda b, i: (_seq_block(b, i), 0))
    halo_prev = pl.BlockSpec((8, w), lambda b, i: (jnp.maximum(_seq_block(b, i) * hb - 1, 0), 0))
    halo_next = pl.BlockSpec((8, w), lambda b, i: (jnp.minimum((_seq_block(b, i) + 1) * hb, m // 8 - 1), 0))
    full = lambda a: pl.BlockSpec(a.shape, lambda b, i: (0,) * a.ndim)
    consts = [mu, w0, w2pad, a0, a2pad, g2, k_k, k_a, r_k, ones_blk]
    return pl.pallas_call(
        _rwkv_prep_kernel,
        grid=(BATCH, SEQ_TILES),
        in_specs=[seq(w), halo_prev, halo_next] + [full(c) for c in consts],
        out_specs=[seq(R_WIDTH)] * 11,
        out_shape=[jax.ShapeDtypeStruct((m, R_WIDTH), F32)] * 11,
        compiler_params=_params("parallel", "parallel"),
        name="rwkv_prep",
    )(p, p, p, *consts)


RW_CHUNK = 64
RW_SEQS = 4


def _rwkv_scan_kernel(reverse, finish, *refs):
    per_seq = 9 if finish else 6
    seq_refs = [refs[q * per_seq:(q + 1) * per_seq] for q in range(RW_SEQS)]
    rest = refs[RW_SEQS * per_seq:]
    if finish:
        lng_ref, lnb_ref = rest[:2]
        rest = rest[2:]
    o_refs, st_ref = rest[:RW_SEQS], rest[RW_SEQS]
    C = RW_CHUNK
    P = 2 * R_HEAD
    n_chunks = ROW_TILE // C

    @pl.when(pl.program_id(1) == 0)
    def _():
        st_ref[...] = jnp.zeros_like(st_ref)

    row = lax.broadcasted_iota(jnp.int32, (C, C), 0)
    col = lax.broadcasted_iota(jnp.int32, (C, C), 1)
    incl = (row <= col) if reverse else (row >= col)
    strict = (row < col) if reverse else (row > col)
    tri = incl.astype(F32)
    row2 = lax.broadcasted_iota(jnp.int32, (C, 2 * C), 0)
    col2 = lax.broadcasted_iota(jnp.int32, (C, 2 * C), 1)
    s2 = jnp.where(col2 < C, col2, col2 - C)
    incl2 = (row2 <= s2) if reverse else (row2 >= s2)
    strict_k = ((row2 < s2) if reverse else (row2 > s2)) & (col2 >= C)
    left = lax.broadcasted_iota(jnp.int32, (C, P), 1) < R_HEAD
    left2 = lax.broadcasted_iota(jnp.int32, (2 * C, P), 1) < R_HEAD
    blockdiag = ((lax.broadcasted_iota(jnp.int32, (P, P), 0) < R_HEAD)
                 == (lax.broadcasted_iota(jnp.int32, (P, P), 1) < R_HEAD))
    steps = int(math.log2(C))

    def chunk(ci, carry):
        c = (n_chunks - 1 - ci) if reverse else ci
        rows = pl.ds(pl.multiple_of(c * C, C), C)
        npair = R_HEADS // 2
        pairs = range(RW_SEQS * npair)
        heads = [(p, half) for p in pairs for half in range(2)]
        lane_sl = [slice((p % npair) * P, (p % npair + 1) * P) for p in pairs]
        sts = [st_ref[p] for p in pairs]
        vps, rhss, ars, tails, decays, fin = [], [], [], [], [], []
        for q in range(RW_SEQS):
            r_ref, kd_ref, v_ref, lw_ref, kkn_ref, b_ref = seq_refs[q][:6]
            lw = lw_ref[rows, :]
            kd, bv, v_all = kd_ref[rows, :], b_ref[rows, :], v_ref[rows, :]
            gi = _dot_sel(tri, lw)
            g_tot = gi[0:1] if reverse else gi[C - 1:C]
            inv = jnp.exp(-gi)
            a_t = -kkn_ref[rows, :] * jnp.exp(gi - lw)
            r_t = r_ref[rows, :] * jnp.exp(gi)
            b_t, k_t = bv * inv, kd * inv
            tail = jnp.exp(g_tot - gi)
            b_tail, k_tail = bv * tail, kd * tail
            decay = jnp.exp(g_tot)
            for j in range(npair):
                sl = lane_sl[j]
                vps.append(v_all[:, sl])
                rhss.append(jnp.concatenate([b_t[:, sl], k_t[:, sl], sts[q * npair + j]], axis=0).astype(_MXU_DTYPE))
                ars.append(jnp.concatenate([a_t[:, sl], r_t[:, sl]], axis=0))
                tails.append(jnp.concatenate([b_tail[:, sl], k_tail[:, sl]], axis=0))
                decays.append(decay[:, sl])
                if finish:
                    of_ref, g_ref, bonus_ref = seq_refs[q][6:9]
                    fin.append((of_ref[rows, sl], g_ref[rows, sl], bonus_ref[rows, sl]))
        zvs = [jnp.concatenate([jnp.zeros((C, P), F32), vp], axis=0).astype(_MXU_DTYPE) for vp in vps]
        prods = [_dot(jnp.where(left2 if half == 0 else jnp.logical_not(left2), ars[p], 0.0).astype(_MXU_DTYPE),
                      rhss[p], trans_b=True) for p, half in heads]
        ahs = [prod[:, 2 * C:] for prod in prods]
        us = [ah[:C] + _dot(jnp.where(strict_k, prod[:C, :2 * C], 0.0).astype(_MXU_DTYPE), zvs[p])
              for (p, half), prod, ah in zip(heads, prods, ahs)]
        npows = [jnp.where(strict, prod[:C, :C], 0.0) for prod in prods]
        for it in range(steps):
            us = [u + _bdot(npow, u) for u, npow in zip(us, npows)]
            if it + 1 < steps:
                npows = [_bdot(npow, npow) for npow in npows]
        os_ = [ah[C:] + _bdot(jnp.where(incl2, prod[C:, :2 * C], 0.0), jnp.concatenate([u, vps[p]], axis=0))
               for (p, half), prod, ah, u in zip(heads, prods, ahs, us)]
        outs = []
        for p in pairs:
            sl, st, vp = lane_sl[p], sts[p], vps[p]
            u_pair = jnp.where(left, us[2 * p], us[2 * p + 1])
            o = jnp.where(left, os_[2 * p], os_[2 * p + 1])
            upd = _bdot(jnp.concatenate([u_pair, vp], axis=0), tails[p], trans_a=True)
            st_ref[p] = jnp.where(blockdiag, st * decays[p] + upd, 0.0)
            if finish:
                of_p, g_p, bonus_p = fin[p]
                o = o + of_p

                def head_mean(t):
                    tot = jnp.sum(t, axis=1, keepdims=True)
                    lsum = jnp.sum(jnp.where(left, t, 0.0), axis=1, keepdims=True)
                    return jnp.where(left, lsum, tot - lsum) * (1.0 / R_HEAD)

                oc = o - head_mean(o)
                o = oc * lax.rsqrt(head_mean(oc * oc) + RWKV_GN_EPS) * lng_ref[:, sl] + lnb_ref[:, sl]
                o = (o + bonus_p) * g_p
            outs.append(o)
        for p, o in enumerate(outs):
            o_refs[p // npair][rows, lane_sl[p]] = o
        return carry

    lax.fori_loop(0, n_chunks, chunk, 0)


def _rwkv_scan(reverse, r, kd, v, lw, kkn, bvec, finish_args=None):
    m, w = r.shape
    nb = BATCH // RW_SEQS

    def seq_spec(q):
        return pl.BlockSpec((ROW_TILE, w), lambda b, t: (_seq_block(b + q * nb, _tile_order(t, reverse)), 0))

    def part_block(b, i):
        return jnp.where(i == 0, b, nb + b * (SEQ // ROW_TILE) + i - 1)

    part_spec = pl.BlockSpec((ROW_TILE, w), lambda b, t: (part_block(b, _tile_order(t, reverse)), 0))
    vec_spec = pl.BlockSpec((1, w), lambda b, t: (0, 0))
    ins, specs = [], []
    for q in range(RW_SEQS):
        ins += [r, kd, v, lw, kkn, bvec]
        specs += [seq_spec(q)] * 6
        if finish_args is not None:
            o_parts, g, bonus = finish_args[:3]
            ins += [o_parts[q], g, bonus]
            specs += [part_spec, seq_spec(q), seq_spec(q)]
    if finish_args is not None:
        ins += list(finish_args[3:])
        specs += [vec_spec, vec_spec]
    return pl.pallas_call(
        functools.partial(_rwkv_scan_kernel, reverse, finish_args is not None),
        grid=(nb, SEQ_TILES),
        in_specs=specs,
        out_specs=[part_spec] * RW_SEQS,
        out_shape=[jax.ShapeDtypeStruct((m // RW_SEQS, w), F32)] * RW_SEQS,
        scratch_shapes=[pltpu.VMEM((RW_SEQS * R_HEADS // 2, 2 * R_HEAD, 2 * R_HEAD), F32)],
        compiler_params=_params("parallel", "arbitrary"),
        name="rwkv_bwd" if reverse else "rwkv_fwd",
    )(*ins)


def _merge_seq_parts(parts):
    c = CTX_ROWS // RW_SEQS
    return jnp.concatenate([p[:c] for p in parts] + [p[c:] for p in parts], axis=0)


def _rwkv7(p, mu, w0, w2, a0, a2, g2, k_k, k_a, r_k, lnx_g, lnx_b):
    zeros_w = jnp.zeros((R_LORA_W, R_WIDTH), F32)
    w2pad = jnp.stack([jnp.concatenate([w2[0], zeros_w], 0), jnp.concatenate([zeros_w, w2[1]], 0)]).astype(_MXU_DTYPE)
    a2pad = jnp.stack([jnp.concatenate([a2[0], zeros_w], 0), jnp.concatenate([zeros_w, a2[1]], 0)]).astype(_MXU_DTYPE)
    head = np.arange(R_WIDTH) // R_HEAD
    ones_blk = jnp.asarray(head[:, None] == head[None, :], F32)
    row = lambda t: t.astype(F32).reshape(1, -1)
    r, v, g, bonus, kkn, lwf, kdf, bf, lwb, kdb, bb = _rwkv_prep(
        p, row(mu), w0.astype(F32), w2pad, a0.astype(F32), a2pad, g2.astype(_MXU_DTYPE),
        row(k_k), row(k_a), row(r_k), ones_blk)
    o_fw = _rwkv_scan(False, r, kdf, v, lwf, kkn, bf)
    return _merge_seq_parts(_rwkv_scan(True, r, kdb, v, lwb, kkn, bb, (o_fw, g, bonus, row(lnx_g), row(lnx_b))))


PROJ_TM = 256
FFN_TM = 512
EVEN_SPLITS = (A_WIDTH,) * 5 + (B_QW, B_KVW, B_KVW)
EVEN_ROTARY = (5, 6)
ODD_SPLITS = (C_WIDTH, R_IN)


def _mod_tiles(vec9, tm):
    idx = np.concatenate([np.full(CTX_ROWS // tm, BATCH), np.repeat(np.arange(BATCH), SEQ // tm)])
    return vec9[idx][:, None, :]


def kernel(x, c, ctx, c_ctx, ada_w, ada_b, ln_g, ln_b, ev_w_in, ev_w_out, hg_lb, hg_norm_g, attn_sink, ffn_w_gate, ffn_w_up, ffn_w_down, od_w_in, od_w_out, s5_lam_re, s5_lam_im, s5_log_dt, s5_b_re, s5_b_im, s5_c_re, s5_c_im, s5_d, s5_glu_w, rwkv_mu, rwkv_w0, rwkv_w2, rwkv_a0, rwkv_a2, rwkv_g2, rwkv_k_k, rwkv_k_a, rwkv_r_k, rwkv_ln_g, rwkv_ln_b, moe_router_w, moe_router_b, moe_w_gate, moe_w_up, moe_w_down):
    d = D_MODEL
    xs = jnp.concatenate([ctx.reshape(CTX_ROWS, d), x.reshape(BATCH * SEQ, d)], axis=0).astype(F32)
    cond = jnp.concatenate([c, c_ctx[None, :], jnp.zeros((16 - BATCH - 1, d), F32)], axis=0)
    ada = _ada_all(cond, ada_w, ada_b)
    lb_soft = jax.nn.softmax(hg_lb.astype(F32), axis=0)
    lb_all = jnp.cumsum(lb_soft, axis=0) - lb_soft[0:1]
    rope_tables = _rope_tables()
    ffn_w = [w.astype(BF16) for w in (ffn_w_gate, ffn_w_up, ffn_w_down)]
    moe_w = [w.astype(BF16) for w in (moe_w_gate, moe_w_up, moe_w_down)]

    for layer in range(DEPTH):
        j = layer // 2
        sh1, sc1, gt1, sh2, sc2, gt2 = [ada[layer, :BATCH + 1, n * d:(n + 1) * d] for n in range(6)]
        if layer % 2 == 0:
            q, f_fw, f_bw, i_in, g, aq, ak, av = _mod_matmul(
                xs, _mod_tiles(1.0 + sc1, PROJ_TM), _mod_tiles(sh1, PROJ_TM), ev_w_in[j].astype(BF16), EVEN_SPLITS, PROJ_TM,
                rotary=EVEN_ROTARY, rope_tables=rope_tables)
            y1 = _hgrn2(q, f_fw, f_bw, i_in, g, lb_all[j], hg_norm_g[j])
            y2 = _window_attention(aq, ak, av, attn_sink[j])
            w_out = ev_w_out[j]
        else:
            u, p_rw = _mod_matmul(
                xs, _mod_tiles(1.0 + sc1, PROJ_TM), _mod_tiles(sh1, PROJ_TM), od_w_in[j].astype(BF16), ODD_SPLITS, PROJ_TM)
            y1 = _s5(u, s5_lam_re[j], s5_lam_im[j], s5_log_dt[j], s5_b_re[j], s5_b_im[j], s5_c_re[j], s5_c_im[j],
                     s5_d[j], s5_glu_w[j])
            y2 = _rwkv7(p_rw, rwkv_mu[j], rwkv_w0[j], rwkv_w2[j], rwkv_a0[j], rwkv_a2[j], rwkv_g2[j],
                        rwkv_k_k[j], rwkv_k_a[j], rwkv_r_k[j], rwkv_ln_g[j], rwkv_ln_b[j])
            w_out = od_w_out[j]
        skip = CTX_ROWS if layer == DEPTH - 1 else 0
        tiles = lambda vec9, tm: _mod_tiles(vec9, tm)[skip // tm:]
        xs = _out_proj_ln(y1, y2, w_out.astype(BF16), xs, tiles(gt1, PROJ_TM), ln_g[layer, 0], ln_b[layer, 0],
                          PROJ_TM, skip_rows=skip)
        scale2, shift2, gate2 = tiles(1.0 + sc2, FFN_TM), tiles(sh2, FFN_TM), tiles(gt2, FFN_TM)
        if layer % 2 == 0:
            xs = _ffn_ln(xs, scale2, shift2, gate2, *ffn_w, j, ln_g[layer, 1], ln_b[layer, 1], FFN_TM)
        else:
            w_pad = jnp.pad(moe_router_w[j].astype(F32), ((0, 0), (0, 128 - N_EXPERTS)))
            b_pad = jnp.pad(moe_router_b[j].astype(F32), (0, 128 - N_EXPERTS)).reshape(1, 128)
            route, sel, h = _router(xs, scale2, shift2, w_pad, b_pad, FFN_TM)
            pos1, pos2, w1, w2, tile_expert, n_used, pad_lo, pad_hi = _moe_plan(route, sel)
            hs = _moe_dispatch(h, pos1, pos2, pad_lo, pad_hi, n_used)
            y_sorted = _moe_experts(hs, tile_expert, n_used, *moe_w, j)
            xs = _moe_combine_ln(y_sorted, pos1, pos2, w1, w2, xs, tiles(gt2, MOE_COMBINE_TM),
                                 ln_g[layer, 1], ln_b[layer, 1])
    return xs.reshape(BATCH, SEQ, d)
```

```python
import functools
import math

import numpy as np
import jax
import jax.numpy as jnp
from jax import lax
from jax.experimental import pallas as pl
from jax.experimental.pallas import tpu as pltpu

F32 = jnp.float32
BF16 = jnp.bfloat16
HIGHEST = lax.Precision.HIGHEST

D_MODEL = 1024
BATCH = 8
SEQ = 2048
CTX_LEN = 256
DEPTH = 4
GRID_W = 64
ROW_TILE = 256
SEQ_TILES = (CTX_LEN + SEQ) // ROW_TILE
CTX_ROWS = BATCH * CTX_LEN
M_ROWS = BATCH * (CTX_LEN + SEQ)

A_HEADS, A_DK, A_WIDTH = 4, 128, 512
B_HEADS, B_KV_HEADS, B_HEAD_DIM = 8, 2, 64
B_QW, B_KVW = 512, 128
B_BLOCK = 128
ROPE_BASE = 10000.0
C_GROUP, C_GROUPS, C_WIDTH, C_STATE = 16, 32, 512, 64
R_HEADS, R_HEAD, R_WIDTH = 8, 64, 512
R_LORA_W, R_LORA_A, R_LORA_G = 64, 64, 128
R_IN = 3 * R_WIDTH + 2 * R_LORA_W + 2 * R_LORA_A + R_LORA_G
D_FF = 2816
N_EXPERTS = 8
DEEPNORM_ALPHA = (2 * DEPTH) ** 0.25
LN_EPS = 1e-5
RWKV_GN_EPS = 64e-5

VMEM_LIMIT_BYTES = 56 * 1024 * 1024


def _params(*sem):
    return pltpu.CompilerParams(dimension_semantics=sem, vmem_limit_bytes=VMEM_LIMIT_BYTES)


def _dot(a, b, *, trans_a=False, trans_b=False, precision=None):
    dn = (((0 if trans_a else 1,), (1 if trans_b else 0,)), ((), ()))
    return lax.dot_general(a, b, dn, preferred_element_type=F32, precision=precision)


_MXU_DTYPE = BF16


def _bdot(a, b, **kw):
    return _dot(a.astype(_MXU_DTYPE), b.astype(_MXU_DTYPE), **kw)


def _split3(x):
    x1 = x.astype(BF16)
    r1 = x - x1.astype(F32)
    x2 = r1.astype(BF16)
    x3 = (r1 - x2.astype(F32)).astype(BF16)
    return x1, x2, x3


def _dot_sel(sel, x):
    s = sel.astype(BF16)
    x1, x2, x3 = _split3(x)
    return _dot(s, x1) + _dot(s, x2) + _dot(s, x3)


def _dot_by_sel(x, sel):
    s = sel.astype(BF16)
    x1, x2, x3 = _split3(x)
    return _dot(x1, s) + _dot(x2, s) + _dot(x3, s)


def _sigmoid(x):
    return jax.nn.sigmoid(x)


def _silu(x):
    return x * _sigmoid(x)


def _seq_block(b, i):
    return jnp.where(i == 0, b, BATCH + b * (SEQ // ROW_TILE) + i - 1)


def _tile_order(i, reverse):
    if not reverse:
        return i
    return jnp.where(i == 0, 0, SEQ_TILES - i)


def _ada_kernel(cond_ref, w_ref, b_ref, o_ref):
    o_ref[0] = _dot(_silu(cond_ref[...]), w_ref[0], precision=HIGHEST) + b_ref[0]


def _ada_all(cond, ada_w, ada_b):
    rows = cond.shape[0]
    tn = 1536
    return pl.pallas_call(
        _ada_kernel,
        grid=(DEPTH, 6 * D_MODEL // tn),
        in_specs=[pl.BlockSpec((rows, D_MODEL), lambda l, j: (0, 0)),
                  pl.BlockSpec((1, D_MODEL, tn), lambda l, j: (l, 0, j)),
                  pl.BlockSpec((1, 1, tn), lambda l, j: (l, 0, j))],
        out_specs=pl.BlockSpec((1, rows, tn), lambda l, j: (l, 0, j)),
        out_shape=jax.ShapeDtypeStruct((DEPTH, rows, 6 * D_MODEL), F32),
        compiler_params=_params("parallel", "parallel"),
        name="ada",
    )(cond, ada_w, ada_b.reshape(DEPTH, 1, 6 * D_MODEL))


def _mod_matmul_kernel(splits, rotary, x_ref, sc_ref, sh_ref, w_ref, *refs):
    if rotary:
        cos, sa, sb = refs[0][...], refs[1][...], refs[2][...]
        refs = refs[3:]
    h = (x_ref[...] * sc_ref[0] + sh_ref[0]).astype(BF16)
    off = 0
    for n, (o_ref, width) in enumerate(zip(refs, splits)):
        y = _dot(h, w_ref[:, off:off + width])
        if n in rotary:
            for j in range(width // 128):
                x = y[:, j * 128:(j + 1) * 128]
                o_ref[:, j * 128:(j + 1) * 128] = x * cos + pltpu.roll(x, 96, 1) * sa + pltpu.roll(x, 32, 1) * sb
        else:
            o_ref[...] = y
        off += width


def _mod_matmul(x, scale_t, shift_t, w_bf16, splits, tm, rotary=(), rope_tables=None):
    m, d = x.shape
    n = w_bf16.shape[1]
    assert sum(splits) == n and m % tm == 0
    ins = [x, scale_t, shift_t, w_bf16]
    specs = [pl.BlockSpec((tm, d), lambda i: (i, 0)),
             pl.BlockSpec((1, 1, d), lambda i: (i, 0, 0)),
             pl.BlockSpec((1, 1, d), lambda i: (i, 0, 0)),
             pl.BlockSpec((d, n), lambda i: (0, 0))]
    if rotary:
        assert tm == ROW_TILE
        ctx_tiles, lat_tiles = CTX_ROWS // tm, SEQ // tm
        tab = pl.BlockSpec((tm, 128), lambda i: (jnp.where(i < ctx_tiles, 0, (i - ctx_tiles) % lat_tiles + 1), 0))
        ins += list(rope_tables)
        specs += [tab, tab, tab]
    return pl.pallas_call(
        functools.partial(_mod_matmul_kernel, splits, tuple(rotary)),
        grid=(m // tm,),
        in_specs=specs,
        out_specs=[pl.BlockSpec((tm, w), lambda i: (i, 0)) for w in splits],
        out_shape=[jax.ShapeDtypeStruct((m, w), F32) for w in splits],
        compiler_params=_params("parallel"),
        name="mod_matmul",
    )(*ins)


def _layer_norm_rows(z, g, b):
    mu = jnp.mean(z, axis=-1, keepdims=True)
    zc = z - mu
    var = jnp.mean(zc * zc, axis=-1, keepdims=True)
    return zc * lax.rsqrt(var + LN_EPS) * g + b


def _out_proj_kernel(y1_ref, y2_ref, w_ref, x_ref, gt_ref, g_ref, b_ref, o_ref):
    k1 = y1_ref.shape[1]
    proj = _bdot(y1_ref[...], w_ref[:k1, :]) + _bdot(y2_ref[...], w_ref[k1:, :])
    z = DEEPNORM_ALPHA * x_ref[...] + gt_ref[0] * proj
    o_ref[...] = _layer_norm_rows(z, g_ref[...], b_ref[...])


def _out_proj_ln(y1, y2, w_bf16, x, gate_t, ln_g, ln_b, tm, skip_rows=0):
    m, d = x.shape
    k1, k2 = y1.shape[1], y2.shape[1]
    off = skip_rows // tm
    return pl.pallas_call(
        _out_proj_kernel,
        grid=((m - skip_rows) // tm,),
        in_specs=[pl.BlockSpec((tm, k1), lambda i: (i + off, 0)),
                  pl.BlockSpec((tm, k2), lambda i: (i + off, 0)),
                  pl.BlockSpec((k1 + k2, d), lambda i: (0, 0)),
                  pl.BlockSpec((tm, d), lambda i: (i + off, 0)),
                  pl.BlockSpec((1, 1, d), lambda i: (i, 0, 0)),
                  pl.BlockSpec((1, d), lambda i: (0, 0)),
                  pl.BlockSpec((1, d), lambda i: (0, 0))],
        out_specs=pl.BlockSpec((tm, d), lambda i: (i, 0)),
        out_shape=jax.ShapeDtypeStruct((m - skip_rows, d), F32),
        compiler_params=_params("parallel"),
        name="out_proj_ln",
    )(y1, y2, w_bf16, x, gate_t, ln_g.reshape(1, d), ln_b.reshape(1, d))


FF_CHUNK = 256


def _swiglu_rows(h, wg_ref, wu_ref, wd_ref):
    ff = wg_ref.shape[1]
    acc = jnp.zeros((h.shape[0], wd_ref.shape[1]), F32)
    pending = None
    for c in range(ff // FF_CHUNK):
        cols = slice(c * FF_CHUNK, (c + 1) * FF_CHUNK)
        g, u = _dot(h, wg_ref[:, cols]), _dot(h, wu_ref[:, cols])
        if pending is not None:
            acc = acc + _dot(pending[0], wd_ref[pending[1], :])
        pending = ((_silu(g) * u).astype(BF16), cols)
    return acc + _dot(pending[0], wd_ref[pending[1], :])


def _ffn_kernel(x_ref, sc_ref, sh_ref, gt_ref, wg_ref, wu_ref, wd_ref, g_ref, b_ref, o_ref):
    x = x_ref[...]
    f = _swiglu_rows((x * sc_ref[0] + sh_ref[0]).astype(BF16), wg_ref, wu_ref, wd_ref)
    o_ref[...] = _layer_norm_rows(DEEPNORM_ALPHA * x + gt_ref[0] * f, g_ref[...], b_ref[...])


def _ffn_ln(x, scale_t, shift_t, gate_t, wg, wu, wd, layer, ln_g, ln_b, tm):
    m, d = x.shape
    ff = wg.shape[2]
    return pl.pallas_call(
        _ffn_kernel,
        grid=(m // tm,),
        in_specs=[pl.BlockSpec((tm, d), lambda i: (i, 0)),
                  pl.BlockSpec((1, 1, d), lambda i: (i, 0, 0)),
                  pl.BlockSpec((1, 1, d), lambda i: (i, 0, 0)),
                  pl.BlockSpec((1, 1, d), lambda i: (i, 0, 0)),
                  pl.BlockSpec((None, d, ff), lambda i: (layer, 0, 0)),
                  pl.BlockSpec((None, d, ff), lambda i: (layer, 0, 0)),
                  pl.BlockSpec((None, ff, d), lambda i: (layer, 0, 0)),
                  pl.BlockSpec((1, d), lambda i: (0, 0)),
                  pl.BlockSpec((1, d), lambda i: (0, 0))],
        out_specs=pl.BlockSpec((tm, d), lambda i: (i, 0)),
        out_shape=jax.ShapeDtypeStruct((m, d), F32),
        compiler_params=_params("parallel"),
        name="ffn_ln",
    )(x, scale_t, shift_t, gate_t, wg, wu, wd, ln_g.reshape(1, d), ln_b.reshape(1, d))


def _router_kernel(x_ref, sc_ref, sh_ref, w_ref, b_ref, o_ref, sel_ref, h_ref):
    h = x_ref[...] * sc_ref[0] + sh_ref[0]
    h_ref[...] = h
    logits = _dot(h, w_ref[...], precision=HIGHEST) + b_ref[...]
    lane = lax.broadcasted_iota(jnp.int32, logits.shape, 1)
    neg = jnp.float32(-jnp.inf)
    logits = jnp.where(lane < N_EXPERTS, logits, neg)
    v1 = jnp.max(logits, axis=1, keepdims=True)
    i1 = jnp.min(jnp.where(logits == v1, lane, 128), axis=1, keepdims=True)
    rest = jnp.where(lane == i1, neg, logits)
    v2 = jnp.max(rest, axis=1, keepdims=True)
    i2 = jnp.min(jnp.where(rest == v2, lane, 128), axis=1, keepdims=True)
    e2 = jnp.exp(v2 - v1)
    p1 = 1.0 / (1.0 + e2)
    p2 = e2 / (1.0 + e2)
    o_ref[...] = jnp.where(lane == i1, p1, 0.0) + jnp.where(lane == i2, p2, 0.0)
    sel_ref[...] = ((lane == i1) | (lane == i2)).astype(F32)


def _router(x, scale_t, shift_t, w_pad, b_pad, tm):
    m, d = x.shape
    lanes = pl.BlockSpec((tm, 128), lambda i: (i, 0))
    return pl.pallas_call(
        _router_kernel,
        grid=(m // tm,),
        in_specs=[pl.BlockSpec((tm, d), lambda i: (i, 0)),
                  pl.BlockSpec((1, 1, d), lambda i: (i, 0, 0)),
                  pl.BlockSpec((1, 1, d), lambda i: (i, 0, 0)),
                  pl.BlockSpec((d, 128), lambda i: (0, 0)),
                  pl.BlockSpec((1, 128), lambda i: (0, 0))],
        out_specs=[lanes, lanes, pl.BlockSpec((tm, d), lambda i: (i, 0))],
        out_shape=[jax.ShapeDtypeStruct((m, 128), F32), jax.ShapeDtypeStruct((m, 128), F32),
                   jax.ShapeDtypeStruct((m, d), F32)],
        compiler_params=_params("parallel"),
        name="router",
    )(x, scale_t, shift_t, w_pad, b_pad)


MOE_TM = 512
MOE_COMBINE_TM = 1024
MOE_DISPATCH_TM = 1024


def _moe_rows(m):
    return 2 * m + N_EXPERTS * MOE_TM


def _moe_plan(route, sel):
    moe_rows = _moe_rows(route.shape[0])
    sel8 = sel[:, :N_EXPERTS].astype(jnp.int32)
    counts = jnp.sum(sel8, axis=0)
    rank = jnp.cumsum(sel8, axis=0) - sel8
    padded = ((counts + MOE_TM - 1) // MOE_TM) * MOE_TM
    ends = jnp.cumsum(padded)
    starts = ends - padded
    pos = jnp.where(sel8 > 0, starts[None, :] + rank, moe_rows)
    pos1 = jnp.min(pos, axis=1)
    pos2 = jnp.min(jnp.where(pos == pos1[:, None], moe_rows, pos), axis=1)
    route8 = route[:, :N_EXPERTS]
    w1 = jnp.sum(jnp.where(pos == pos1[:, None], route8, 0.0), axis=1, keepdims=True)
    w2 = jnp.sum(jnp.where(pos == pos2[:, None], route8, 0.0), axis=1, keepdims=True)
    tile_start = jnp.arange(moe_rows // MOE_TM, dtype=jnp.int32) * MOE_TM
    tile_expert = jnp.minimum(jnp.sum((tile_start[:, None] >= ends[None, :]).astype(jnp.int32), axis=1),
                              N_EXPERTS - 1).astype(jnp.int32)
    n_used = (ends[-1] // MOE_TM).astype(jnp.int32).reshape(1)
    pad_lo, pad_hi = (starts + counts).astype(jnp.int32), ends.astype(jnp.int32)
    return pos1.astype(jnp.int32), pos2.astype(jnp.int32), w1, w2, tile_expert, n_used, pad_lo, pad_hi


def _moe_dispatch_kernel(p1_ref, p2_ref, lo_ref, hi_ref, nused_ref, h_ref, hs_hbm, zero_ref, sem, zsem):
    i = pl.program_id(0)
    base = i * MOE_DISPATCH_TM

    def row_copy(r, dst_row):
        return pltpu.make_async_copy(h_ref.at[pl.ds(r, 1), :], hs_hbm.at[pl.ds(dst_row, 1), :], sem)

    def start(r, carry):
        row_copy(r, p1_ref[base + r]).start(priority=0)
        row_copy(r, p2_ref[base + r]).start(priority=1)
        return carry

    def wait(r, carry):
        row_copy(0, 0).wait()
        row_copy(0, 0).wait()
        return carry

    lax.fori_loop(0, MOE_DISPATCH_TM, start, 0, unroll=8)
    lax.fori_loop(0, MOE_DISPATCH_TM, wait, 0, unroll=8)

    @pl.when(i == pl.num_programs(0) - 1)
    def _():
        zero_ref[...] = jnp.zeros_like(zero_ref)

        def zero_copy(dst_row):
            return pltpu.make_async_copy(zero_ref.at[pl.ds(0, 1), :], hs_hbm.at[pl.ds(dst_row, 1), :], zsem)

        for e in range(N_EXPERTS):
            lo, hi = lo_ref[e], hi_ref[e]

            def zstart(r, carry):
                zero_copy(r).start()
                return carry

            def zwait(r, carry):
                zero_copy(0).wait()
                return carry

            lax.fori_loop(lo, hi, zstart, 0)
            lax.fori_loop(lo, hi, zwait, 0)

        def tile_copy(t):
            return pltpu.make_async_copy(zero_ref, hs_hbm.at[pl.ds(t * MOE_TM, MOE_TM), :], zsem)

        def tstart(t, carry):
            tile_copy(t).start()
            return carry

        def twait(t, carry):
            tile_copy(0).wait()
            return carry

        n_tiles = hs_hbm.shape[0] // MOE_TM
        lax.fori_loop(nused_ref[0], n_tiles, tstart, 0)
        lax.fori_loop(nused_ref[0], n_tiles, twait, 0)


def _moe_dispatch(h, pos1, pos2, pad_lo, pad_hi, n_used):
    m, d = h.shape
    grid_spec = pltpu.PrefetchScalarGridSpec(
        num_scalar_prefetch=5,
        grid=(m // MOE_DISPATCH_TM,),
        in_specs=[pl.BlockSpec((MOE_DISPATCH_TM, d), lambda i, *_: (i, 0))],
        out_specs=pl.BlockSpec(memory_space=pl.ANY),
        scratch_shapes=[pltpu.VMEM((MOE_TM, d), F32), pltpu.SemaphoreType.DMA(()), pltpu.SemaphoreType.DMA(())],
    )
    return pl.pallas_call(
        _moe_dispatch_kernel,
        grid_spec=grid_spec,
        out_shape=jax.ShapeDtypeStruct((_moe_rows(m), d), F32),
        compiler_params=_params("arbitrary"),
        name="moe_dispatch",
    )(pos1, pos2, pad_lo, pad_hi, n_used, h)


def _moe_expert_kernel(texp_ref, nused_ref, hs_ref, wg_ref, wu_ref, wd_ref, y_ref):
    t = pl.program_id(0)

    @pl.when(t < nused_ref[0])
    def _():
        y_ref[...] = _swiglu_rows(hs_ref[...].astype(BF16), wg_ref, wu_ref, wd_ref)

    @pl.when(t >= nused_ref[0])
    def _():
        y_ref[...] = jnp.zeros_like(y_ref)


def _moe_experts(hs, tile_expert, n_used, wg, wu, wd, layer):
    moe_rows, d = hs.shape
    ff = wg.shape[3]
    grid_spec = pltpu.PrefetchScalarGridSpec(
        num_scalar_prefetch=2,
        grid=(moe_rows // MOE_TM,),
        in_specs=[pl.BlockSpec((MOE_TM, d), lambda t, te, nu: (jnp.minimum(t, nu[0] - 1), 0)),
                  pl.BlockSpec((None, None, d, ff), lambda t, te, nu: (layer, te[t], 0, 0)),
                  pl.BlockSpec((None, None, d, ff), lambda t, te, nu: (layer, te[t], 0, 0)),
                  pl.BlockSpec((None, None, ff, d), lambda t, te, nu: (layer, te[t], 0, 0))],
        out_specs=pl.BlockSpec((MOE_TM, d), lambda t, te, nu: (t, 0)),
    )
    return pl.pallas_call(
        _moe_expert_kernel,
        grid_spec=grid_spec,
        out_shape=jax.ShapeDtypeStruct((moe_rows, d), F32),
        compiler_params=_params("arbitrary"),
        name="moe_experts",
    )(tile_expert, n_used, hs, wg, wu, wd)


def _moe_combine_kernel(p1_ref, p2_ref, y_hbm, x_ref, w1_ref, w2_ref, gt_ref, g_ref, b_ref, o_ref, buf1, buf2, sem):
    base = pl.program_id(0) * MOE_COMBINE_TM

    def row_copy(buf, k, r, src_row):
        return pltpu.make_async_copy(y_hbm.at[pl.ds(src_row, 1), :], buf.at[pl.ds(r, 1), :], sem.at[k])

    def start(r, carry):
        row_copy(buf1, 0, r, p1_ref[base + r]).start(priority=0)
        row_copy(buf2, 1, r, p2_ref[base + r]).start(priority=1)
        return carry

    def wait(r, carry):
        row_copy(buf1, 0, 0, 0).wait()
        row_copy(buf2, 1, 0, 0).wait()
        return carry

    lax.fori_loop(0, MOE_COMBINE_TM, start, 0, unroll=8)
    lax.fori_loop(0, MOE_COMBINE_TM, wait, 0, unroll=8)
    f = w1_ref[...] * buf1[...] + w2_ref[...] * buf2[...]
    z = DEEPNORM_ALPHA * x_ref[...] + gt_ref[0] * f
    o_ref[...] = _layer_norm_rows(z, g_ref[...], b_ref[...])


def _moe_combine_ln(y_sorted, pos1, pos2, w1, w2, x, gate_t, ln_g, ln_b):
    m, d = x.shape
    tm = MOE_COMBINE_TM
    grid_spec = pltpu.PrefetchScalarGridSpec(
        num_scalar_prefetch=2,
        grid=(m // tm,),
        in_specs=[pl.BlockSpec(memory_space=pl.ANY),
                  pl.BlockSpec((tm, d), lambda i, p1, p2: (i, 0)),
                  pl.BlockSpec((tm, 1), lambda i, p1, p2: (i, 0)),
                  pl.BlockSpec((tm, 1), lambda i, p1, p2: (i, 0)),
                  pl.BlockSpec((1, 1, d), lambda i, p1, p2: (i, 0, 0)),
                  pl.BlockSpec((1, d), lambda i, p1, p2: (0, 0)),
                  pl.BlockSpec((1, d), lambda i, p1, p2: (0, 0))],
        out_specs=pl.BlockSpec((tm, d), lambda i, p1, p2: (i, 0)),
        scratch_shapes=[pltpu.VMEM((tm, d), F32), pltpu.VMEM((tm, d), F32), pltpu.SemaphoreType.DMA((2,))],
    )
    return pl.pallas_call(
        _moe_combine_kernel,
        grid_spec=grid_spec,
        out_shape=jax.ShapeDtypeStruct((m, d), F32),
        compiler_params=_params("arbitrary"),
        name="moe_combine_ln",
    )(pos1, pos2, y_sorted, x, w1, w2, gate_t, ln_g.reshape(1, d), ln_b.reshape(1, d))


HGRN_CHUNK = 16
HGRN_SUPER = 128


def _hgrn_kernel(reverse, finish, *refs):
    if finish:
        q_ref, f_ref, i_ref, lb_ref, of_ref, g_ref, ng_ref, o_ref, st_ref = refs
    else:
        q_ref, f_ref, i_ref, lb_ref, o_ref, st_ref = refs
    C, S = HGRN_CHUNK, HGRN_SUPER
    nsub = S // C
    n_super = ROW_TILE // S

    @pl.when(pl.program_id(1) == 0)
    def _():
        st_ref[...] = jnp.zeros_like(st_ref)

    row = lax.broadcasted_iota(jnp.int32, (S, S), 0)
    col = lax.broadcasted_iota(jnp.int32, (S, S), 1)
    blk_r, blk_c = jnp.zeros_like(row), jnp.zeros_like(col)
    for j in range(1, nsub):
        blk_r = blk_r + (row >= j * C).astype(jnp.int32)
        blk_c = blk_c + (col >= j * C).astype(jnp.int32)
    H = C // 2
    hi_r, hi_c = (row - blk_r * C) >= H, (col - blk_c * C) >= H
    col_in = col - blk_c * C - jnp.where(hi_c, H, 0)
    ordered = (row <= col) if reverse else (row >= col)
    same_chunk = blk_r == blk_c
    tri = (same_chunk & ordered).astype(F32)
    keep = same_chunk & (hi_r == hi_c) & ordered
    cross = same_chunk & ((jnp.logical_not(hi_r) & hi_c) if reverse else (hi_r & jnp.logical_not(hi_c)))
    heads = range(A_HEADS)
    sls = [slice(h * A_DK, (h + 1) * A_DK) for h in heads]
    chunk_rows = [slice(c * C, (c + 1) * C) for c in range(nsub)]
    width = A_DK

    def per_chunk_row(t, offset):
        return jnp.concatenate(
            [jnp.broadcast_to(t[c * C + offset:c * C + offset + 1], (C, width)) for c in range(nsub)], axis=0)

    def per_half_row(t, offset):
        return jnp.concatenate(
            [jnp.broadcast_to(t[g * H + offset:g * H + offset + 1], (H, width)) for g in range(S // H)], axis=0)

    def superchunk(si, carry):
        sc = (n_super - 1 - si) if reverse else si
        rows = pl.ds(pl.multiple_of(sc * S, S), S)
        last = 0 if reverse else C - 1
        vs, qes, kts, qxs, kxs, atts, bs = [], [], [], [], [], [], []
        for sl in sls:
            lb = lb_ref[:, sl]
            q = _silu(q_ref[rows, sl]) * A_DK ** -0.5
            f = lb + (1.0 - lb) * _sigmoid(f_ref[rows, sl])
            k = 1.0 - f
            b = _dot_sel(tri, jnp.log(f))
            b_tot = per_chunk_row(b, last)
            b_mid = per_chunk_row(b, H if reverse else H - 1)
            att = jnp.zeros((S, S), F32)
            for s in range(H):
                tmp = q * jnp.exp(b - per_half_row(b, s)) * per_half_row(k, s)
                att = jnp.where(col_in == s, jnp.sum(tmp, axis=1, keepdims=True), att)
            vs.append(i_ref[rows, sl])
            qes.append(q * jnp.exp(b))
            kts.append(k * jnp.exp(b_tot - b))
            qxs.append(q * jnp.exp(b - b_mid))
            kxs.append(k * jnp.exp(b_mid - b))
            atts.append(att)
            bs.append(b)
        xatts = [_bdot(qx, kx, trans_b=True) for qx, kx in zip(qxs, kxs)]
        intra = [_bdot(jnp.where(keep, att, 0.0) + jnp.where(cross, xatt, 0.0), v)
                 for v, att, xatt in zip(vs, atts, xatts)]
        upds = [[_bdot(v[cr], kt[cr], trans_a=True) for cr in chunk_rows] for v, kt in zip(vs, kts)]
        sts = [st_ref[h] for h in heads]
        inter = [[None] * nsub for _ in heads]
        for c in (range(nsub - 1, -1, -1) if reverse else range(nsub)):
            cr = chunk_rows[c]
            for h in heads:
                inter[h][c] = _bdot(qes[h][cr], sts[h], trans_b=True)
                sts[h] = sts[h] * jnp.exp(bs[h][c * C + last:c * C + last + 1]) + upds[h][c]
        outs = []
        for h in heads:
            st_ref[h] = sts[h]
            o = intra[h] + jnp.concatenate(inter[h], axis=0)
            if finish:
                o = o + of_ref[rows, sls[h]]
                o = o * lax.rsqrt(jnp.mean(o * o, axis=1, keepdims=True) + 1e-6)
                o = o * ng_ref[:, sls[h]] * _silu(g_ref[rows, sls[h]])
            outs.append(o)
        for h in heads:
            o_ref[rows, sls[h]] = outs[h]
        return carry

    lax.fori_loop(0, n_super, superchunk, 0)


def _hgrn_pass(reverse, q, f, i, lb, finish_args=None):
    m, w = q.shape
    seq_spec = pl.BlockSpec((ROW_TILE, w), lambda b, t: (_seq_block(b, _tile_order(t, reverse)), 0))
    vec_spec = pl.BlockSpec((1, w), lambda b, t: (0, 0))
    ins = [q, f, i, lb]
    specs = [seq_spec, seq_spec, seq_spec, vec_spec]
    if finish_args is not None:
        of, g, ng = finish_args
        ins += [of, g, ng]
        specs += [seq_spec, seq_spec, vec_spec]
    return pl.pallas_call(
        functools.partial(_hgrn_kernel, reverse, finish_args is not None),
        grid=(BATCH, SEQ_TILES),
        in_specs=specs,
        out_specs=seq_spec,
        out_shape=jax.ShapeDtypeStruct((m, w), F32),
        scratch_shapes=[pltpu.VMEM((A_HEADS, A_DK, A_DK), F32)],
        compiler_params=_params("parallel", "arbitrary"),
        name="hgrn_bwd" if reverse else "hgrn_fwd",
    )(*ins)


def _rope_tables():
    rows = SEQ // GRID_W
    row = jnp.repeat(jnp.arange(rows, dtype=F32), GRID_W)
    colp = jnp.tile(jnp.arange(GRID_W, dtype=F32), rows)
    n_freq = B_HEAD_DIM // 4
    inv = ROPE_BASE ** (-jnp.arange(n_freq, dtype=F32) / n_freq)
    ang = jnp.concatenate([row[:, None] * inv, colp[:, None] * inv], axis=-1)
    cos, sin = jnp.cos(ang), jnp.sin(ang)
    zero = jnp.zeros_like(sin)
    cos_l = jnp.tile(cos, (1, 4))
    sa_l = jnp.tile(jnp.concatenate([-sin, zero], axis=-1), (1, 2))
    sb_l = jnp.tile(jnp.concatenate([zero, sin], axis=-1), (1, 2))
    pad = lambda t, v: jnp.concatenate([jnp.full((CTX_LEN, 128), v, F32), t], axis=0)
    return pad(cos_l, 1.0), pad(sa_l, 0.0), pad(sb_l, 0.0)


def _attend(q_ref, o_ref, sink_ref, k, v, mask):
    tq = q_ref.shape[0]
    left = lax.broadcasted_iota(jnp.int32, (tq, 128), 1) < B_HEAD_DIM
    lk = lax.broadcasted_iota(jnp.int32, k.shape, 1) < B_HEAD_DIM
    neg = jnp.float32(-jnp.inf)
    scale = B_HEAD_DIM ** -0.5
    kr, vr = pltpu.roll(k, 64, 1), pltpu.roll(v, 64, 1)
    kds = [jnp.where(lk, k, kr).astype(_MXU_DTYPE), jnp.where(lk, kr, k).astype(_MXU_DTYPE)]
    vds = [jnp.where(lk, v, vr), jnp.where(lk, vr, v)]
    vsel = [[jnp.where(lk if half == 0 else jnp.logical_not(lk), vd, 0.0).astype(_MXU_DTYPE) for half in range(2)]
            for vd in vds]
    pairs = B_HEADS // 2
    heads = [(p, half) for p in range(pairs) for half in range(2)]
    group = lambda p: p // (pairs // B_KV_HEADS)
    qms = [jnp.where(left if half == 0 else jnp.logical_not(left), q_ref[:, p * 128:(p + 1) * 128] * scale, 0.0)
           .astype(_MXU_DTYPE) for p, half in heads]
    scores = [_dot(qm, kds[group(p)], trans_b=True) for (p, half), qm in zip(heads, qms)]
    if mask is not None:
        scores = [jnp.where(mask, s, neg) for s in scores]
    sinks = [sink_ref[2 * p + half:2 * p + half + 1, 0:1] for p, half in heads]
    mxs = [jnp.maximum(sink, jnp.max(s, axis=1, keepdims=True)) for s, sink in zip(scores, sinks)]
    es = [jnp.exp(s - mx) for s, mx in zip(scores, mxs)]
    denoms = [jnp.exp(sink - mx) + jnp.sum(e, axis=1, keepdims=True) for e, mx, sink in zip(es, mxs, sinks)]
    accs = [_dot(e.astype(_MXU_DTYPE), vsel[group(p)][half]) for (p, half), e in zip(heads, es)]
    for p in range(pairs):
        o_ref[:, p * 128:(p + 1) * 128] = accs[2 * p] / denoms[2 * p] + accs[2 * p + 1] / denoms[2 * p + 1]


ATTN_LATENT_BLOCKS = SEQ // B_BLOCK
ATTN_CTX_BLOCKS = CTX_LEN // B_BLOCK


def _attn_kernel(q_ref, kp_ref, kc_ref, kn_ref, vp_ref, vc_ref, vn_ref, kx_ref, vx_ref, sink_ref, o_ref):
    n = pl.program_id(1)
    nb = ATTN_LATENT_BLOCKS

    @pl.when(n < nb)
    def _():
        row = lax.broadcasted_iota(jnp.int32, (B_BLOCK, B_BLOCK), 0)
        col = lax.broadcasted_iota(jnp.int32, (B_BLOCK, B_BLOCK), 1)
        mask = jnp.concatenate([(col >= row) & (n > 0), jnp.ones((B_BLOCK, B_BLOCK), jnp.bool_),
                                (col <= row) & (n < nb - 1), jnp.ones((B_BLOCK, CTX_LEN), jnp.bool_)], axis=1)
        k = jnp.concatenate([kp_ref[...], kc_ref[...], kn_ref[...], kx_ref[...]], axis=0)
        v = jnp.concatenate([vp_ref[...], vc_ref[...], vn_ref[...], vx_ref[...]], axis=0)
        _attend(q_ref, o_ref, sink_ref, k, v, mask)

    @pl.when(n >= nb)
    def _():
        _attend(q_ref, o_ref, sink_ref, kx_ref[...], vx_ref[...], None)


def _attention(q, k, v, sink_rows):
    m = q.shape[0]
    nb, nc = ATTN_LATENT_BLOCKS, ATTN_CTX_BLOCKS
    base = CTX_ROWS // B_BLOCK

    def q_block(b, n):
        return jnp.where(n < nb, base + b * nb + n, b * nc + n - nb)

    qspec = pl.BlockSpec((B_BLOCK, B_QW), lambda b, n: (q_block(b, n), 0))

    def kv(shift):
        return pl.BlockSpec((B_BLOCK, B_KVW), lambda b, n: (base + b * nb + jnp.clip(n + shift, 0, nb - 1), 0))

    ctx_kv = pl.BlockSpec((CTX_LEN, B_KVW), lambda b, n: (b, 0))
    sink_spec = pl.BlockSpec((B_HEADS, 128), lambda b, n: (0, 0))
    return pl.pallas_call(
        _attn_kernel,
        grid=(BATCH, nb + nc),
        in_specs=[qspec, kv(-1), kv(0), kv(1), kv(-1), kv(0), kv(1), ctx_kv, ctx_kv, sink_spec],
        out_specs=qspec,
        out_shape=jax.ShapeDtypeStruct((m, B_QW), F32),
        compiler_params=_params("parallel", "parallel"),
        name="attention",
    )(q, k, k, k, v, v, v, k, v, sink_rows)


def _window_attention(q, k, v, sink):
    sink_rows = jnp.broadcast_to(sink.astype(F32)[:, None], (B_HEADS, 128))
    return _attention(q, k, v, sink_rows)


def _hgrn2(q, f_fw, f_bw, i, g, lb, norm_g):
    lb = lb.reshape(1, A_WIDTH)
    o_fw = _hgrn_pass(False, q, f_fw, i, lb)
    return _hgrn_pass(True, q, f_bw, i, lb, (o_fw, g, norm_g.reshape(1, A_WIDTH)))


S5_STEPS = 64
S5_ROWS = S5_STEPS * BATCH
S5_TILES = (CTX_LEN + SEQ) // S5_STEPS
S5_CTX_TILES = CTX_LEN // S5_STEPS
S5_BLOCKS = 4
S5_BLOCK_IN = C_WIDTH // S5_BLOCKS
S5_BLOCK_STATE = C_GROUPS * C_STATE // S5_BLOCKS


def _s5_kernel(reverse, finish, *refs):
    if finish:
        u_ref, a_ref, wb_ref, wc_ref, yf_ref, glu_ref, y_ref, x_ref, st_ref = refs
    else:
        u_ref, a_ref, wb_ref, wc_ref, d_ref, y_ref, x_ref, st_ref = refs
    ns = S5_BLOCK_STATE

    @pl.when(pl.program_id(0) == 0)
    def _():
        st_ref[...] = jnp.zeros_like(st_ref)

    for k in range(S5_BLOCKS):
        x_ref[:, 2 * ns * k:2 * ns * (k + 1)] = _bdot(u_ref[:, S5_BLOCK_IN * k:S5_BLOCK_IN * (k + 1)], wb_ref[k])

    def step(tt, carry):
        t = (S5_STEPS - 1 - tt) if reverse else tt
        rows = pl.ds(pl.multiple_of(t * BATCH, BATCH), BATCH)
        for k in range(S5_BLOCKS):
            re = slice(2 * ns * k, 2 * ns * k + ns)
            im = slice(2 * ns * k + ns, 2 * ns * (k + 1))
            ar, ai = a_ref[:, re], a_ref[:, im]
            sr, si = st_ref[:, re], st_ref[:, im]
            nr = ar * sr - ai * si + x_ref[rows, re]
            ni = ar * si + ai * sr + x_ref[rows, im]
            st_ref[:, re] = nr
            st_ref[:, im] = ni
            x_ref[rows, re] = nr
            x_ref[rows, im] = ni
        return carry

    lax.fori_loop(0, S5_STEPS, step, 0)

    for k in range(S5_BLOCKS):
        cols = slice(S5_BLOCK_IN * k, S5_BLOCK_IN * (k + 1))
        y = _bdot(x_ref[:, 2 * ns * k:2 * ns * (k + 1)], wc_ref[k])
        if finish:
            y_ref[:, cols] = y + yf_ref[:, cols]
        else:
            y_ref[:, cols] = y + d_ref[:, cols] * u_ref[:, cols]
    if finish:
        y = jax.nn.gelu(y_ref[...])
        y_ref[...] = y * _sigmoid(_bdot(y, glu_ref[...]))


def _s5_tile_order(i, reverse):
    if not reverse:
        return i
    return jnp.where(i < S5_CTX_TILES, S5_CTX_TILES - 1 - i, S5_TILES + S5_CTX_TILES - 1 - i)


def _s5_pass(reverse, u_tm, acoef, wb, wc, extra):
    m, w = u_tm.shape
    nstate = 2 * C_GROUPS * C_STATE
    row_spec = pl.BlockSpec((S5_ROWS, w), lambda i: (_s5_tile_order(i, reverse), 0))
    full = lambda a: pl.BlockSpec(a.shape, lambda i: (0,) * a.ndim)
    finish = reverse
    if finish:
        yf, glu_w = extra
        ins, specs = [u_tm, acoef, wb, wc, yf, glu_w], [row_spec, full(acoef), full(wb), full(wc), row_spec, full(glu_w)]
    else:
        (dskip,) = extra
        ins, specs = [u_tm, acoef, wb, wc, dskip], [row_spec, full(acoef), full(wb), full(wc), full(dskip)]
    return pl.pallas_call(
        functools.partial(_s5_kernel, reverse, finish),
        grid=(S5_TILES,),
        in_specs=specs,
        out_specs=row_spec,
        out_shape=jax.ShapeDtypeStruct((m, w), F32),
        scratch_shapes=[pltpu.VMEM((S5_ROWS, nstate), F32), pltpu.VMEM((BATCH, nstate), F32)],
        compiler_params=_params("arbitrary"),
        name="s5_bwd" if reverse else "s5_fwd",
    )(*ins)


def _s5_discretize(lam_re, lam_im, log_dt, b_re, b_im):
    lam_re = jnp.minimum(lam_re.astype(F32), -1e-4)
    lam_im = lam_im.astype(F32)
    dt = jnp.exp(log_dt.astype(F32))[:, None]
    mag = jnp.exp(lam_re * dt)
    ab_re, ab_im = mag * jnp.cos(lam_im * dt), mag * jnp.sin(lam_im * dt)
    den = lam_re ** 2 + lam_im ** 2
    nr = ab_re - 1.0
    co_re = (nr * lam_re + ab_im * lam_im) / den
    co_im = (ab_im * lam_re - nr * lam_im) / den
    bb_re = co_re[..., None] * b_re - co_im[..., None] * b_im
    bb_im = co_re[..., None] * b_im + co_im[..., None] * b_re
    return ab_re, ab_im, bb_re, bb_im


def _s5_tables(lam_re, lam_im, log_dt, b_re, b_im, c_re, c_im):
    eye = jnp.eye(C_GROUPS // S5_BLOCKS, dtype=F32)
    gb = C_GROUPS // S5_BLOCKS

    def in_map(bb):
        return jnp.einsum('kgph,gG->kghGp', bb.reshape(S5_BLOCKS, gb, C_STATE, C_GROUP), eye).reshape(
            S5_BLOCKS, S5_BLOCK_IN, S5_BLOCK_STATE)

    def out_map(cc):
        return jnp.einsum('kghp,gG->kgpGh', cc.reshape(S5_BLOCKS, gb, C_GROUP, C_STATE), eye).reshape(
            S5_BLOCKS, S5_BLOCK_STATE, S5_BLOCK_IN)

    wc = jnp.concatenate([out_map(c_re.astype(F32)), -out_map(c_im.astype(F32))], axis=1).astype(_MXU_DTYPE)
    tables = []
    for d in range(2):
        ab_re, ab_im, bb_re, bb_im = _s5_discretize(lam_re[d], lam_im[d], log_dt[d], b_re.astype(F32), b_im.astype(F32))
        a = jnp.concatenate([ab_re.reshape(S5_BLOCKS, S5_BLOCK_STATE), ab_im.reshape(S5_BLOCKS, S5_BLOCK_STATE)], axis=1)
        acoef = jnp.broadcast_to(a.reshape(1, -1), (BATCH, 2 * C_GROUPS * C_STATE))
        wb = jnp.concatenate([in_map(bb_re), in_map(bb_im)], axis=2).astype(_MXU_DTYPE)
        tables.append((acoef, wb))
    return tables, wc


def _to_time_major(y):
    w = y.shape[1]
    c = y[:CTX_ROWS].reshape(BATCH, CTX_LEN, w).transpose(1, 0, 2).reshape(CTX_ROWS, w)
    l = y[CTX_ROWS:].reshape(BATCH, SEQ, w).transpose(1, 0, 2).reshape(BATCH * SEQ, w)
    return jnp.concatenate([c, l], axis=0)


def _from_time_major(y):
    w = y.shape[1]
    c = y[:CTX_ROWS].reshape(CTX_LEN, BATCH, w).transpose(1, 0, 2).reshape(CTX_ROWS, w)
    l = y[CTX_ROWS:].reshape(SEQ, BATCH, w).transpose(1, 0, 2).reshape(BATCH * SEQ, w)
    return jnp.concatenate([c, l], axis=0)


def _s5(u, lam_re, lam_im, log_dt, b_re, b_im, c_re, c_im, d_skip, glu_w):
    (fw, bw), wc = _s5_tables(lam_re, lam_im, log_dt, b_re, b_im, c_re, c_im)
    u_tm = _to_time_major(u)
    y_fw = _s5_pass(False, u_tm, fw[0], fw[1], wc, (d_skip.astype(F32).reshape(1, C_WIDTH),))
    y = _s5_pass(True, u_tm, bw[0], bw[1], wc, (y_fw, glu_w.astype(_MXU_DTYPE)))
    return _from_time_major(y)


RW_LORA_OFF = 3 * R_WIDTH


def _softplus(z):
    return jnp.maximum(z, 0.0) + jnp.log1p(jnp.exp(-jnp.abs(z)))


def _rwkv_prep_kernel(p_ref, hp_ref, hn_ref, mu_ref, w0_ref, w2_ref, a0_ref, a2_ref, g2_ref, kk_ref, ka_ref, rk_ref,
                      ones_ref, r_o, v_o, g_o, bonus_o, kkn_o, lwf_o, kdf_o, bf_o, lwb_o, kdb_o, bb_o):
    i = pl.program_id(1)
    x = p_ref[...]
    rows = x.shape[0]
    rowi = lax.broadcasted_iota(jnp.int32, (rows, 1), 0)
    prev_row = jnp.where(i >= 2, hp_ref[7:8, :], 0.0)
    next_row = jnp.where((i >= 1) & (i < SEQ_TILES - 1), hn_ref[0:1, :], 0.0)
    prev = jnp.where(rowi == 0, prev_row, pltpu.roll(x, 1, 0))
    nxt = jnp.where(rowi == rows - 1, next_row, pltpu.roll(x, rows - 1, 0))
    x = x + mu_ref[...] * (0.5 * (prev + nxt) - x)

    r = x[:, 0:R_WIDTH]
    k = x[:, R_WIDTH:2 * R_WIDTH]
    v = x[:, 2 * R_WIDTH:3 * R_WIDTH]
    wd = x[:, RW_LORA_OFF:RW_LORA_OFF + 128]
    ad = x[:, RW_LORA_OFF + 128:RW_LORA_OFF + 256]
    gd = x[:, RW_LORA_OFF + 256:RW_LORA_OFF + 384]
    ones = ones_ref[...]

    r_o[...] = r
    v_o[...] = v
    g_o[...] = _bdot(_sigmoid(gd), g2_ref[...])
    bonus_o[...] = _dot_by_sel(r * k * rk_ref[...], ones) * v
    kk = k * kk_ref[...]
    kkn = kk * jnp.minimum(lax.rsqrt(_dot_by_sel(kk * kk, ones)), 1e12)
    kkn_o[...] = kkn
    tw = jnp.tanh(wd)
    for d, (lw_o, kd_o, b_o) in enumerate(((lwf_o, kdf_o, bf_o), (lwb_o, kdb_o, bb_o))):
        w = -_softplus(-(w0_ref[d:d + 1, :] + _bdot(tw, w2_ref[d]))) - 0.5
        lw_o[...] = -jnp.exp(w)
        a = _sigmoid(a0_ref[d:d + 1, :] + _bdot(ad, a2_ref[d]))
        kd_o[...] = k * (1.0 + (a - 1.0) * ka_ref[...])
        b_o[...] = kkn * a


def _rwkv_prep(p, mu, w0, w2pad, a0, a2pad, g2, k_k, k_a, r_k, ones_blk):
    m, w = p.shape
    hb = ROW_TILE // 8
    seq = lambda width: pl.BlockSpec((ROW_TILE, width), lambda b, i: (_seq_block(b, i), 0))
    halo_prev = pl.BlockSpec((8, w), lambda b, i: (jnp.maximum(_seq_block(b, i) * hb - 1, 0), 0))
    halo_next = pl.BlockSpec((8, w), lambda b, i: (jnp.minimum((_seq_block(b, i) + 1) * hb, m // 8 - 1), 0))
    full = lambda a: pl.BlockSpec(a.shape, lambda b, i: (0,) * a.ndim)
    consts = [mu, w0, w2pad, a0, a2pad, g2, k_k, k_a, r_k, ones_blk]
    return pl.pallas_call(
        _rwkv_prep_kernel,
        grid=(BATCH, SEQ_TILES),
        in_specs=[seq(w), halo_prev, halo_next] + [full(c) for c in consts],
        out_specs=[seq(R_WIDTH)] * 11,
        out_shape=[jax.ShapeDtypeStruct((m, R_WIDTH), F32)] * 11,
        compiler_params=_params("parallel", "parallel"),
        name="rwkv_prep",
    )(p, p, p, *consts)


RW_CHUNK = 64
RW_SEQS = 4


def _rwkv_scan_kernel(reverse, finish, *refs):
    per_seq = 9 if finish else 6
    seq_refs = [refs[q * per_seq:(q + 1) * per_seq] for q in range(RW_SEQS)]
    rest = refs[RW_SEQS * per_seq:]
    if finish:
        lng_ref, lnb_ref = rest[:2]
        rest = rest[2:]
    o_refs, st_ref = rest[:RW_SEQS], rest[RW_SEQS]
    C = RW_CHUNK
    P = 2 * R_HEAD
    n_chunks = ROW_TILE // C

    @pl.when(pl.program_id(1) == 0)
    def _():
        st_ref[...] = jnp.zeros_like(st_ref)

    row = lax.broadcasted_iota(jnp.int32, (C, C), 0)
    col = lax.broadcasted_iota(jnp.int32, (C, C), 1)
    incl = (row <= col) if reverse else (row >= col)
    strict = (row < col) if reverse else (row > col)
    tri = incl.astype(F32)
    row2 = lax.broadcasted_iota(jnp.int32, (C, 2 * C), 0)
    col2 = lax.broadcasted_iota(jnp.int32, (C, 2 * C), 1)
    s2 = jnp.where(col2 < C, col2, col2 - C)
    incl2 = (row2 <= s2) if reverse else (row2 >= s2)
    strict_k = ((row2 < s2) if reverse else (row2 > s2)) & (col2 >= C)
    left = lax.broadcasted_iota(jnp.int32, (C, P), 1) < R_HEAD
    left2 = lax.broadcasted_iota(jnp.int32, (2 * C, P), 1) < R_HEAD
    blockdiag = ((lax.broadcasted_iota(jnp.int32, (P, P), 0) < R_HEAD)
                 == (lax.broadcasted_iota(jnp.int32, (P, P), 1) < R_HEAD))
    steps = int(math.log2(C))

    def chunk(ci, carry):
        c = (n_chunks - 1 - ci) if reverse else ci
        rows = pl.ds(pl.multiple_of(c * C, C), C)
        npair = R_HEADS // 2
        pairs = range(RW_SEQS * npair)
        heads = [(p, half) for p in pairs for half in range(2)]
        lane_sl = [slice((p % npair) * P, (p % npair + 1) * P) for p in pairs]
        sts = [st_ref[p] for p in pairs]
        vps, rhss, ars, tails, decays, fin = [], [], [], [], [], []
        for q in range(RW_SEQS):
            r_ref, kd_ref, v_ref, lw_ref, kkn_ref, b_ref = seq_refs[q][:6]
            lw = lw_ref[rows, :]
            kd, bv, v_all = kd_ref[rows, :], b_ref[rows, :], v_ref[rows, :]
            gi = _dot_sel(tri, lw)
            g_tot = gi[0:1] if reverse else gi[C - 1:C]
            inv = jnp.exp(-gi)
            a_t = -kkn_ref[rows, :] * jnp.exp(gi - lw)
            r_t = r_ref[rows, :] * jnp.exp(gi)
            b_t, k_t = bv * inv, kd * inv
            tail = jnp.exp(g_tot - gi)
            b_tail, k_tail = bv * tail, kd * tail
            decay = jnp.exp(g_tot)
            for j in range(npair):
                sl = lane_sl[j]
                vps.append(v_all[:, sl])
                rhss.append(jnp.concatenate([b_t[:, sl], k_t[:, sl], sts[q * npair + j]], axis=0).astype(_MXU_DTYPE))
                ars.append(jnp.concatenate([a_t[:, sl], r_t[:, sl]], axis=0))
                tails.append(jnp.concatenate([b_tail[:, sl], k_tail[:, sl]], axis=0))
                decays.append(decay[:, sl])
                if finish:
                    of_ref, g_ref, bonus_ref = seq_refs[q][6:9]
                    fin.append((of_ref[rows, sl], g_ref[rows, sl], bonus_ref[rows, sl]))
        zvs = [jnp.concatenate([jnp.zeros((C, P), F32), vp], axis=0).astype(_MXU_DTYPE) for vp in vps]
        prods = [_dot(jnp.where(left2 if half == 0 else jnp.logical_not(left2), ars[p], 0.0).astype(_MXU_DTYPE),
                      rhss[p], trans_b=True) for p, half in heads]
        ahs = [prod[:, 2 * C:] for prod in prods]
        us = [ah[:C] + _dot(jnp.where(strict_k, prod[:C, :2 * C], 0.0).astype(_MXU_DTYPE), zvs[p])
              for (p, half), prod, ah in zip(heads, prods, ahs)]
        npows = [jnp.where(strict, prod[:C, :C], 0.0) for prod in prods]
        for it in range(steps):
            us = [u + _bdot(npow, u) for u, npow in zip(us, npows)]
            if it + 1 < steps:
                npows = [_bdot(npow, npow) for npow in npows]
        os_ = [ah[C:] + _bdot(jnp.where(incl2, prod[C:, :2 * C], 0.0), jnp.concatenate([u, vps[p]], axis=0))
               for (p, half), prod, ah, u in zip(heads, prods, ahs, us)]
        outs = []
        for p in pairs:
            sl, st, vp = lane_sl[p], sts[p], vps[p]
            u_pair = jnp.where(left, us[2 * p], us[2 * p + 1])
            o = jnp.where(left, os_[2 * p], os_[2 * p + 1])
            upd = _bdot(jnp.concatenate([u_pair, vp], axis=0), tails[p], trans_a=True)
            st_ref[p] = jnp.where(blockdiag, st * decays[p] + upd, 0.0)
            if finish:
                of_p, g_p, bonus_p = fin[p]
                o = o + of_p

                def head_mean(t):
                    tot = jnp.sum(t, axis=1, keepdims=True)
                    lsum = jnp.sum(jnp.where(left, t, 0.0), axis=1, keepdims=True)
                    return jnp.where(left, lsum, tot - lsum) * (1.0 / R_HEAD)

                oc = o - head_mean(o)
                o = oc * lax.rsqrt(head_mean(oc * oc) + RWKV_GN_EPS) * lng_ref[:, sl] + lnb_ref[:, sl]
                o = (o + bonus_p) * g_p
            outs.append(o)
        for p, o in enumerate(outs):
            o_refs[p // npair][rows, lane_sl[p]] = o
        return carry

    lax.fori_loop(0, n_chunks, chunk, 0)


def _rwkv_scan(reverse, r, kd, v, lw, kkn, bvec, finish_args=None):
    m, w = r.shape
    nb = BATCH // RW_SEQS

    def seq_spec(q):
        return pl.BlockSpec((ROW_TILE, w), lambda b, t: (_seq_block(b + q * nb, _tile_order(t, reverse)), 0))

    def part_block(b, i):
        return jnp.where(i == 0, b, nb + b * (SEQ // ROW_TILE) + i - 1)

    part_spec = pl.BlockSpec((ROW_TILE, w), lambda b, t: (part_block(b, _tile_order(t, reverse)), 0))
    vec_spec = pl.BlockSpec((1, w), lambda b, t: (0, 0))
    ins, specs = [], []
    for q in range(RW_SEQS):
        ins += [r, kd, v, lw, kkn, bvec]
        specs += [seq_spec(q)] * 6
        if finish_args is not None:
            o_parts, g, bonus = finish_args[:3]
            ins += [o_parts[q], g, bonus]
            specs += [part_spec, seq_spec(q), seq_spec(q)]
    if finish_args is not None:
        ins += list(finish_args[3:])
        specs += [vec_spec, vec_spec]
    return pl.pallas_call(
        functools.partial(_rwkv_scan_kernel, reverse, finish_args is not None),
        grid=(nb, SEQ_TILES),
        in_specs=specs,
        out_specs=[part_spec] * RW_SEQS,
        out_shape=[jax.ShapeDtypeStruct((m // RW_SEQS, w), F32)] * RW_SEQS,
        scratch_shapes=[pltpu.VMEM((RW_SEQS * R_HEADS // 2, 2 * R_HEAD, 2 * R_HEAD), F32)],
        compiler_params=_params("parallel", "arbitrary"),
        name="rwkv_bwd" if reverse else "rwkv_fwd",
    )(*ins)


def _merge_seq_parts(parts):
    c = CTX_ROWS // RW_SEQS
    return jnp.concatenate([p[:c] for p in parts] + [p[c:] for p in parts], axis=0)


def _rwkv7(p, mu, w0, w2, a0, a2, g2, k_k, k_a, r_k, lnx_g, lnx_b):
    zeros_w = jnp.zeros((R_LORA_W, R_WIDTH), F32)
    w2pad = jnp.stack([jnp.concatenate([w2[0], zeros_w], 0), jnp.concatenate([zeros_w, w2[1]], 0)]).astype(_MXU_DTYPE)
    a2pad = jnp.stack([jnp.concatenate([a2[0], zeros_w], 0), jnp.concatenate([zeros_w, a2[1]], 0)]).astype(_MXU_DTYPE)
    head = np.arange(R_WIDTH) // R_HEAD
    ones_blk = jnp.asarray(head[:, None] == head[None, :], F32)
    row = lambda t: t.astype(F32).reshape(1, -1)
    r, v, g, bonus, kkn, lwf, kdf, bf, lwb, kdb, bb = _rwkv_prep(
        p, row(mu), w0.astype(F32), w2pad, a0.astype(F32), a2pad, g2.astype(_MXU_DTYPE),
        row(k_k), row(k_a), row(r_k), ones_blk)
    o_fw = _rwkv_scan(False, r, kdf, v, lwf, kkn, bf)
    return _merge_seq_parts(_rwkv_scan(True, r, kdb, v, lwb, kkn, bb, (o_fw, g, bonus, row(lnx_g), row(lnx_b))))


PROJ_TM = 256
OUT_TM = 512
FFN_TM = 512
EVEN_SPLITS = (A_WIDTH,) * 5 + (B_QW, B_KVW, B_KVW)
EVEN_ROTARY = (5, 6)
ODD_SPLITS = (C_WIDTH, R_IN)


def _mod_tiles(vec9, tm):
    idx = np.concatenate([np.full(CTX_ROWS // tm, BATCH), np.repeat(np.arange(BATCH), SEQ // tm)])
    return vec9[idx][:, None, :]


def kernel(x, c, ctx, c_ctx, ada_w, ada_b, ln_g, ln_b, ev_w_in, ev_w_out, hg_lb, hg_norm_g, attn_sink, ffn_w_gate, ffn_w_up, ffn_w_down, od_w_in, od_w_out, s5_lam_re, s5_lam_im, s5_log_dt, s5_b_re, s5_b_im, s5_c_re, s5_c_im, s5_d, s5_glu_w, rwkv_mu, rwkv_w0, rwkv_w2, rwkv_a0, rwkv_a2, rwkv_g2, rwkv_k_k, rwkv_k_a, rwkv_r_k, rwkv_ln_g, rwkv_ln_b, moe_router_w, moe_router_b, moe_w_gate, moe_w_up, moe_w_down):
    d = D_MODEL
    xs = jnp.concatenate([ctx.reshape(CTX_ROWS, d), x.reshape(BATCH * SEQ, d)], axis=0).astype(F32)
    cond = jnp.concatenate([c, c_ctx[None, :], jnp.zeros((16 - BATCH - 1, d), F32)], axis=0)
    ada = _ada_all(cond, ada_w, ada_b)
    lb_soft = jax.nn.softmax(hg_lb.astype(F32), axis=0)
    lb_all = jnp.cumsum(lb_soft, axis=0) - lb_soft[0:1]
    rope_tables = _rope_tables()
    ffn_w = [w.astype(BF16) for w in (ffn_w_gate, ffn_w_up, ffn_w_down)]
    moe_w = [w.astype(BF16) for w in (moe_w_gate, moe_w_up, moe_w_down)]

    for layer in range(DEPTH):
        j = layer // 2
        sh1, sc1, gt1, sh2, sc2, gt2 = [ada[layer, :BATCH + 1, n * d:(n + 1) * d] for n in range(6)]
        if layer % 2 == 0:
            q, f_fw, f_bw, i_in, g, aq, ak, av = _mod_matmul(
                xs, _mod_tiles(1.0 + sc1, PROJ_TM), _mod_tiles(sh1, PROJ_TM), ev_w_in[j].astype(BF16), EVEN_SPLITS, PROJ_TM,
                rotary=EVEN_ROTARY, rope_tables=rope_tables)
            y1 = _hgrn2(q, f_fw, f_bw, i_in, g, lb_all[j], hg_norm_g[j])
            y2 = _window_attention(aq, ak, av, attn_sink[j])
            w_out = ev_w_out[j]
        else:
            u, p_rw = _mod_matmul(
                xs, _mod_tiles(1.0 + sc1, OUT_TM), _mod_tiles(sh1, OUT_TM), od_w_in[j].astype(BF16), ODD_SPLITS, OUT_TM)
            y1 = _s5(u, s5_lam_re[j], s5_lam_im[j], s5_log_dt[j], s5_b_re[j], s5_b_im[j], s5_c_re[j], s5_c_im[j],
                     s5_d[j], s5_glu_w[j])
            y2 = _rwkv7(p_rw, rwkv_mu[j], rwkv_w0[j], rwkv_w2[j], rwkv_a0[j], rwkv_a2[j], rwkv_g2[j],
                        rwkv_k_k[j], rwkv_k_a[j], rwkv_r_k[j], rwkv_ln_g[j], rwkv_ln_b[j])
            w_out = od_w_out[j]
        skip = CTX_ROWS if layer == DEPTH - 1 else 0
        tiles = lambda vec9, tm: _mod_tiles(vec9, tm)[skip // tm:]
        xs = _out_proj_ln(y1, y2, w_out.astype(BF16), xs, tiles(gt1, OUT_TM), ln_g[layer, 0], ln_b[layer, 0],
                          OUT_TM, skip_rows=skip)
        scale2, shift2, gate2 = tiles(1.0 + sc2, FFN_TM), tiles(sh2, FFN_TM), tiles(gt2, FFN_TM)
        if layer % 2 == 0:
            xs = _ffn_ln(xs, scale2, shift2, gate2, *ffn_w, j, ln_g[layer, 1], ln_b[layer, 1], FFN_TM)
        else:
            w_pad = jnp.pad(moe_router_w[j].astype(F32), ((0, 0), (0, 128 - N_EXPERTS)))
            b_pad = jnp.pad(moe_router_b[j].astype(F32), (0, 128 - N_EXPERTS)).reshape(1, 128)
            route, sel, h = _router(xs, scale2, shift2, w_pad, b_pad, FFN_TM)
            pos1, pos2, w1, w2, tile_expert, n_used, pad_lo, pad_hi = _moe_plan(route, sel)
            hs = _moe_dispatch(h, pos1, pos2, pad_lo, pad_hi, n_used)
            y_sorted = _moe_experts(hs, tile_expert, n_used, *moe_w, j)
            xs = _moe_combine_ln(y_sorted, pos1, pos2, w1, w2, xs, tiles(gt2, MOE_COMBINE_TM),
                                 ln_g[layer, 1], ln_b[layer, 1])
    return xs.reshape(BATCH, SEQ, d)
```
